```python
import jax, jax.numpy as jnp
from jax import lax
import numpy as np

D_MODEL = 2048
BATCH = 1
SEQ = 16384
DEPTH = 1
DEC_BATCH = 16
DEC_SEQ = 64
PAST_LEN = 4096

CHUNK = 64
SB_HEADS = 8
SB_HEAD_DIM = 128
SB_WIDTH = SB_HEADS * SB_HEAD_DIM
SB_BLOCK = 128
HG_HEADS = 8
HG_KEY_DIM = 128
HG_VAL_DIM = 128
HG_KEY_WIDTH = HG_HEADS * HG_KEY_DIM
HG_WIDTH = HG_HEADS * HG_VAL_DIM
HG_BLOCK = 16
EPS = 1e-6
IN_WIDTHS = (SB_WIDTH, SB_WIDTH, SB_WIDTH, SB_WIDTH,
             HG_KEY_WIDTH, HG_WIDTH, HG_KEY_WIDTH, HG_WIDTH,
             D_MODEL, D_MODEL)
IN_TOTAL = sum(IN_WIDTHS)
IN_SPLITS = tuple(int(s) for s in np.cumsum(IN_WIDTHS)[:-1])

kernel_name = "stickbreaking_hgrn2_gated_streaming_step"


def _rms(x):
    xf = x.astype(jnp.float32)
    return xf * lax.rsqrt(jnp.mean(xf * xf, axis=-1, keepdims=True) + EPS)


def _layer_inputs(x, c, norm_gain, w_ada, b_ada, w_in, q_gain, k_gain, lb):
    N, T, _ = x.shape
    dt = x.dtype
    mod = jax.nn.silu(c) @ w_ada + b_ada
    shift, scale, gate = jnp.split(mod.astype(jnp.float32), 3, axis=-1)
    h = (_rms(x) * norm_gain.astype(jnp.float32) * (1.0 + scale[:, None]) + shift[:, None]).astype(dt)
    q_sb, k_sb, v_sb, z_sb, f_h, i_h, q_h, z_h, g_sb, g_h = jnp.split(h @ w_in, IN_SPLITS, axis=-1)
    sb_heads = lambda a: a.reshape(N, T, SB_HEADS, SB_HEAD_DIM)
    q_sb = (_rms(sb_heads(q_sb)) * q_gain.astype(jnp.float32)).astype(dt)
    k_sb = (_rms(sb_heads(k_sb)) * k_gain.astype(jnp.float32)).astype(dt)
    v_sb = sb_heads(v_sb)
    f = lb + (1.0 - lb) * jax.nn.sigmoid(f_h.astype(jnp.float32))
    logf = jnp.log(f).reshape(N, T, HG_HEADS, HG_KEY_DIM)
    k_h = (1.0 - f).reshape(N, T, HG_HEADS, HG_KEY_DIM)
    q_h = jax.nn.silu(q_h.astype(jnp.float32)).reshape(N, T, HG_HEADS, HG_KEY_DIM)
    i_h = i_h.astype(jnp.float32).reshape(N, T, HG_HEADS, HG_VAL_DIM)
    return q_sb, k_sb, v_sb, z_sb, q_h, k_h, i_h, logf, z_h, g_sb, g_h, gate


def _sb_attend(q, q_pos, k, v, k_pos):
    z = jnp.einsum('nqhd,nkhd->nhqk', q, k).astype(jnp.float32) * SB_HEAD_DIM ** -0.5
    causal = k_pos[None, :] < q_pos[:, None]
    log_1m = jnp.where(causal, jax.nn.log_sigmoid(-z), 0.0)
    rev = lax.cumsum(log_1m, axis=3, reverse=True) - log_1m
    w = jnp.where(causal, jnp.exp(jax.nn.log_sigmoid(z) + rev), 0.0)
    return jnp.einsum('nhqk,nkhd->nqhd', w.astype(v.dtype), v)


def _sb_prompt(q, k, v):
    N, T, H, Dh = q.shape
    nb = T // SB_BLOCK
    qb = jnp.moveaxis(q.reshape(N, nb, SB_BLOCK, H, Dh), 1, 0)
    pos = jnp.arange(T).reshape(nb, SB_BLOCK)
    k_pos = jnp.arange(T)
    o = lax.map(lambda a: _sb_attend(a[0], a[1], k, v, k_pos), (qb, pos))
    return jnp.moveaxis(o, 0, 1).reshape(N, T, H, Dh)


def _hgrn2(q, k, v, logf, s0):
    N, T = q.shape[:2]
    pad = (-T) % HG_BLOCK
    padw = ((0, 0), (0, pad), (0, 0), (0, 0))
    q, k, v, logf = (jnp.pad(a, padw) for a in (q, k, v, logf))
    nb = (T + pad) // HG_BLOCK
    blk = lambda a: a.reshape(N, nb, HG_BLOCK, *a.shape[2:])
    q, k, v, logf = blk(q), blk(k), blk(v), blk(logf)
    b = jnp.cumsum(logf, axis=2)
    b_last = b[:, :, -1:]
    q_dec = q * jnp.exp(b)
    k_inv = k * jnp.exp(-b)
    k_end = k * jnp.exp(b_last - b)
    decay = jnp.exp(b_last[:, :, 0])
    tril = jnp.tril(jnp.ones((HG_BLOCK, HG_BLOCK), dtype=bool))
    att = jnp.where(tril, jnp.einsum('ncthd,ncshd->nchts', q_dec, k_inv), 0.0)
    intra = jnp.einsum('nchts,ncshv->ncthv', att, v)

    def step(S, xs):
        qd, ke, vv, dec = xs
        inter = jnp.einsum('nthd,nhdv->nthv', qd, S)
        S = S * dec[..., None] + jnp.einsum('nthd,nthv->nhdv', ke, vv)
        return S, inter

    s_fin, inter = lax.scan(step, s0, (jnp.moveaxis(q_dec, 1, 0), jnp.moveaxis(k_end, 1, 0),
                                       jnp.moveaxis(v, 1, 0), jnp.moveaxis(decay, 1, 0)))
    o = intra + jnp.moveaxis(inter, 0, 1)
    return o.reshape(N, nb * HG_BLOCK, *o.shape[3:])[:, :T], s_fin


def _layer_output(x, o_sb, o_h, z_sb, z_h, g_sb, g_h, gate, onorm_gain, w_br_sb, w_br_hg, w_out):
    N, T, _ = x.shape
    dt = x.dtype
    y_sb = (o_sb.reshape(N, T, SB_WIDTH).astype(jnp.float32) * jax.nn.silu(z_sb.astype(jnp.float32))).astype(dt) @ w_br_sb
    o_h = (_rms(o_h) * onorm_gain.astype(jnp.float32)).reshape(N, T, HG_WIDTH)
    y_h = (o_h * jax.nn.silu(z_h.astype(jnp.float32))).astype(dt) @ w_br_hg
    merged = jax.nn.sigmoid(g_sb) * y_sb + jax.nn.sigmoid(g_h) * y_h
    return (x.astype(jnp.float32) + gate[:, None] * (merged @ w_out).astype(jnp.float32)).astype(dt)


def setup_inputs(seed: int = 0) -> dict:
    key = jax.random.key(seed)
    ks = jax.random.split(key, 20)
    nrm = lambda k, shape: jax.random.normal(k, shape, jnp.float32)
    return {
        "x_prompt": nrm(ks[0], (BATCH, SEQ, D_MODEL)),
        "x_sample": nrm(ks[1], (DEC_BATCH, DEC_SEQ, D_MODEL)),
        "cache_sb_k": nrm(ks[2], (DEPTH, DEC_BATCH, PAST_LEN, SB_HEADS, SB_HEAD_DIM)),
        "cache_sb_v": nrm(ks[3], (DEPTH, DEC_BATCH, PAST_LEN, SB_HEADS, SB_HEAD_DIM)),
        "state_hgrn": 0.5 * nrm(ks[4], (DEPTH, DEC_BATCH, HG_HEADS, HG_KEY_DIM, HG_VAL_DIM)),
        "c_prompt": nrm(ks[5], (BATCH, D_MODEL)),
        "c_sample": nrm(ks[6], (DEC_BATCH, D_MODEL)),
        "norm_gain": 1.0 + 0.02 * nrm(ks[7], (DEPTH, D_MODEL)),
        "w_ada": 0.5 * D_MODEL ** -0.5 * nrm(ks[8], (DEPTH, D_MODEL, 3 * D_MODEL)),
        "b_ada": 0.02 * nrm(ks[9], (DEPTH, 3 * D_MODEL)),
        "w_in": D_MODEL ** -0.5 * nrm(ks[10], (DEPTH, D_MODEL, IN_TOTAL)),
        "q_norm_gain": 1.0 + 0.02 * nrm(ks[11], (DEPTH, SB_HEAD_DIM)),
        "k_norm_gain": 1.0 + 0.02 * nrm(ks[12], (DEPTH, SB_HEAD_DIM)),
        "hgrn_lb_raw": 0.1 * nrm(ks[13], (DEPTH + 1, HG_KEY_WIDTH)),
        "hgrn_onorm_gain": 1.0 + 0.02 * nrm(ks[14], (DEPTH, HG_HEADS, HG_VAL_DIM)),
        "w_branch_sb": SB_WIDTH ** -0.5 * nrm(ks[15], (DEPTH, SB_WIDTH, D_MODEL)),
        "w_branch_hgrn": HG_WIDTH ** -0.5 * nrm(ks[16], (DEPTH, HG_WIDTH, D_MODEL)),
        "w_out": D_MODEL ** -0.5 * nrm(ks[17], (DEPTH, D_MODEL, D_MODEL)),
    }


def reference(x_prompt, x_sample, cache_sb_k, cache_sb_v, state_hgrn, c_prompt, c_sample,
              norm_gain, w_ada, b_ada, w_in, q_norm_gain, k_norm_gain, hgrn_lb_raw,
              hgrn_onorm_gain, w_branch_sb, w_branch_hgrn, w_out):
    lbs = jnp.cumsum(jax.nn.softmax(hgrn_lb_raw.astype(jnp.float32), axis=0), axis=0)
    xp, xs = x_prompt, x_sample
    kp, vp, sp, ksm, vsm, ssm = [], [], [], [], [], []
    n_p, t_p = xp.shape[:2]
    t_s = xs.shape[1]
    past = cache_sb_k.shape[2]
    for l in range(DEPTH):
        w_i = (norm_gain[l], w_ada[l], b_ada[l], w_in[l], q_norm_gain[l], k_norm_gain[l], lbs[l])
        w_o = (hgrn_onorm_gain[l], w_branch_sb[l], w_branch_hgrn[l], w_out[l])
        q_sb, k_sb, v_sb, z_sb, q_h, k_h, i_h, logf, z_h, g_sb, g_h, gate = _layer_inputs(xp, c_prompt, *w_i)
        o_sb = _sb_prompt(q_sb, k_sb, v_sb)
        s0 = jnp.zeros((n_p, HG_HEADS, HG_KEY_DIM, HG_VAL_DIM), jnp.float32)
        o_h, s_p = _hgrn2(q_h, k_h, i_h, logf, s0)
        xp = _layer_output(xp, o_sb, o_h, z_sb, z_h, g_sb, g_h, gate, *w_o)
        kp.append(k_sb)
        vp.append(v_sb)
        sp.append(s_p.astype(x_prompt.dtype))
        q_sb, k_sb, v_sb, z_sb, q_h, k_h, i_h, logf, z_h, g_sb, g_h, gate = _layer_inputs(xs, c_sample, *w_i)
        k_all = jnp.concatenate([cache_sb_k[l].astype(k_sb.dtype), k_sb], axis=1)
        v_all = jnp.concatenate([cache_sb_v[l].astype(v_sb.dtype), v_sb], axis=1)
        q_pos = past + jnp.arange(t_s)
        k_pos = jnp.arange(past + t_s)
        o_sb = _sb_attend(q_sb, q_pos, k_all, v_all, k_pos)
        o_h, s_s = _hgrn2(q_h, k_h, i_h, logf, state_hgrn[l].astype(jnp.float32))
        xs = _layer_output(xs, o_sb, o_h, z_sb, z_h, g_sb, g_h, gate, *w_o)
        ksm.append(k_sb)
        vsm.append(v_sb)
        ssm.append(s_s.astype(state_hgrn.dtype))
    new_k_prompt = jnp.stack(kp)
    new_v_prompt = jnp.stack(vp)
    new_s_prompt = jnp.stack(sp)
    new_k_sample = jnp.stack(ksm)
    new_v_sample = jnp.stack(vsm)
    new_s_sample = jnp.stack(ssm)
    return (xp, xs, new_k_prompt, new_v_prompt, new_s_prompt, new_k_sample, new_v_sample, new_s_sample)
```

```python
import functools

import jax
import jax.numpy as jnp
from jax import lax
from jax.experimental import pallas as pl
from jax.experimental.pallas import tpu as pltpu

F32 = jnp.float32
BF16 = jnp.bfloat16

N_HEADS = 8
HEAD_DIM = 128
WIDTH = N_HEADS * HEAD_DIM
HG_SUB = 16
EPS = 1e-6
SB_BLOCK = 128
SB_LOG_CUTOFF = -88.0
VMEM_LIMIT = 56 * 1024 * 1024


def _cparams(*sem):
    return pltpu.CompilerParams(dimension_semantics=sem, vmem_limit_bytes=VMEM_LIMIT)


def _silu(x):
    return x * jax.nn.sigmoid(x)


def _ada_kernel(c_ref, w_ref, b_ref, o_ref):
    c = c_ref[...]
    a = _silu(c).astype(BF16)
    o_ref[...] = jnp.dot(a, w_ref[...].astype(BF16), preferred_element_type=F32) + b_ref[...]


def _ada_call(c, w, b):
    r, d = c.shape
    n = w.shape[1]
    tn = 1024
    return pl.pallas_call(
        _ada_kernel,
        grid=(n // tn,),
        in_specs=[pl.BlockSpec((r, d), lambda j: (0, 0)),
                  pl.BlockSpec((d, tn), lambda j: (0, j)),
                  pl.BlockSpec((1, tn), lambda j: (0, j))],
        out_specs=pl.BlockSpec((r, tn), lambda j: (0, j)),
        out_shape=jax.ShapeDtypeStruct((r, n), F32),
        compiler_params=_cparams("arbitrary"),
        name="ada_mod",
    )(c, w, b)


def _prenorm_kernel(x_ref, g_ref, sc_ref, sh_ref, h_ref):
    x = x_ref[...]
    ms = jnp.mean(x * x, axis=-1, keepdims=True)
    xn = x * lax.rsqrt(ms + EPS)
    h = xn * g_ref[...] * (1.0 + sc_ref[...]) + sh_ref[...]
    h_ref[...] = h.astype(BF16)


def _prenorm_call(x, gain, scale, shift, nb, tr):
    n, t, d = x.shape
    vec = pl.BlockSpec((nb, 1, d), lambda i, j: (i, 0, 0))
    return pl.pallas_call(
        _prenorm_kernel,
        grid=(n // nb, t // tr),
        in_specs=[pl.BlockSpec((nb, tr, d), lambda i, j: (i, j, 0)),
                  pl.BlockSpec((1, 1, d), lambda i, j: (0, 0, 0)), vec, vec],
        out_specs=pl.BlockSpec((nb, tr, d), lambda i, j: (i, j, 0)),
        out_shape=jax.ShapeDtypeStruct((n, t, d), BF16),
        compiler_params=_cparams("arbitrary", "arbitrary"),
        name="prenorm",
    )(x, gain, scale, shift)


def _head_rms(y, gain):
    outs = []
    for g in range(N_HEADS):
        yh = y[:, g * HEAD_DIM:(g + 1) * HEAD_DIM]
        ms = jnp.mean(yh * yh, axis=-1, keepdims=True)
        outs.append(yh * lax.rsqrt(ms + EPS) * gain)
    return outs


def _proj_kernel(*refs, kind):
    h_ref, w_ref = refs[0], refs[1]
    wb_ref = refs[-1]

    @pl.when(pl.program_id(1) == 0)
    def _():
        wb_ref[...] = w_ref[...].astype(BF16)

    y = jnp.dot(h_ref[...], wb_ref[...], preferred_element_type=F32)
    if kind == "plain":
        refs[2][...] = y
    elif kind == "silu":
        refs[2][...] = _silu(y)
    elif kind == "copy2":
        refs[2][...] = y
        refs[3][...] = y.astype(BF16)
    elif kind == "norm_q":
        gain = refs[2][...]
        for g, o in enumerate(_head_rms(y, gain)):
            refs[3][:, g * HEAD_DIM:(g + 1) * HEAD_DIM] = o.astype(BF16)
    elif kind == "norm_k":
        gain = refs[2][...]
        for g, o in enumerate(_head_rms(y, gain)):
            refs[3][:, g * HEAD_DIM:(g + 1) * HEAD_DIM] = o
            refs[4][:, g * HEAD_DIM:(g + 1) * HEAD_DIM] = o.astype(BF16)
    elif kind == "forget":
        raw = refs[2][...]
        e = jnp.exp(raw - jnp.max(raw, axis=0, keepdims=True))
        lb = e[0:1, :] / jnp.sum(e, axis=0, keepdims=True)
        f = lb + (1.0 - lb) * jax.nn.sigmoid(y)
        refs[3][...] = jnp.log(f)
        refs[4][...] = 1.0 - f
    else:
        raise ValueError(kind)


def _proj_call(h, w_in, col0, ncols, kind, extra=(), *, tm=512, tn=1024):
    rows, d = h.shape
    assert col0 % tn == 0 and ncols % tn == 0 and rows % tm == 0
    jb = col0 // tn
    grid = (ncols // tn, rows // tm)
    tile = lambda: pl.BlockSpec((tm, tn), lambda j, i: (i, j))
    in_specs = [pl.BlockSpec((tm, d), lambda j, i: (i, 0)),
                pl.BlockSpec((d, tn), lambda j, i: (0, jb + j))]
    for e in extra:
        in_specs.append(pl.BlockSpec(e.shape, lambda j, i: (0, 0)))
    if kind in ("plain", "silu"):
        out_dt = (F32,)
    elif kind == "norm_q":
        out_dt = (BF16,)
    elif kind in ("copy2", "norm_k"):
        out_dt = (F32, BF16)
    else:
        out_dt = (F32, F32)
    outs = pl.pallas_call(
        functools.partial(_proj_kernel, kind=kind),
        grid=grid,
        in_specs=in_specs,
        out_specs=[tile() for _ in out_dt],
        out_shape=[jax.ShapeDtypeStruct((rows, ncols), dt) for dt in out_dt],
        scratch_shapes=[pltpu.VMEM((d, tn), BF16)],
        compiler_params=_cparams("arbitrary", "arbitrary"),
        name="proj_" + kind,
    )(h, w_in, *extra)
    return outs


def _suffix_matrix(bk):
    j = lax.broadcasted_iota(jnp.int32, (bk, 2 * bk), 0)
    s = lax.broadcasted_iota(jnp.int32, (bk, 2 * bk), 1)
    return jnp.where((j > s) | (s >= bk), 1.0, 0.0).astype(BF16)


def _sb_tile(q, k, v, carry, sfx, mask):
    bk = k.shape[0]
    z = lax.dot_general(q, k, (((1,), (1,)), ((), ())), preferred_element_type=F32) * HEAD_DIM ** -0.5
    sp = jnp.maximum(z, 0.0) + jnp.log1p(jnp.exp(-jnp.abs(z)))
    l1m = -sp
    if mask is not None:
        l1m = jnp.where(mask, l1m, 0.0)
    hi = l1m.astype(BF16)
    lo = (l1m - hi.astype(F32)).astype(BF16)
    r2 = jnp.dot(hi, sfx, preferred_element_type=F32) + jnp.dot(lo, sfx, preferred_element_type=F32)
    rev = r2[:, :bk] + carry
    w = jnp.exp(z - sp + rev)
    if mask is not None:
        w = jnp.where(mask, w, 0.0)
    pv = jnp.dot(w.astype(BF16), v, preferred_element_type=F32)
    return carry + r2[:, bk:], pv


def _sb_prompt_kernel(q_ref, k_ref, v_ref, z_ref, o_ref, c_scr, acc_scr, *, n_qb):
    blk = SB_BLOCK
    qt = pl.program_id(1)
    sfx = _suffix_matrix(blk)
    row = lax.broadcasted_iota(jnp.int32, (blk, blk), 0)
    col = lax.broadcasted_iota(jnp.int32, (blk, blk), 1)
    causal = col < row

    def kv(kb):
        start = pl.multiple_of(kb * blk, blk)
        return k_ref[pl.ds(start, blk), :], v_ref[pl.ds(start, blk), :]

    def qblock(ib, _):
        r0 = pl.multiple_of(ib * blk, blk)
        q = q_ref[pl.ds(r0, blk), :]
        gq = qt * n_qb + ib
        k, v = kv(gq)
        c, pv = _sb_tile(q, k, v, jnp.zeros((blk, blk), F32), sfx, causal)
        c_scr[...] = c
        acc_scr[...] = pv

        def cond(st):
            kb, go = st
            return jnp.logical_and(kb >= 0, go > 0)

        def body(st):
            kb, _ = st
            k, v = kv(kb)
            c, pv = _sb_tile(q, k, v, c_scr[...], sfx, None)
            c_scr[...] = c
            acc_scr[...] += pv
            return kb - 1, (jnp.max(c) >= SB_LOG_CUTOFF).astype(jnp.int32)

        lax.while_loop(cond, body, (gq - 1, (jnp.max(c) >= SB_LOG_CUTOFF).astype(jnp.int32)))
        o_ref[pl.ds(r0, blk), :] = (acc_scr[...] * _silu(z_ref[pl.ds(r0, blk), :])).astype(BF16)
        return 0

    lax.fori_loop(0, n_qb, qblock, 0)


def _sb_prompt_call(q, k, v, z, *, tq=2048):
    t = q.shape[0]
    tq = min(tq, t)
    assert t % tq == 0 and tq % SB_BLOCK == 0
    qspec = pl.BlockSpec((tq, HEAD_DIM), lambda h, i: (i, h))
    kvspec = pl.BlockSpec((t, HEAD_DIM), lambda h, i: (0, h))
    return pl.pallas_call(
        functools.partial(_sb_prompt_kernel, n_qb=tq // SB_BLOCK),
        grid=(N_HEADS, t // tq),
        in_specs=[qspec, kvspec, kvspec, qspec],
        out_specs=qspec,
        out_shape=jax.ShapeDtypeStruct((t, WIDTH), BF16),
        scratch_shapes=[pltpu.VMEM((SB_BLOCK, SB_BLOCK), F32), pltpu.VMEM((SB_BLOCK, HEAD_DIM), F32)],
        compiler_params=_cparams("arbitrary", "arbitrary"),
        name="sb_prompt",
    )(q, k, v, z)


def _sb_sample_kernel(q_ref, kn_ref, vn_ref, kc_ref, vc_ref, z_ref, o_ref, c_scr, acc_scr, *, past):
    blk = SB_BLOCK
    tq = q_ref.shape[0]
    half = blk - tq
    sfx = _suffix_matrix(blk)
    row = lax.broadcasted_iota(jnp.int32, (tq, blk), 0)
    col = lax.broadcasted_iota(jnp.int32, (tq, blk), 1)
    q = q_ref[...]

    k0 = jnp.concatenate([kc_ref[0, past - half:past, :].astype(BF16), kn_ref[...]], axis=0)
    v0 = jnp.concatenate([vc_ref[0, past - half:past, :].astype(BF16), vn_ref[...]], axis=0)
    c, pv = _sb_tile(q, k0, v0, jnp.zeros((tq, blk), F32), sfx, col < row + half)
    c_scr[...] = c
    acc_scr[...] = pv

    n_full = (past - half) // blk
    rem = (past - half) % blk

    def cond(st):
        j, go = st
        return jnp.logical_and(j < n_full, go > 0)

    def body(st):
        j, _ = st
        start = past - half - (j + 1) * blk
        if rem % 8 == 0:
            start = pl.multiple_of(start, 8)
        k = kc_ref[0, pl.ds(start, blk), :].astype(BF16)
        v = vc_ref[0, pl.ds(start, blk), :].astype(BF16)
        c, pv = _sb_tile(q, k, v, c_scr[...], sfx, None)
        c_scr[...] = c
        acc_scr[...] += pv
        return j + 1, (jnp.max(c) >= SB_LOG_CUTOFF).astype(jnp.int32)

    _, go = lax.while_loop(cond, body, (0, (jnp.max(c) >= SB_LOG_CUTOFF).astype(jnp.int32)))

    if rem:
        @pl.when(go > 0)
        def _():
            k = kc_ref[0, 0:blk, :].astype(BF16)
            v = vc_ref[0, 0:blk, :].astype(BF16)
            _, pv = _sb_tile(q, k, v, c_scr[...], sfx, col < rem)
            acc_scr[...] += pv

    o_ref[...] = (acc_scr[...] * _silu(z_ref[...])).astype(BF16)


def _sb_sample_call(q, kn, vn, kc, vc, z):
    nb, past, _ = kc.shape
    tq = q.shape[0] // nb
    assert tq % 16 == 0 and tq <= SB_BLOCK and past >= SB_BLOCK
    new = pl.BlockSpec((tq, HEAD_DIM), lambda b, h: (b, h))
    cache = pl.BlockSpec((1, past, HEAD_DIM), lambda b, h: (b, 0, h))
    return pl.pallas_call(
        functools.partial(_sb_sample_kernel, past=past),
        grid=(nb, N_HEADS),
        in_specs=[new, new, new, cache, cache, new],
        out_specs=new,
        out_shape=jax.ShapeDtypeStruct(q.shape, BF16),
        scratch_shapes=[pltpu.VMEM((tq, SB_BLOCK), F32), pltpu.VMEM((tq, HEAD_DIM), F32)],
        compiler_params=_cparams("arbitrary", "arbitrary"),
        name="sb_sample",
    )(q, kn, vn, kc, vc, z)


def _prefix_matrix(c):
    t = lax.broadcasted_iota(jnp.int32, (2 * c, c), 0)
    s = lax.broadcasted_iota(jnp.int32, (2 * c, c), 1)
    incl = (t < c) & (s <= t)
    sub = (t >= c) & (s < ((t - c) // HG_SUB) * HG_SUB)
    return jnp.where(incl | sub, 1.0, 0.0).astype(BF16)


def _hgrn_chunk(lf, qh, kh, v, st, pfx, tril):
    c = lf.shape[0]
    p0 = lf.astype(BF16)
    r1 = lf - p0.astype(F32)
    p1 = r1.astype(BF16)
    p2 = (r1 - p1.astype(F32)).astype(BF16)
    br = (jnp.dot(pfx, p0, preferred_element_type=F32) + jnp.dot(pfx, p1, preferred_element_type=F32)
          + jnp.dot(pfx, p2, preferred_element_type=F32))
    b = br[:c]
    r = br[c:]
    b_last = b[c - 1:c, :]
    vb = v.astype(BF16)
    q_sub = (qh * jnp.exp(b - r)).astype(BF16)
    q_dec = (qh * jnp.exp(b)).astype(BF16)
    k_end = (kh * jnp.exp(b_last - b)).astype(BF16)
    rows = lax.broadcasted_iota(jnp.int32, (c, HEAD_DIM), 0)
    att = []
    for i in range(c // HG_SUB):
        lo, hi = i * HG_SUB, (i + 1) * HG_SUB
        expo = jnp.where(rows < hi, r[lo:lo + 1, :] - b, 0.0)
        k_i = (kh * jnp.exp(expo)).astype(BF16)
        att.append(lax.dot_general(q_sub[lo:hi], k_i, (((1,), (1,)), ((), ())), preferred_element_type=F32))
    att = jnp.where(tril, jnp.concatenate(att, axis=0), 0.0).astype(BF16)
    o = jnp.dot(att, vb, preferred_element_type=F32)
    o = o + lax.dot_general(q_dec, st.astype(BF16), (((1,), (1,)), ((), ())), preferred_element_type=F32)
    st = st * jnp.exp(b_last) + lax.dot_general(vb, k_end, (((0,), (0,)), ((), ())), preferred_element_type=F32)
    return o, st


def _hgrn_kernel(lf_ref, qh_ref, kh_ref, v_ref, zh_ref, g_ref, s0_ref, o_ref, s_ref, st_scr, *, chunk, n_chunks):
    tt = pl.program_id(2)

    @pl.when(tt == 0)
    def _():
        st_scr[...] = s0_ref[0, 0].T

    pfx = _prefix_matrix(chunk)
    ti = lax.broadcasted_iota(jnp.int32, (chunk, chunk), 0)
    si = lax.broadcasted_iota(jnp.int32, (chunk, chunk), 1)
    tril = si <= ti
    gain = g_ref[0]

    def step(ci, _):
        r0 = pl.multiple_of(ci * chunk, chunk)
        rs = pl.ds(r0, chunk)
        o, st = _hgrn_chunk(lf_ref[rs, :], qh_ref[rs, :], kh_ref[rs, :], v_ref[rs, :], st_scr[...], pfx, tril)
        st_scr[...] = st
        ms = jnp.mean(o * o, axis=-1, keepdims=True)
        o_ref[rs, :] = (o * lax.rsqrt(ms + EPS) * gain * _silu(zh_ref[rs, :])).astype(BF16)
        return 0

    lax.fori_loop(0, n_chunks, step, 0)

    @pl.when(tt == pl.num_programs(2) - 1)
    def _():
        s_ref[0, 0] = st_scr[...].T


def _hgrn_call(lf, qh, kh, v, zh, gain, s0, *, chunk, tile):
    nb = s0.shape[0]
    t = lf.shape[0] // nb
    tile = min(tile, t)
    assert t % tile == 0 and tile % chunk == 0 and chunk % HG_SUB == 0
    nt = t // tile
    tok = pl.BlockSpec((tile, HEAD_DIM), lambda b, h, i: (b * nt + i, h))
    state = pl.BlockSpec((1, 1, HEAD_DIM, HEAD_DIM), lambda b, h, i: (b, h, 0, 0))
    return pl.pallas_call(
        functools.partial(_hgrn_kernel, chunk=chunk, n_chunks=tile // chunk),
        grid=(nb, N_HEADS, nt),
        in_specs=[tok, tok, tok, tok, tok, pl.BlockSpec((1, 1, HEAD_DIM), lambda b, h, i: (h, 0, 0)), state],
        out_specs=[tok, state],
        out_shape=[jax.ShapeDtypeStruct(lf.shape, BF16), jax.ShapeDtypeStruct(s0.shape, F32)],
        scratch_shapes=[pltpu.VMEM((HEAD_DIM, HEAD_DIM), F32)],
        compiler_params=_cparams("arbitrary", "arbitrary", "arbitrary"),
        name="hgrn2",
    )(lf, qh, kh, v, zh, gain, s0)


def _out_kernel(gs_ref, gh_ref, gsb_ref, ghg_ref, x_ref, gate_ref, wsb_ref, whg_ref, wo_ref, y_ref):
    nb, tr, d = x_ref.shape
    y_sb = jnp.dot(gs_ref[...], wsb_ref[...], preferred_element_type=F32)
    y_h = jnp.dot(gh_ref[...], whg_ref[...], preferred_element_type=F32)
    merged = jax.nn.sigmoid(gsb_ref[...]) * y_sb + jax.nn.sigmoid(ghg_ref[...]) * y_h
    upd = jnp.dot(merged.astype(BF16), wo_ref[...], preferred_element_type=F32)
    y_ref[...] = x_ref[...] + gate_ref[...] * upd.reshape(nb, tr, d)


def _out_call(gs, gh, gg, x, gate, wsb, whg, wo, nb, tr):
    n, t, d = x.shape
    tm = nb * tr
    nt = t // tr
    rowblk = lambda w, c: pl.BlockSpec((tm, w), lambda i, j: (i * nt + j if nb == 1 else i, c))
    const = lambda a: pl.BlockSpec(a.shape, lambda i, j: (0, 0), pipeline_mode=pl.Buffered(1))
    return pl.pallas_call(
        _out_kernel,
        grid=(n // nb, nt),
        in_specs=[rowblk(WIDTH, 0), rowblk(WIDTH, 0), rowblk(d, 0), rowblk(d, 1),
                  pl.BlockSpec((nb, tr, d), lambda i, j: (i, j, 0)),
                  pl.BlockSpec((nb, 1, d), lambda i, j: (i, 0, 0)),
                  const(wsb), const(whg), const(wo)],
        out_specs=pl.BlockSpec((nb, tr, d), lambda i, j: (i, j, 0)),
        out_shape=jax.ShapeDtypeStruct(x.shape, F32),
        compiler_params=_cparams("arbitrary", "arbitrary"),
        name="merge_out",
    )(gs, gh, gg, gg, x, gate, wsb, whg, wo)


def _stream(x, shift, scale, gate, p, *, nb, tr, hg_chunk, hg_tile, s0, cache=None):
    n, t, d = x.shape
    rows = n * t
    h = _prenorm_call(x, p["norm_gain"], scale, shift, nb, tr).reshape(rows, d)
    w_in = p["w_in"]
    tm = min(512, rows)
    proj = functools.partial(_proj_call, h, w_in, tm=tm)
    (q_sb,) = proj(0 * WIDTH, WIDTH, "norm_q", (p["q_gain"],))
    k_sb, k_bf = proj(1 * WIDTH, WIDTH, "norm_k", (p["k_gain"],))
    v_sb, v_bf = proj(2 * WIDTH, WIDTH, "copy2")
    (z_sb,) = proj(3 * WIDTH, WIDTH, "plain")
    logf, k_h = proj(4 * WIDTH, WIDTH, "forget", (p["lb_raw"],))
    (i_h,) = proj(5 * WIDTH, WIDTH, "plain")
    (q_h,) = proj(6 * WIDTH, WIDTH, "silu")
    (z_h,) = proj(7 * WIDTH, WIDTH, "plain")
    (gg,) = proj(8 * WIDTH, 2 * d, "plain")

    if cache is None:
        assert n == 1
        gs = _sb_prompt_call(q_sb, k_bf, v_bf, z_sb)
    else:
        gs = _sb_sample_call(q_sb, k_bf, v_bf, cache[0], cache[1], z_sb)
    gh, s_new = _hgrn_call(logf, q_h, k_h, i_h, z_h, p["onorm_gain"], s0, chunk=hg_chunk, tile=hg_tile)
    y = _out_call(gs, gh, gg, x, gate, p["w_br_sb"], p["w_br_hg"], p["w_out"], nb, tr)
    k_new = k_sb.reshape(1, n, t, N_HEADS, HEAD_DIM)
    v_new = v_sb.reshape(1, n, t, N_HEADS, HEAD_DIM)
    return y, k_new, v_new, s_new[None]


def kernel(x_prompt, x_sample, cache_sb_k, cache_sb_v, state_hgrn, c_prompt, c_sample, norm_gain, w_ada, b_ada, w_in, q_norm_gain, k_norm_gain, hgrn_lb_raw, hgrn_onorm_gain, w_branch_sb, w_branch_hgrn, w_out):
    assert w_in.shape[0] == 1, "single-layer trunk"
    n_p, t_p, d = x_prompt.shape
    n_s, t_s, _ = x_sample.shape
    past = cache_sb_k.shape[2]

    c_all = jnp.concatenate([c_prompt, c_sample], axis=0)
    pad = (-c_all.shape[0]) % 8
    c_all = jnp.pad(c_all, ((0, pad), (0, 0)))
    mod = _ada_call(c_all, w_ada[0], b_ada[0].reshape(1, 3 * d))
    shift, scale, gate = (mod[:, i * d:(i + 1) * d] for i in range(3))
    vec = lambda a, lo, hi: a[lo:hi].reshape(hi - lo, 1, d)

    p = {
        "norm_gain": norm_gain[0].reshape(1, 1, d),
        "w_in": w_in[0],
        "q_gain": q_norm_gain[0].reshape(1, HEAD_DIM),
        "k_gain": k_norm_gain[0].reshape(1, HEAD_DIM),
        "lb_raw": hgrn_lb_raw,
        "onorm_gain": hgrn_onorm_gain[0].reshape(N_HEADS, 1, HEAD_DIM),
        "w_br_sb": w_branch_sb[0].astype(BF16),
        "w_br_hg": w_branch_hgrn[0].astype(BF16),
        "w_out": w_out[0].astype(BF16),
    }

    y_p, k_p, v_p, s_p = _stream(
        x_prompt, vec(shift, 0, n_p), vec(scale, 0, n_p), vec(gate, 0, n_p), p,
        nb=1, tr=min(256, t_p), hg_chunk=min(128, t_p), hg_tile=2048,
        s0=jnp.zeros((n_p, N_HEADS, HEAD_DIM, HEAD_DIM), F32))
    nb_s = max(1, min(n_s, 256 // t_s))
    y_s, k_s, v_s, s_s = _stream(
        x_sample, vec(shift, n_p, n_p + n_s), vec(scale, n_p, n_p + n_s), vec(gate, n_p, n_p + n_s), p,
        nb=nb_s, tr=t_s, hg_chunk=t_s, hg_tile=t_s, s0=state_hgrn[0],
        cache=(cache_sb_k[0].reshape(n_s, past, WIDTH), cache_sb_v[0].reshape(n_s, past, WIDTH)))
    return (y_p, y_s, k_p, v_p, s_p, k_s, v_s, s_s)
```

```python
import functools

import jax
import jax.numpy as jnp
from jax import lax
from jax.experimental import pallas as pl
from jax.experimental.pallas import tpu as pltpu

F32 = jnp.float32
BF16 = jnp.bfloat16

N_HEADS = 8
HEAD_DIM = 128
WIDTH = N_HEADS * HEAD_DIM
HG_SUB = 16
EPS = 1e-6
SB_BLOCK = 128
SB_LOG_CUTOFF = -88.0
VMEM_LIMIT = 56 * 1024 * 1024


def _cparams(*sem):
    return pltpu.CompilerParams(dimension_semantics=sem, vmem_limit_bytes=VMEM_LIMIT)


def _silu(x):
    return x * jax.nn.sigmoid(x)


def _ada_kernel(c_ref, w_ref, b_ref, o_ref):
    c = c_ref[...]
    a = _silu(c).astype(BF16)
    o_ref[...] = jnp.dot(a, w_ref[...].astype(BF16), preferred_element_type=F32) + b_ref[...]


def _ada_call(c, w, b):
    r, d = c.shape
    n = w.shape[1]
    tn = 1024
    return pl.pallas_call(
        _ada_kernel,
        grid=(n // tn,),
        in_specs=[pl.BlockSpec((r, d), lambda j: (0, 0)),
                  pl.BlockSpec((d, tn), lambda j: (0, j)),
                  pl.BlockSpec((1, tn), lambda j: (0, j))],
        out_specs=pl.BlockSpec((r, tn), lambda j: (0, j)),
        out_shape=jax.ShapeDtypeStruct((r, n), F32),
        compiler_params=_cparams("arbitrary"),
        name="ada_mod",
    )(c, w, b)


def _prenorm_kernel(x_ref, g_ref, sc_ref, sh_ref, h_ref):
    x = x_ref[...]
    ms = jnp.mean(x * x, axis=-1, keepdims=True)
    xn = x * lax.rsqrt(ms + EPS)
    h = xn * g_ref[...] * (1.0 + sc_ref[...]) + sh_ref[...]
    h_ref[...] = h.astype(BF16)


def _prenorm_call(x, gain, scale, shift, nb, tr):
    n, t, d = x.shape
    vec = pl.BlockSpec((nb, 1, d), lambda i, j: (i, 0, 0))
    return pl.pallas_call(
        _prenorm_kernel,
        grid=(n // nb, t // tr),
        in_specs=[pl.BlockSpec((nb, tr, d), lambda i, j: (i, j, 0)),
                  pl.BlockSpec((1, 1, d), lambda i, j: (0, 0, 0)), vec, vec],
        out_specs=pl.BlockSpec((nb, tr, d), lambda i, j: (i, j, 0)),
        out_shape=jax.ShapeDtypeStruct((n, t, d), BF16),
        compiler_params=_cparams("arbitrary", "arbitrary"),
        name="prenorm",
    )(x, gain, scale, shift)


def _head_rms(y, gain):
    outs = []
    for g in range(N_HEADS):
        yh = y[:, g * HEAD_DIM:(g + 1) * HEAD_DIM]
        ms = jnp.mean(yh * yh, axis=-1, keepdims=True)
        outs.append(yh * lax.rsqrt(ms + EPS) * gain)
    return outs


def _proj_kernel(*refs, kind):
    h_ref, w_ref = refs[0], refs[1]
    wb_ref = refs[-1]

    @pl.when(pl.program_id(1) == 0)
    def _():
        wb_ref[...] = w_ref[...].astype(BF16)

    y = jnp.dot(h_ref[...], wb_ref[...], preferred_element_type=F32)
    if kind == "plain":
        refs[2][...] = y
    elif kind == "silu":
        refs[2][...] = _silu(y)
    elif kind == "copy2":
        refs[2][...] = y
        refs[3][...] = y.astype(BF16)
    elif kind == "norm_q":
        gain = refs[2][...]
        for g, o in enumerate(_head_rms(y, gain)):
            refs[3][:, g * HEAD_DIM:(g + 1) * HEAD_DIM] = o.astype(BF16)
    elif kind == "norm_k":
        gain = refs[2][...]
        for g, o in enumerate(_head_rms(y, gain)):
            refs[3][:, g * HEAD_DIM:(g + 1) * HEAD_DIM] = o
            refs[4][:, g * HEAD_DIM:(g + 1) * HEAD_DIM] = o.astype(BF16)
    elif kind == "forget":
        raw = refs[2][...]
        e = jnp.exp(raw - jnp.max(raw, axis=0, keepdims=True))
        lb = e[0:1, :] / jnp.sum(e, axis=0, keepdims=True)
        f = lb + (1.0 - lb) * jax.nn.sigmoid(y)
        refs[3][...] = jnp.log(f)
        refs[4][...] = 1.0 - f
    else:
        raise ValueError(kind)


def _proj_call(h, w_in, col0, ncols, kind, extra=(), *, tm=512, tn=1024):
    rows, d = h.shape
    assert col0 % tn == 0 and ncols % tn == 0 and rows % tm == 0
    jb = col0 // tn
    grid = (ncols // tn, rows // tm)
    tile = lambda: pl.BlockSpec((tm, tn), lambda j, i: (i, j))
    in_specs = [pl.BlockSpec((tm, d), lambda j, i: (i, 0)),
                pl.BlockSpec((d, tn), lambda j, i: (0, jb + j))]
    for e in extra:
        in_specs.append(pl.BlockSpec(e.shape, lambda j, i: (0, 0)))
    if kind in ("plain", "silu"):
        out_dt = (F32,)
    elif kind == "norm_q":
        out_dt = (BF16,)
    elif kind in ("copy2", "norm_k"):
        out_dt = (F32, BF16)
    else:
        out_dt = (F32, F32)
    outs = pl.pallas_call(
        functools.partial(_proj_kernel, kind=kind),
        grid=grid,
        in_specs=in_specs,
        out_specs=[tile() for _ in out_dt],
        out_shape=[jax.ShapeDtypeStruct((rows, ncols), dt) for dt in out_dt],
        scratch_shapes=[pltpu.VMEM((d, tn), BF16)],
        compiler_params=_cparams("arbitrary", "arbitrary"),
        name="proj_" + kind,
    )(h, w_in, *extra)
    return outs


def _suffix_matrix(bk):
    j = lax.broadcasted_iota(jnp.int32, (bk, 2 * bk), 0)
    s = lax.broadcasted_iota(jnp.int32, (bk, 2 * bk), 1)
    return jnp.where((j > s) | (s >= bk), -1.0, 0.0).astype(BF16)


def _sb_tiles(qs, ks, vs, carries, sfx, mask):
    bk = ks[0].shape[0]
    n = range(len(qs))
    zs = [lax.dot_general(qs[i], ks[i], (((1,), (1,)), ((), ())), preferred_element_type=F32) * HEAD_DIM ** -0.5
          for i in n]
    sps = [jnp.maximum(z, 0.0) + jnp.log(1.0 + jnp.exp(-jnp.abs(z))) for z in zs]
    l1m = sps if mask is None else [jnp.where(mask, sp, 0.0) for sp in sps]
    his = [x.astype(BF16) for x in l1m]
    los = [(x - hi.astype(F32)).astype(BF16) for x, hi in zip(l1m, his)]
    r2s = [jnp.dot(hi, sfx, preferred_element_type=F32) + jnp.dot(lo, sfx, preferred_element_type=F32)
           for hi, lo in zip(his, los)]
    ws = [jnp.exp(zs[i] - sps[i] + r2s[i][:, :bk] + carries[i]) for i in n]
    if mask is not None:
        ws = [jnp.where(mask, w, 0.0) for w in ws]
    pvs = [jnp.dot(ws[i].astype(BF16), vs[i], preferred_element_type=F32) for i in n]
    return [carries[i] + r2s[i][:, bk:] for i in n], pvs


def _sb_prompt_kernel(q_ref, k_ref, v_ref, z_ref, o_ref, c_scr, acc_scr, *, n_groups, group):
    blk = SB_BLOCK
    qt = pl.program_id(1)
    sfx = _suffix_matrix(blk)
    row = lax.broadcasted_iota(jnp.int32, (blk, blk), 0)
    col = lax.broadcasted_iota(jnp.int32, (blk, blk), 1)
    causal = col < row
    alive = lambda cs: (functools.reduce(jnp.maximum, [jnp.max(c) for c in cs]) >= SB_LOG_CUTOFF).astype(jnp.int32)

    def kv(kb):
        start = pl.multiple_of(kb * blk, blk)
        return k_ref[pl.ds(start, blk), :], v_ref[pl.ds(start, blk), :]

    def qgroup(ig, _):
        gq0 = (qt * n_groups + ig) * group
        rows = [pl.ds(pl.multiple_of((ig * group + g) * blk, blk), blk) for g in range(group)]
        qs = [q_ref[r, :] for r in rows]
        kvs = [kv(gq0 + g) for g in range(group)]
        zero = jnp.zeros((blk, blk), F32)
        cs, pvs = _sb_tiles(qs, [k for k, _ in kvs], [v for _, v in kvs], [zero] * group, sfx, causal)
        for g in range(group):
            c_scr[g] = cs[g]
            acc_scr[g] = pvs[g]

        def cond(st):
            s, go = st
            return jnp.logical_and(s <= gq0 + group - 1, go > 0)

        def body(st):
            s, _ = st
            kbs = [gq0 + g - s for g in range(group)]
            kvs = [kv(jnp.maximum(kb, 0)) for kb in kbs]
            cs, pvs = _sb_tiles(qs, [k for k, _ in kvs], [v for _, v in kvs],
                                [c_scr[g] for g in range(group)], sfx, None)
            for g in range(group):
                c_scr[g] = cs[g]
                acc_scr[g] += jnp.where(kbs[g] >= 0, pvs[g], 0.0)
            return s + 1, alive(cs)

        lax.while_loop(cond, body, (1, alive(cs)))
        for g in range(group):
            o_ref[rows[g], :] = (acc_scr[g] * _silu(z_ref[rows[g], :])).astype(BF16)
        return 0

    lax.fori_loop(0, n_groups, qgroup, 0)


def _sb_prompt_call(q, k, v, z, *, tq=2048, group=4):
    t = q.shape[0]
    tq = min(tq, t)
    assert t % tq == 0 and tq % (SB_BLOCK * group) == 0
    qspec = pl.BlockSpec((tq, HEAD_DIM), lambda h, i: (i, h))
    kvspec = pl.BlockSpec((t, HEAD_DIM), lambda h, i: (0, h))
    return pl.pallas_call(
        functools.partial(_sb_prompt_kernel, n_groups=tq // (SB_BLOCK * group), group=group),
        grid=(N_HEADS, t // tq),
        in_specs=[qspec, kvspec, kvspec, qspec],
        out_specs=qspec,
        out_shape=jax.ShapeDtypeStruct((t, WIDTH), BF16),
        scratch_shapes=[pltpu.VMEM((group, SB_BLOCK, SB_BLOCK), F32), pltpu.VMEM((group, SB_BLOCK, HEAD_DIM), F32)],
        compiler_params=_cparams("arbitrary", "arbitrary"),
        name="sb_prompt",
    )(q, k, v, z)


def _sb_sample_kernel(q_ref, kn_ref, vn_ref, z_ref, kc_hbm, vc_hbm, o_ref, kbuf, vbuf, sem, c_scr, acc_scr, *, past):
    blk = SB_BLOCK
    nh = N_HEADS
    b = pl.program_id(0)
    tq = q_ref.shape[0]
    half = blk - tq
    n_full = (past - half) // blk
    rem = (past - half) % blk
    sfx = _suffix_matrix(blk)
    row = lax.broadcasted_iota(jnp.int32, (tq, blk), 0)
    col = lax.broadcasted_iota(jnp.int32, (tq, blk), 1)
    heads = [slice(h * HEAD_DIM, (h + 1) * HEAD_DIM) for h in range(nh)]
    alive = lambda cs: (functools.reduce(jnp.maximum, [jnp.max(c) for c in cs]) >= SB_LOG_CUTOFF).astype(jnp.int32)

    def copies(key0, nkeys, slot):
        src = pl.ds(key0 * nh, nkeys * nh)
        dst = pl.ds(0, nkeys * nh)
        return (pltpu.make_async_copy(kc_hbm.at[b, src, :], kbuf.at[slot, dst, :], sem.at[0, slot]),
                pltpu.make_async_copy(vc_hbm.at[b, src, :], vbuf.at[slot, dst, :], sem.at[1, slot]))

    def start(cps):
        for cp in cps:
            cp.start()

    def wait(cps):
        for cp in cps:
            cp.wait()

    def tile_copies(j):
        return copies(past - half - (j + 1) * blk, blk, (j + 1) % 2)

    def cached(buf, slot, h, nkeys):
        return buf[slot, pl.ds(h, nkeys, stride=nh), :].astype(BF16)

    start(copies(past - half, half, 0))
    if n_full:
        start(tile_copies(0))
    wait(copies(past - half, half, 0))

    qs = [q_ref[:, heads[h]] for h in range(nh)]
    k0 = [jnp.concatenate([cached(kbuf, 0, h, half), kn_ref[:, heads[h]]], axis=0) for h in range(nh)]
    v0 = [jnp.concatenate([cached(vbuf, 0, h, half), vn_ref[:, heads[h]]], axis=0) for h in range(nh)]
    cs, pvs = _sb_tiles(qs, k0, v0, [jnp.zeros((tq, blk), F32)] * nh, sfx, col < row + half)
    for h in range(nh):
        c_scr[h] = cs[h]
        acc_scr[h] = pvs[h]

    def sweep(slot, mask):
        cs, pvs = _sb_tiles(qs, [cached(kbuf, slot, h, blk) for h in range(nh)],
                            [cached(vbuf, slot, h, blk) for h in range(nh)],
                            [c_scr[h] for h in range(nh)], sfx, mask)
        for h in range(nh):
            c_scr[h] = cs[h]
            acc_scr[h] += pvs[h]
        return cs

    def cond(st):
        j, go = st
        return jnp.logical_and(j < n_full, go > 0)

    def body(st):
        j, _ = st
        slot = (j + 1) % 2
        wait(tile_copies(j))

        @pl.when(j + 1 < n_full)
        def _():
            start(tile_copies(j + 1))

        return j + 1, alive(sweep(slot, None))

    done, go = lax.while_loop(cond, body, (0, alive(cs)))

    @pl.when(done < n_full)
    def _():
        wait(tile_copies(done))

    if rem:
        @pl.when(go > 0)
        def _():
            cps = copies(0, blk, 0)
            start(cps)
            wait(cps)
            sweep(0, col < rem)

    for h in range(nh):
        o_ref[:, heads[h]] = (acc_scr[h] * _silu(z_ref[:, heads[h]])).astype(BF16)


def _sb_sample_call(q, kn, vn, z, kc, vc):
    nb = kc.shape[0]
    past = kc.shape[1] // N_HEADS
    tq = q.shape[0] // nb
    assert tq % 16 == 0 and tq < SB_BLOCK and past >= SB_BLOCK
    new = pl.BlockSpec((tq, WIDTH), lambda b: (b, 0))
    hbm = pl.BlockSpec(memory_space=pl.ANY)
    return pl.pallas_call(
        functools.partial(_sb_sample_kernel, past=past),
        grid=(nb,),
        in_specs=[new, new, new, new, hbm, hbm],
        out_specs=new,
        out_shape=jax.ShapeDtypeStruct(q.shape, BF16),
        scratch_shapes=[pltpu.VMEM((2, SB_BLOCK * N_HEADS, HEAD_DIM), F32),
                        pltpu.VMEM((2, SB_BLOCK * N_HEADS, HEAD_DIM), F32),
                        pltpu.SemaphoreType.DMA((2, 2)),
                        pltpu.VMEM((N_HEADS, tq, SB_BLOCK), F32), pltpu.VMEM((N_HEADS, tq, HEAD_DIM), F32)],
        compiler_params=_cparams("arbitrary"),
        name="sb_sample",
    )(q, kn, vn, z, kc, vc)


def _prefix_matrix(c):
    t = lax.broadcasted_iota(jnp.int32, (2 * c, c), 0)
    s = lax.broadcasted_iota(jnp.int32, (2 * c, c), 1)
    incl = (t < c) & (s <= t)
    sub = (t >= c) & (s < ((t - c) // HG_SUB) * HG_SUB)
    return jnp.where(incl | sub, 1.0, 0.0).astype(BF16)


def _hgrn_chunk(lf, qh, kh, v, st, pfx, tril):
    c = lf.shape[0]
    p0 = lf.astype(BF16)
    r1 = lf - p0.astype(F32)
    p1 = r1.astype(BF16)
    p2 = (r1 - p1.astype(F32)).astype(BF16)
    br = (jnp.dot(pfx, p0, preferred_element_type=F32) + jnp.dot(pfx, p1, preferred_element_type=F32)
          + jnp.dot(pfx, p2, preferred_element_type=F32))
    b = br[:c]
    r = br[c:]
    b_last = b[c - 1:c, :]
    vb = v.astype(BF16)
    q_sub = (qh * jnp.exp(b - r)).astype(BF16)
    q_dec = (qh * jnp.exp(b)).astype(BF16)
    k_end = (kh * jnp.exp(b_last - b)).astype(BF16)
    rows = lax.broadcasted_iota(jnp.int32, (c, HEAD_DIM), 0)
    att = []
    for i in range(c // HG_SUB):
        lo, hi = i * HG_SUB, (i + 1) * HG_SUB
        expo = jnp.where(rows < hi, r[lo:lo + 1, :] - b, 0.0)
        k_i = (kh * jnp.exp(expo)).astype(BF16)
        att.append(lax.dot_general(q_sub[lo:hi], k_i, (((1,), (1,)), ((), ())), preferred_element_type=F32))
    att = jnp.where(tril, jnp.concatenate(att, axis=0), 0.0).astype(BF16)
    o = jnp.dot(att, vb, preferred_element_type=F32)
    o = o + lax.dot_general(q_dec, st.astype(BF16), (((1,), (1,)), ((), ())), preferred_element_type=F32)
    st = st * jnp.exp(b_last) + lax.dot_general(vb, k_end, (((0,), (0,)), ((), ())), preferred_element_type=F32)
    return o, st


def _hgrn_kernel(lf_ref, qh_ref, kh_ref, v_ref, zh_ref, g_ref, s0_ref, o_ref, s_ref, st_scr, *, chunk, n_chunks):
    tt = pl.program_id(2)

    @pl.when(tt == 0)
    def _():
        st_scr[...] = s0_ref[0, 0].T

    pfx = _prefix_matrix(chunk)
    ti = lax.broadcasted_iota(jnp.int32, (chunk, chunk), 0)
    si = lax.broadcasted_iota(jnp.int32, (chunk, chunk), 1)
    tril = si <= ti
    gain = g_ref[0]

    def step(ci, _):
        r0 = pl.multiple_of(ci * chunk, chunk)
        rs = pl.ds(r0, chunk)
        o, st = _hgrn_chunk(lf_ref[rs, :], qh_ref[rs, :], kh_ref[rs, :], v_ref[rs, :], st_scr[...], pfx, tril)
        st_scr[...] = st
        ms = jnp.mean(o * o, axis=-1, keepdims=True)
        o_ref[rs, :] = (o * lax.rsqrt(ms + EPS) * gain * _silu(zh_ref[rs, :])).astype(BF16)
        return 0

    lax.fori_loop(0, n_chunks, step, 0)

    @pl.when(tt == pl.num_programs(2) - 1)
    def _():
        s_ref[0, 0] = st_scr[...].T


def _hgrn_call(lf, qh, kh, v, zh, gain, s0, *, chunk, tile):
    nb = s0.shape[0]
    t = lf.shape[0] // nb
    tile = min(tile, t)
    assert t % tile == 0 and tile % chunk == 0 and chunk % HG_SUB == 0
    nt = t // tile
    tok = pl.BlockSpec((tile, HEAD_DIM), lambda b, h, i: (b * nt + i, h))
    state = pl.BlockSpec((1, 1, HEAD_DIM, HEAD_DIM), lambda b, h, i: (b, h, 0, 0))
    return pl.pallas_call(
        functools.partial(_hgrn_kernel, chunk=chunk, n_chunks=tile // chunk),
        grid=(nb, N_HEADS, nt),
        in_specs=[tok, tok, tok, tok, tok, pl.BlockSpec((1, 1, HEAD_DIM), lambda b, h, i: (h, 0, 0)), state],
        out_specs=[tok, state],
        out_shape=[jax.ShapeDtypeStruct(lf.shape, BF16), jax.ShapeDtypeStruct(s0.shape, F32)],
        scratch_shapes=[pltpu.VMEM((HEAD_DIM, HEAD_DIM), F32)],
        compiler_params=_cparams("arbitrary", "arbitrary", "arbitrary"),
        name="hgrn2",
    )(lf, qh, kh, v, zh, gain, s0)


def _out_kernel(gs_ref, gh_ref, gsb_ref, ghg_ref, x_ref, gate_ref, wsb_ref, whg_ref, wo_ref, y_ref):
    nb, tr, d = x_ref.shape
    y_sb = jnp.dot(gs_ref[...], wsb_ref[...], preferred_element_type=F32)
    y_h = jnp.dot(gh_ref[...], whg_ref[...], preferred_element_type=F32)
    merged = jax.nn.sigmoid(gsb_ref[...]) * y_sb + jax.nn.sigmoid(ghg_ref[...]) * y_h
    upd = jnp.dot(merged.astype(BF16), wo_ref[...], preferred_element_type=F32)
    y_ref[...] = x_ref[...] + gate_ref[...] * upd.reshape(nb, tr, d)


def _out_call(gs, gh, gg, x, gate, wsb, whg, wo, nb, tr):
    n, t, d = x.shape
    tm = nb * tr
    nt = t // tr
    rowblk = lambda w, c: pl.BlockSpec((tm, w), lambda i, j: (i * nt + j if nb == 1 else i, c))
    const = lambda a: pl.BlockSpec(a.shape, lambda i, j: (0, 0), pipeline_mode=pl.Buffered(1))
    return pl.pallas_call(
        _out_kernel,
        grid=(n // nb, nt),
        in_specs=[rowblk(WIDTH, 0), rowblk(WIDTH, 0), rowblk(d, 0), rowblk(d, 1),
                  pl.BlockSpec((nb, tr, d), lambda i, j: (i, j, 0)),
                  pl.BlockSpec((nb, 1, d), lambda i, j: (i, 0, 0)),
                  const(wsb), const(whg), const(wo)],
        out_specs=pl.BlockSpec((nb, tr, d), lambda i, j: (i, j, 0)),
        out_shape=jax.ShapeDtypeStruct(x.shape, F32),
        compiler_params=_cparams("arbitrary", "arbitrary"),
        name="merge_out",
    )(gs, gh, gg, gg, x, gate, wsb, whg, wo)


def _stream(x, shift, scale, gate, p, *, nb, tr, hg_chunk, hg_tile, s0, cache=None):
    n, t, d = x.shape
    rows = n * t
    h = _prenorm_call(x, p["norm_gain"], scale, shift, nb, tr).reshape(rows, d)
    w_in = p["w_in"]
    tm = min(512, rows)
    proj = functools.partial(_proj_call, h, w_in, tm=tm)
    (q_sb,) = proj(0 * WIDTH, WIDTH, "norm_q", (p["q_gain"],))
    k_sb, k_bf = proj(1 * WIDTH, WIDTH, "norm_k", (p["k_gain"],))
    v_sb, v_bf = proj(2 * WIDTH, WIDTH, "copy2")
    (z_sb,) = proj(3 * WIDTH, WIDTH, "plain")
    logf, k_h = proj(4 * WIDTH, WIDTH, "forget", (p["lb_raw"],))
    (i_h,) = proj(5 * WIDTH, WIDTH, "plain")
    (q_h,) = proj(6 * WIDTH, WIDTH, "silu")
    (z_h,) = proj(7 * WIDTH, WIDTH, "plain")
    (gg,) = proj(8 * WIDTH, 2 * d, "plain")

    if cache is None:
        assert n == 1
        gs = _sb_prompt_call(q_sb, k_bf, v_bf, z_sb)
    else:
        gs = _sb_sample_call(q_sb, k_bf, v_bf, z_sb, cache[0], cache[1])
    gh, s_new = _hgrn_call(logf, q_h, k_h, i_h, z_h, p["onorm_gain"], s0, chunk=hg_chunk, tile=hg_tile)
    y = _out_call(gs, gh, gg, x, gate, p["w_br_sb"], p["w_br_hg"], p["w_out"], nb, tr)
    k_new = k_sb.reshape(1, n, t, N_HEADS, HEAD_DIM)
    v_new = v_sb.reshape(1, n, t, N_HEADS, HEAD_DIM)
    return y, k_new, v_new, s_new[None]


def kernel(x_prompt, x_sample, cache_sb_k, cache_sb_v, state_hgrn, c_prompt, c_sample, norm_gain, w_ada, b_ada, w_in, q_norm_gain, k_norm_gain, hgrn_lb_raw, hgrn_onorm_gain, w_branch_sb, w_branch_hgrn, w_out):
    assert w_in.shape[0] == 1, "single-layer trunk"
    n_p, t_p, d = x_prompt.shape
    n_s, t_s, _ = x_sample.shape
    past = cache_sb_k.shape[2]

    c_all = jnp.concatenate([c_prompt, c_sample], axis=0)
    pad = (-c_all.shape[0]) % 8
    c_all = jnp.pad(c_all, ((0, pad), (0, 0)))
    mod = _ada_call(c_all, w_ada[0], b_ada[0].reshape(1, 3 * d))
    shift, scale, gate = (mod[:, i * d:(i + 1) * d] for i in range(3))
    vec = lambda a, lo, hi: a[lo:hi].reshape(hi - lo, 1, d)

    p = {
        "norm_gain": norm_gain[0].reshape(1, 1, d),
        "w_in": w_in[0],
        "q_gain": q_norm_gain[0].reshape(1, HEAD_DIM),
        "k_gain": k_norm_gain[0].reshape(1, HEAD_DIM),
        "lb_raw": hgrn_lb_raw,
        "onorm_gain": hgrn_onorm_gain[0].reshape(N_HEADS, 1, HEAD_DIM),
        "w_br_sb": w_branch_sb[0].astype(BF16),
        "w_br_hg": w_branch_hgrn[0].astype(BF16),
        "w_out": w_out[0].astype(BF16),
    }

    y_p, k_p, v_p, s_p = _stream(
        x_prompt, vec(shift, 0, n_p), vec(scale, 0, n_p), vec(gate, 0, n_p), p,
        nb=1, tr=min(256, t_p), hg_chunk=min(128, t_p), hg_tile=2048,
        s0=jnp.zeros((n_p, N_HEADS, HEAD_DIM, HEAD_DIM), F32))
    nb_s = max(1, min(n_s, 256 // t_s))
    y_s, k_s, v_s, s_s = _stream(
        x_sample, vec(shift, n_p, n_p + n_s), vec(scale, n_p, n_p + n_s), vec(gate, n_p, n_p + n_s), p,
        nb=nb_s, tr=t_s, hg_chunk=t_s, hg_tile=t_s, s0=state_hgrn[0],
        cache=(cache_sb_k.reshape(n_s, past * N_HEADS, HEAD_DIM), cache_sb_v.reshape(n_s, past * N_HEADS, HEAD_DIM)))
    return (y_p, y_s, k_p, v_p, s_p, k_s, v_s, s_s)
```

```python
import functools

import jax
import jax.numpy as jnp
from jax import lax
from jax.experimental import pallas as pl
from jax.experimental.pallas import tpu as pltpu

F32 = jnp.float32
BF16 = jnp.bfloat16

N_HEADS = 8
HEAD_DIM = 128
WIDTH = N_HEADS * HEAD_DIM
HG_SUB = 16
EPS = 1e-6
SB_BLOCK = 128
SB_LOG_CUTOFF = -88.0
VMEM_LIMIT = 56 * 1024 * 1024


def _cparams(*sem):
    return pltpu.CompilerParams(dimension_semantics=sem, vmem_limit_bytes=VMEM_LIMIT)


def _silu(x):
    return x * jax.nn.sigmoid(x)


def _ada_kernel(c_ref, w_ref, b_ref, o_ref):
    c = c_ref[...]
    a = _silu(c).astype(BF16)
    o_ref[...] = jnp.dot(a, w_ref[...].astype(BF16), preferred_element_type=F32) + b_ref[...]


def _ada_call(c, w, b):
    r, d = c.shape
    n = w.shape[1]
    tn = 1024
    return pl.pallas_call(
        _ada_kernel,
        grid=(n // tn,),
        in_specs=[pl.BlockSpec((r, d), lambda j: (0, 0)),
                  pl.BlockSpec((d, tn), lambda j: (0, j)),
                  pl.BlockSpec((1, tn), lambda j: (0, j))],
        out_specs=pl.BlockSpec((r, tn), lambda j: (0, j)),
        out_shape=jax.ShapeDtypeStruct((r, n), F32),
        compiler_params=_cparams("arbitrary"),
        name="ada_mod",
    )(c, w, b)


def _prenorm_kernel(x_ref, g_ref, sc_ref, sh_ref, h_ref):
    x = x_ref[...]
    ms = jnp.mean(x * x, axis=-1, keepdims=True)
    xn = x * lax.rsqrt(ms + EPS)
    h = xn * g_ref[...] * (1.0 + sc_ref[...]) + sh_ref[...]
    h_ref[...] = h.astype(BF16)


def _prenorm_call(x, gain, scale, shift, nb, tr):
    n, t, d = x.shape
    vec = pl.BlockSpec((nb, 1, d), lambda i, j: (i, 0, 0))
    return pl.pallas_call(
        _prenorm_kernel,
        grid=(n // nb, t // tr),
        in_specs=[pl.BlockSpec((nb, tr, d), lambda i, j: (i, j, 0)),
                  pl.BlockSpec((1, 1, d), lambda i, j: (0, 0, 0)), vec, vec],
        out_specs=pl.BlockSpec((nb, tr, d), lambda i, j: (i, j, 0)),
        out_shape=jax.ShapeDtypeStruct((n, t, d), BF16),
        compiler_params=_cparams("arbitrary", "arbitrary"),
        name="prenorm",
    )(x, gain, scale, shift)


def _head_rms(y, gain):
    outs = []
    for g in range(N_HEADS):
        yh = y[:, g * HEAD_DIM:(g + 1) * HEAD_DIM]
        ms = jnp.mean(yh * yh, axis=-1, keepdims=True)
        outs.append(yh * lax.rsqrt(ms + EPS) * gain)
    return outs


def _proj_kernel(*refs, kind):
    h_ref, w_ref = refs[0], refs[1]
    wb_ref = refs[-1]

    @pl.when(pl.program_id(1) == 0)
    def _():
        wb_ref[...] = w_ref[...].astype(BF16)

    y = jnp.dot(h_ref[...], wb_ref[...], preferred_element_type=F32)
    if kind == "plain":
        refs[2][...] = y
    elif kind == "silu":
        refs[2][...] = _silu(y)
    elif kind == "copy2":
        refs[2][...] = y
        refs[3][...] = y.astype(BF16)
    elif kind == "norm_q":
        gain = refs[2][...]
        for g, o in enumerate(_head_rms(y, gain)):
            refs[3][:, g * HEAD_DIM:(g + 1) * HEAD_DIM] = o.astype(BF16)
    elif kind == "norm_k":
        gain = refs[2][...]
        for g, o in enumerate(_head_rms(y, gain)):
            refs[3][:, g * HEAD_DIM:(g + 1) * HEAD_DIM] = o
            refs[4][:, g * HEAD_DIM:(g + 1) * HEAD_DIM] = o.astype(BF16)
    elif kind == "forget":
        raw = refs[2][...]
        e = jnp.exp(raw - jnp.max(raw, axis=0, keepdims=True))
        lb = e[0:1, :] / jnp.sum(e, axis=0, keepdims=True)
        f = lb + (1.0 - lb) * jax.nn.sigmoid(y)
        refs[3][...] = jnp.log(f)
        refs[4][...] = 1.0 - f
    else:
        raise ValueError(kind)


def _proj_call(h, w_in, col0, ncols, kind, extra=(), *, tm=512, tn=1024):
    rows, d = h.shape
    assert col0 % tn == 0 and ncols % tn == 0 and rows % tm == 0
    jb = col0 // tn
    grid = (ncols // tn, rows // tm)
    tile = lambda: pl.BlockSpec((tm, tn), lambda j, i: (i, j))
    in_specs = [pl.BlockSpec((tm, d), lambda j, i: (i, 0)),
                pl.BlockSpec((d, tn), lambda j, i: (0, jb + j))]
    for e in extra:
        in_specs.append(pl.BlockSpec(e.shape, lambda j, i: (0, 0)))
    if kind in ("plain", "silu"):
        out_dt = (F32,)
    elif kind == "norm_q":
        out_dt = (BF16,)
    elif kind in ("copy2", "norm_k"):
        out_dt = (F32, BF16)
    else:
        out_dt = (F32, F32)
    outs = pl.pallas_call(
        functools.partial(_proj_kernel, kind=kind),
        grid=grid,
        in_specs=in_specs,
        out_specs=[tile() for _ in out_dt],
        out_shape=[jax.ShapeDtypeStruct((rows, ncols), dt) for dt in out_dt],
        scratch_shapes=[pltpu.VMEM((d, tn), BF16)],
        compiler_params=_cparams("arbitrary", "arbitrary"),
        name="proj_" + kind,
    )(h, w_in, *extra)
    return outs


def _suffix_matrix(bk):
    j = lax.broadcasted_iota(jnp.int32, (bk, 2 * bk), 0)
    s = lax.broadcasted_iota(jnp.int32, (bk, 2 * bk), 1)
    return jnp.where((j > s) | (s >= bk), -1.0, 0.0).astype(BF16)


def _sb_tiles(qs, ks, vs, carries, sfx, mask):
    bk = ks[0].shape[0]
    n = range(len(qs))
    zs = [lax.dot_general(qs[i], ks[i], (((1,), (1,)), ((), ())), preferred_element_type=F32) * HEAD_DIM ** -0.5
          for i in n]
    sps = [jnp.maximum(z, 0.0) + jnp.log(1.0 + jnp.exp(-jnp.abs(z))) for z in zs]
    l1m = sps if mask is None else [jnp.where(mask, sp, 0.0) for sp in sps]
    his = [x.astype(BF16) for x in l1m]
    los = [(x - hi.astype(F32)).astype(BF16) for x, hi in zip(l1m, his)]
    r2s = [jnp.dot(hi, sfx, preferred_element_type=F32) + jnp.dot(lo, sfx, preferred_element_type=F32)
           for hi, lo in zip(his, los)]
    ws = [jnp.exp(zs[i] - sps[i] + r2s[i][:, :bk] + carries[i]) for i in n]
    if mask is not None:
        ws = [jnp.where(mask, w, 0.0) for w in ws]
    pvs = [jnp.dot(ws[i].astype(BF16), vs[i], preferred_element_type=F32) for i in n]
    return [carries[i] + r2s[i][:, bk:] for i in n], pvs


def _sb_prompt_kernel(q_ref, k_ref, v_ref, z_ref, o_ref, c_scr, acc_scr, *, n_groups, group):
    blk = SB_BLOCK
    qt = pl.program_id(1)
    sfx = _suffix_matrix(blk)
    row = lax.broadcasted_iota(jnp.int32, (blk, blk), 0)
    col = lax.broadcasted_iota(jnp.int32, (blk, blk), 1)
    causal = col < row
    alive = lambda cs: (functools.reduce(jnp.maximum, [jnp.max(c) for c in cs]) >= SB_LOG_CUTOFF).astype(jnp.int32)

    def kv(kb):
        start = pl.multiple_of(kb * blk, blk)
        return k_ref[pl.ds(start, blk), :], v_ref[pl.ds(start, blk), :]

    def qgroup(ig, _):
        gq0 = (qt * n_groups + ig) * group
        rows = [pl.ds(pl.multiple_of((ig * group + g) * blk, blk), blk) for g in range(group)]
        qs = [q_ref[r, :] for r in rows]
        kvs = [kv(gq0 + g) for g in range(group)]
        zero = jnp.zeros((blk, blk), F32)
        cs, pvs = _sb_tiles(qs, [k for k, _ in kvs], [v for _, v in kvs], [zero] * group, sfx, causal)
        for g in range(group):
            c_scr[g] = cs[g]
            acc_scr[g] = pvs[g]

        def cond(st):
            s, go = st
            return jnp.logical_and(s <= gq0 + group - 1, go > 0)

        def body(st):
            s, _ = st
            kbs = [gq0 + g - s for g in range(group)]
            kvs = [kv(jnp.maximum(kb, 0)) for kb in kbs]
            cs, pvs = _sb_tiles(qs, [k for k, _ in kvs], [v for _, v in kvs],
                                [c_scr[g] for g in range(group)], sfx, None)
            for g in range(group):
                c_scr[g] = cs[g]
                acc_scr[g] += jnp.where(kbs[g] >= 0, pvs[g], 0.0)
            return s + 1, alive(cs)

        lax.while_loop(cond, body, (1, alive(cs)))
        for g in range(group):
            o_ref[rows[g], :] = (acc_scr[g] * _silu(z_ref[rows[g], :])).astype(BF16)
        return 0

    lax.fori_loop(0, n_groups, qgroup, 0)


def _sb_prompt_call(q, k, v, z, *, tq=2048, group=8):
    t = q.shape[0]
    tq = min(tq, t)
    assert t % tq == 0 and tq % (SB_BLOCK * group) == 0
    qspec = pl.BlockSpec((tq, HEAD_DIM), lambda h, i: (i, h))
    kvspec = pl.BlockSpec((t, HEAD_DIM), lambda h, i: (0, h))
    return pl.pallas_call(
        functools.partial(_sb_prompt_kernel, n_groups=tq // (SB_BLOCK * group), group=group),
        grid=(N_HEADS, t // tq),
        in_specs=[qspec, kvspec, kvspec, qspec],
        out_specs=qspec,
        out_shape=jax.ShapeDtypeStruct((t, WIDTH), BF16),
        scratch_shapes=[pltpu.VMEM((group, SB_BLOCK, SB_BLOCK), F32), pltpu.VMEM((group, SB_BLOCK, HEAD_DIM), F32)],
        compiler_params=_cparams("arbitrary", "arbitrary"),
        name="sb_prompt",
    )(q, k, v, z)


def _sb_sample_kernel(q_ref, kn_ref, vn_ref, z_ref, kc_hbm, vc_hbm, o_ref, kbuf, vbuf, sem, c_scr, acc_scr, *, past):
    blk = SB_BLOCK
    nh = N_HEADS
    b = pl.program_id(0)
    tq = q_ref.shape[0]
    half = blk - tq
    n_full = (past - half) // blk
    rem = (past - half) % blk
    sfx = _suffix_matrix(blk)
    row = lax.broadcasted_iota(jnp.int32, (tq, blk), 0)
    col = lax.broadcasted_iota(jnp.int32, (tq, blk), 1)
    heads = [slice(h * HEAD_DIM, (h + 1) * HEAD_DIM) for h in range(nh)]
    alive = lambda cs: (functools.reduce(jnp.maximum, [jnp.max(c) for c in cs]) >= SB_LOG_CUTOFF).astype(jnp.int32)

    def copies(key0, nkeys, slot):
        src = pl.ds(key0 * nh, nkeys * nh)
        dst = pl.ds(0, nkeys * nh)
        return (pltpu.make_async_copy(kc_hbm.at[b, src, :], kbuf.at[slot, dst, :], sem.at[0, slot]),
                pltpu.make_async_copy(vc_hbm.at[b, src, :], vbuf.at[slot, dst, :], sem.at[1, slot]))

    def start(cps):
        for cp in cps:
            cp.start()

    def wait(cps):
        for cp in cps:
            cp.wait()

    def tile_copies(j):
        return copies(past - half - (j + 1) * blk, blk, (j + 1) % 2)

    def cached(buf, slot, h, nkeys):
        return buf[slot, pl.ds(h, nkeys, stride=nh), :].astype(BF16)

    start(copies(past - half, half, 0))
    if n_full:
        start(tile_copies(0))
    wait(copies(past - half, half, 0))

    qs = [q_ref[:, heads[h]] for h in range(nh)]
    k0 = [jnp.concatenate([cached(kbuf, 0, h, half), kn_ref[:, heads[h]]], axis=0) for h in range(nh)]
    v0 = [jnp.concatenate([cached(vbuf, 0, h, half), vn_ref[:, heads[h]]], axis=0) for h in range(nh)]
    cs, pvs = _sb_tiles(qs, k0, v0, [jnp.zeros((tq, blk), F32)] * nh, sfx, col < row + half)
    for h in range(nh):
        c_scr[h] = cs[h]
        acc_scr[h] = pvs[h]

    def sweep(slot, mask):
        cs, pvs = _sb_tiles(qs, [cached(kbuf, slot, h, blk) for h in range(nh)],
                            [cached(vbuf, slot, h, blk) for h in range(nh)],
                            [c_scr[h] for h in range(nh)], sfx, mask)
        for h in range(nh):
            c_scr[h] = cs[h]
            acc_scr[h] += pvs[h]
        return cs

    def cond(st):
        j, go = st
        return jnp.logical_and(j < n_full, go > 0)

    def body(st):
        j, _ = st
        slot = (j + 1) % 2
        wait(tile_copies(j))

        @pl.when(j + 1 < n_full)
        def _():
            start(tile_copies(j + 1))

        return j + 1, alive(sweep(slot, None))

    done, go = lax.while_loop(cond, body, (0, alive(cs)))

    @pl.when(done < n_full)
    def _():
        wait(tile_copies(done))

    if rem:
        @pl.when(go > 0)
        def _():
            cps = copies(0, blk, 0)
            start(cps)
            wait(cps)
            sweep(0, col < rem)

    for h in range(nh):
        o_ref[:, heads[h]] = (acc_scr[h] * _silu(z_ref[:, heads[h]])).astype(BF16)


def _sb_sample_call(q, kn, vn, z, kc, vc):
    nb = kc.shape[0]
    past = kc.shape[1] // N_HEADS
    tq = q.shape[0] // nb
    assert tq % 16 == 0 and tq < SB_BLOCK and past >= SB_BLOCK
    new = pl.BlockSpec((tq, WIDTH), lambda b: (b, 0))
    hbm = pl.BlockSpec(memory_space=pl.ANY)
    return pl.pallas_call(
        functools.partial(_sb_sample_kernel, past=past),
        grid=(nb,),
        in_specs=[new, new, new, new, hbm, hbm],
        out_specs=new,
        out_shape=jax.ShapeDtypeStruct(q.shape, BF16),
        scratch_shapes=[pltpu.VMEM((2, SB_BLOCK * N_HEADS, HEAD_DIM), F32),
                        pltpu.VMEM((2, SB_BLOCK * N_HEADS, HEAD_DIM), F32),
                        pltpu.SemaphoreType.DMA((2, 2)),
                        pltpu.VMEM((N_HEADS, tq, SB_BLOCK), F32), pltpu.VMEM((N_HEADS, tq, HEAD_DIM), F32)],
        compiler_params=_cparams("arbitrary"),
        name="sb_sample",
    )(q, kn, vn, z, kc, vc)


def _prefix_matrix(c):
    t = lax.broadcasted_iota(jnp.int32, (2 * c, c), 0)
    s = lax.broadcasted_iota(jnp.int32, (2 * c, c), 1)
    incl = (t < c) & (s <= t)
    sub = (t >= c) & (s < ((t - c) // HG_SUB) * HG_SUB)
    return jnp.where(incl | sub, 1.0, 0.0).astype(BF16)


def _hgrn_chunk(lf, qh, kh, v, sts, pfx, tril):
    c = lf.shape[0]
    heads = [slice(h * HEAD_DIM, (h + 1) * HEAD_DIM) for h in range(len(sts))]
    p0 = lf.astype(BF16)
    r1 = lf - p0.astype(F32)
    p1 = r1.astype(BF16)
    p2 = (r1 - p1.astype(F32)).astype(BF16)
    br = (jnp.dot(pfx, p0, preferred_element_type=F32) + jnp.dot(pfx, p1, preferred_element_type=F32)
          + jnp.dot(pfx, p2, preferred_element_type=F32))
    b = br[:c]
    r = br[c:]
    b_last = b[c - 1:c, :]
    vb = v.astype(BF16)
    q_sub = (qh * jnp.exp(b - r)).astype(BF16)
    q_dec = (qh * jnp.exp(b)).astype(BF16)
    k_end = (kh * jnp.exp(b_last - b)).astype(BF16)
    dec = jnp.exp(b_last)
    att = [[] for _ in heads]
    for i in range(c // HG_SUB):
        lo, hi = i * HG_SUB, (i + 1) * HG_SUB
        k_i = (kh[:hi] * jnp.exp(r[lo:lo + 1, :] - b[:hi])).astype(BF16)
        if hi < c:
            k_i = jnp.concatenate([k_i, jnp.zeros((c - hi, k_i.shape[1]), BF16)], axis=0)
        for h, hs in enumerate(heads):
            att[h].append(lax.dot_general(q_sub[lo:hi, hs], k_i[:, hs], (((1,), (1,)), ((), ())),
                                          preferred_element_type=F32))
    att = [jnp.where(tril, jnp.concatenate(a, axis=0), 0.0).astype(BF16) for a in att]
    outs, new_sts = [], []
    for h, hs in enumerate(heads):
        o = jnp.dot(att[h], vb[:, hs], preferred_element_type=F32)
        o = o + lax.dot_general(q_dec[:, hs], sts[h].astype(BF16), (((1,), (1,)), ((), ())),
                                preferred_element_type=F32)
        outs.append(o)
    for h, hs in enumerate(heads):
        new_sts.append(sts[h] * dec[:, hs] + lax.dot_general(vb[:, hs], k_end[:, hs], (((0,), (0,)), ((), ())),
                                                             preferred_element_type=F32))
    return outs, new_sts


def _hgrn_kernel(lf_ref, qh_ref, kh_ref, v_ref, zh_ref, g_ref, s0_ref, o_ref, s_ref, st_scr, *, chunk, n_chunks):
    tt = pl.program_id(2)
    nh = st_scr.shape[0]
    heads = [slice(h * HEAD_DIM, (h + 1) * HEAD_DIM) for h in range(nh)]

    @pl.when(tt == 0)
    def _():
        for h in range(nh):
            st_scr[h] = s0_ref[0, h].T

    pfx = _prefix_matrix(chunk)
    ti = lax.broadcasted_iota(jnp.int32, (chunk, chunk), 0)
    si = lax.broadcasted_iota(jnp.int32, (chunk, chunk), 1)
    tril = si <= ti

    def step(ci, _):
        r0 = pl.multiple_of(ci * chunk, chunk)
        rs = pl.ds(r0, chunk)
        outs, sts = _hgrn_chunk(lf_ref[rs, :], qh_ref[rs, :], kh_ref[rs, :], v_ref[rs, :],
                                [st_scr[h] for h in range(nh)], pfx, tril)
        for h in range(nh):
            st_scr[h] = sts[h]
            o = outs[h]
            ms = jnp.mean(o * o, axis=-1, keepdims=True)
            o_ref[rs, heads[h]] = (o * lax.rsqrt(ms + EPS) * g_ref[h] * _silu(zh_ref[rs, heads[h]])).astype(BF16)
        return 0

    lax.fori_loop(0, n_chunks, step, 0)

    @pl.when(tt == pl.num_programs(2) - 1)
    def _():
        for h in range(nh):
            s_ref[0, h] = st_scr[h].T


def _hgrn_call(lf, qh, kh, v, zh, gain, s0, *, chunk, tile, heads):
    nb = s0.shape[0]
    t = lf.shape[0] // nb
    tile = min(tile, t)
    assert t % tile == 0 and tile % chunk == 0 and chunk % HG_SUB == 0 and N_HEADS % heads == 0
    nt = t // tile
    tok = pl.BlockSpec((tile, heads * HEAD_DIM), lambda b, h, i: (b * nt + i, h))
    state = pl.BlockSpec((1, heads, HEAD_DIM, HEAD_DIM), lambda b, h, i: (b, h, 0, 0))
    return pl.pallas_call(
        functools.partial(_hgrn_kernel, chunk=chunk, n_chunks=tile // chunk),
        grid=(nb, N_HEADS // heads, nt),
        in_specs=[tok, tok, tok, tok, tok, pl.BlockSpec((heads, 1, HEAD_DIM), lambda b, h, i: (h, 0, 0)), state],
        out_specs=[tok, state],
        out_shape=[jax.ShapeDtypeStruct(lf.shape, BF16), jax.ShapeDtypeStruct(s0.shape, F32)],
        scratch_shapes=[pltpu.VMEM((heads, HEAD_DIM, HEAD_DIM), F32)],
        compiler_params=_cparams("arbitrary", "arbitrary", "arbitrary"),
        name="hgrn2",
    )(lf, qh, kh, v, zh, gain, s0)


def _out_kernel(gs_ref, gh_ref, gsb_ref, ghg_ref, x_ref, gate_ref, wsb_ref, whg_ref, wo_ref, y_ref):
    nb, tr, d = x_ref.shape
    y_sb = jnp.dot(gs_ref[...], wsb_ref[...], preferred_element_type=F32)
    y_h = jnp.dot(gh_ref[...], whg_ref[...], preferred_element_type=F32)
    merged = jax.nn.sigmoid(gsb_ref[...]) * y_sb + jax.nn.sigmoid(ghg_ref[...]) * y_h
    upd = jnp.dot(merged.astype(BF16), wo_ref[...], preferred_element_type=F32)
    y_ref[...] = x_ref[...] + gate_ref[...] * upd.reshape(nb, tr, d)


def _out_call(gs, gh, gg, x, gate, wsb, whg, wo, nb, tr):
    n, t, d = x.shape
    tm = nb * tr
    nt = t // tr
    rowblk = lambda w, c: pl.BlockSpec((tm, w), lambda i, j: (i * nt + j if nb == 1 else i, c))
    const = lambda a: pl.BlockSpec(a.shape, lambda i, j: (0, 0), pipeline_mode=pl.Buffered(1))
    return pl.pallas_call(
        _out_kernel,
        grid=(n // nb, nt),
        in_specs=[rowblk(WIDTH, 0), rowblk(WIDTH, 0), rowblk(d, 0), rowblk(d, 1),
                  pl.BlockSpec((nb, tr, d), lambda i, j: (i, j, 0)),
                  pl.BlockSpec((nb, 1, d), lambda i, j: (i, 0, 0)),
                  const(wsb), const(whg), const(wo)],
        out_specs=pl.BlockSpec((nb, tr, d), lambda i, j: (i, j, 0)),
        out_shape=jax.ShapeDtypeStruct(x.shape, F32),
        compiler_params=_cparams("arbitrary", "arbitrary"),
        name="merge_out",
    )(gs, gh, gg, gg, x, gate, wsb, whg, wo)


def _stream(x, shift, scale, gate, p, *, nb, tr, hg_chunk, hg_tile, hg_heads, s0, cache=None):
    n, t, d = x.shape
    rows = n * t
    h = _prenorm_call(x, p["norm_gain"], scale, shift, nb, tr).reshape(rows, d)
    w_in = p["w_in"]
    tm = min(1024, rows)
    proj = functools.partial(_proj_call, h, w_in, tm=tm)
    (q_sb,) = proj(0 * WIDTH, WIDTH, "norm_q", (p["q_gain"],))
    k_sb, k_bf = proj(1 * WIDTH, WIDTH, "norm_k", (p["k_gain"],))
    v_sb, v_bf = proj(2 * WIDTH, WIDTH, "copy2")
    (z_sb,) = proj(3 * WIDTH, WIDTH, "plain")
    logf, k_h = proj(4 * WIDTH, WIDTH, "forget", (p["lb_raw"],))
    (i_h,) = proj(5 * WIDTH, WIDTH, "plain")
    (q_h,) = proj(6 * WIDTH, WIDTH, "silu")
    (z_h,) = proj(7 * WIDTH, WIDTH, "plain")
    (gg,) = proj(8 * WIDTH, 2 * d, "plain")

    if cache is None:
        assert n == 1
        gs = _sb_prompt_call(q_sb, k_bf, v_bf, z_sb)
    else:
        gs = _sb_sample_call(q_sb, k_bf, v_bf, z_sb, cache[0], cache[1])
    gh, s_new = _hgrn_call(logf, q_h, k_h, i_h, z_h, p["onorm_gain"], s0, chunk=hg_chunk, tile=hg_tile,
                           heads=hg_heads)
    y = _out_call(gs, gh, gg, x, gate, p["w_br_sb"], p["w_br_hg"], p["w_out"], nb, tr)
    k_new = k_sb.reshape(1, n, t, N_HEADS, HEAD_DIM)
    v_new = v_sb.reshape(1, n, t, N_HEADS, HEAD_DIM)
    return y, k_new, v_new, s_new[None]


def kernel(x_prompt, x_sample, cache_sb_k, cache_sb_v, state_hgrn, c_prompt, c_sample, norm_gain, w_ada, b_ada, w_in, q_norm_gain, k_norm_gain, hgrn_lb_raw, hgrn_onorm_gain, w_branch_sb, w_branch_hgrn, w_out):
    assert w_in.shape[0] == 1, "single-layer trunk"
    n_p, t_p, d = x_prompt.shape
    n_s, t_s, _ = x_sample.shape
    past = cache_sb_k.shape[2]

    c_all = jnp.concatenate([c_prompt, c_sample], axis=0)
    pad = (-c_all.shape[0]) % 8
    c_all = jnp.pad(c_all, ((0, pad), (0, 0)))
    mod = _ada_call(c_all, w_ada[0], b_ada[0].reshape(1, 3 * d))
    shift, scale, gate = (mod[:, i * d:(i + 1) * d] for i in range(3))
    vec = lambda a, lo, hi: a[lo:hi].reshape(hi - lo, 1, d)

    p = {
        "norm_gain": norm_gain[0].reshape(1, 1, d),
        "w_in": w_in[0],
        "q_gain": q_norm_gain[0].reshape(1, HEAD_DIM),
        "k_gain": k_norm_gain[0].reshape(1, HEAD_DIM),
        "lb_raw": hgrn_lb_raw,
        "onorm_gain": hgrn_onorm_gain[0].reshape(N_HEADS, 1, HEAD_DIM),
        "w_br_sb": w_branch_sb[0].astype(BF16),
        "w_br_hg": w_branch_hgrn[0].astype(BF16),
        "w_out": w_out[0].astype(BF16),
    }

    y_p, k_p, v_p, s_p = _stream(
        x_prompt, vec(shift, 0, n_p), vec(scale, 0, n_p), vec(gate, 0, n_p), p,
        nb=1, tr=min(256, t_p), hg_chunk=min(128, t_p), hg_tile=1024, hg_heads=4,
        s0=jnp.zeros((n_p, N_HEADS, HEAD_DIM, HEAD_DIM), F32))
    nb_s = max(1, min(n_s, 256 // t_s))
    y_s, k_s, v_s, s_s = _stream(
        x_sample, vec(shift, n_p, n_p + n_s), vec(scale, n_p, n_p + n_s), vec(gate, n_p, n_p + n_s), p,
        nb=nb_s, tr=t_s, hg_chunk=t_s, hg_tile=t_s, hg_heads=N_HEADS, s0=state_hgrn[0],
        cache=(cache_sb_k.reshape(n_s, past * N_HEADS, HEAD_DIM), cache_sb_v.reshape(n_s, past * N_HEADS, HEAD_DIM)))
    return (y_p, y_s, k_p, v_p, s_p, k_s, v_s, s_s)
```

```python
import functools

import jax
import jax.numpy as jnp
from jax import lax
from jax.experimental import pallas as pl
from jax.experimental.pallas import tpu as pltpu

F32 = jnp.float32
BF16 = jnp.bfloat16

N_HEADS = 8
HEAD_DIM = 128
WIDTH = N_HEADS * HEAD_DIM
HG_SUB = 16
EPS = 1e-6
SB_BLOCK = 128
SB_LOG_CUTOFF = -88.0
VMEM_LIMIT = 56 * 1024 * 1024


def _cparams(*sem):
    return pltpu.CompilerParams(dimension_semantics=sem, vmem_limit_bytes=VMEM_LIMIT)


def _silu(x):
    return x * jax.nn.sigmoid(x)


def _ada_kernel(c_ref, w_ref, b_ref, o_ref):
    c = c_ref[...]
    a = _silu(c).astype(BF16)
    o_ref[...] = jnp.dot(a, w_ref[...].astype(BF16), preferred_element_type=F32) + b_ref[...]


def _ada_call(c, w, b):
    r, d = c.shape
    n = w.shape[1]
    tn = 1024
    return pl.pallas_call(
        _ada_kernel,
        grid=(n // tn,),
        in_specs=[pl.BlockSpec((r, d), lambda j: (0, 0)),
                  pl.BlockSpec((d, tn), lambda j: (0, j)),
                  pl.BlockSpec((1, tn), lambda j: (0, j))],
        out_specs=pl.BlockSpec((r, tn), lambda j: (0, j)),
        out_shape=jax.ShapeDtypeStruct((r, n), F32),
        compiler_params=_cparams("arbitrary"),
        name="ada_mod",
    )(c, w, b)


def _prenorm_kernel(x_ref, g_ref, sc_ref, sh_ref, h_ref):
    x = x_ref[...]
    ms = jnp.mean(x * x, axis=-1, keepdims=True)
    xn = x * lax.rsqrt(ms + EPS)
    h = xn * g_ref[...] * (1.0 + sc_ref[...]) + sh_ref[...]
    h_ref[...] = h.astype(BF16)


def _prenorm_call(x, gain, scale, shift, nb, tr):
    n, t, d = x.shape
    vec = pl.BlockSpec((nb, 1, d), lambda i, j: (i, 0, 0))
    return pl.pallas_call(
        _prenorm_kernel,
        grid=(n // nb, t // tr),
        in_specs=[pl.BlockSpec((nb, tr, d), lambda i, j: (i, j, 0)),
                  pl.BlockSpec((1, 1, d), lambda i, j: (0, 0, 0)), vec, vec],
        out_specs=pl.BlockSpec((nb, tr, d), lambda i, j: (i, j, 0)),
        out_shape=jax.ShapeDtypeStruct((n, t, d), BF16),
        compiler_params=_cparams("arbitrary", "arbitrary"),
        name="prenorm",
    )(x, gain, scale, shift)


def _head_rms(y, gain):
    outs = []
    for g in range(N_HEADS):
        yh = y[:, g * HEAD_DIM:(g + 1) * HEAD_DIM]
        ms = jnp.mean(yh * yh, axis=-1, keepdims=True)
        outs.append(yh * lax.rsqrt(ms + EPS) * gain)
    return outs


def _proj_kernel(*refs, kind):
    h_ref, w_ref = refs[0], refs[1]
    wb_ref = refs[-1]

    @pl.when(pl.program_id(1) == 0)
    def _():
        wb_ref[...] = w_ref[...].astype(BF16)

    y = jnp.dot(h_ref[...], wb_ref[...], preferred_element_type=F32)
    tm = y.shape[0]
    if kind == "plain":
        refs[2][...] = y
    elif kind == "plain_bf16":
        refs[2][...] = y.astype(BF16)
    elif kind == "silu":
        refs[2][...] = _silu(y)
    elif kind == "copy2":
        for g in range(N_HEADS):
            refs[2][pl.ds(g, tm, stride=N_HEADS), :] = y[:, g * HEAD_DIM:(g + 1) * HEAD_DIM]
        refs[3][...] = y.astype(BF16)
    elif kind == "norm_q":
        gain = refs[2][...]
        for g, o in enumerate(_head_rms(y, gain)):
            refs[3][:, g * HEAD_DIM:(g + 1) * HEAD_DIM] = o.astype(BF16)
    elif kind == "norm_k":
        gain = refs[2][...]
        for g, o in enumerate(_head_rms(y, gain)):
            refs[3][pl.ds(g, tm, stride=N_HEADS), :] = o
            refs[4][:, g * HEAD_DIM:(g + 1) * HEAD_DIM] = o.astype(BF16)
    elif kind == "forget":
        raw = refs[2][...]
        e = jnp.exp(raw - jnp.max(raw, axis=0, keepdims=True))
        lb = e[0:1, :] / jnp.sum(e, axis=0, keepdims=True)
        f = lb + (1.0 - lb) * jax.nn.sigmoid(y)
        refs[3][...] = jnp.log(f)
        refs[4][...] = 1.0 - f
    else:
        raise ValueError(kind)


def _proj_call(h, w_in, col0, ncols, kind, extra=(), *, tm=512, tn=1024):
    rows, d = h.shape
    assert col0 % tn == 0 and ncols % tn == 0 and rows % tm == 0
    jb = col0 // tn
    grid = (ncols // tn, rows // tm)
    tile = lambda: pl.BlockSpec((tm, tn), lambda j, i: (i, j))
    in_specs = [pl.BlockSpec((tm, d), lambda j, i: (i, 0)),
                pl.BlockSpec((d, tn), lambda j, i: (0, jb + j))]
    for e in extra:
        in_specs.append(pl.BlockSpec(e.shape, lambda j, i: (0, 0)))
    if kind in ("plain", "silu"):
        out_dt = (F32,)
    elif kind in ("norm_q", "plain_bf16"):
        out_dt = (BF16,)
    elif kind in ("copy2", "norm_k"):
        out_dt = (F32, BF16)
    else:
        out_dt = (F32, F32)
    out_specs = [tile() for _ in out_dt]
    out_shape = [jax.ShapeDtypeStruct((rows, ncols), dt) for dt in out_dt]
    if kind in ("copy2", "norm_k"):
        assert ncols == WIDTH
        out_specs[0] = pl.BlockSpec((tm * N_HEADS, HEAD_DIM), lambda j, i: (i, 0))
        out_shape[0] = jax.ShapeDtypeStruct((rows * N_HEADS, HEAD_DIM), F32)
    outs = pl.pallas_call(
        functools.partial(_proj_kernel, kind=kind),
        grid=grid,
        in_specs=in_specs,
        out_specs=out_specs,
        out_shape=out_shape,
        scratch_shapes=[pltpu.VMEM((d, tn), BF16)],
        compiler_params=_cparams("arbitrary", "arbitrary"),
        name="proj_" + kind,
    )(h, w_in, *extra)
    return outs


def _suffix_matrix(bk):
    j = lax.broadcasted_iota(jnp.int32, (bk, 2 * bk), 0)
    s = lax.broadcasted_iota(jnp.int32, (bk, 2 * bk), 1)
    return jnp.where((j > s) | (s >= bk), -1.0, 0.0).astype(BF16)


def _sb_tiles(qs, ks, vs, carries, sfx, mask):
    bk = ks[0].shape[0]
    n = range(len(qs))
    zs = [lax.dot_general(qs[i], ks[i], (((1,), (1,)), ((), ())), preferred_element_type=F32) * HEAD_DIM ** -0.5
          for i in n]
    sps = [jnp.maximum(z, 0.0) + jnp.log(1.0 + jnp.exp(-jnp.abs(z))) for z in zs]
    l1m = sps if mask is None else [jnp.where(mask, sp, 0.0) for sp in sps]
    his = [x.astype(BF16) for x in l1m]
    los = [(x - hi.astype(F32)).astype(BF16) for x, hi in zip(l1m, his)]
    r2s = [jnp.dot(hi, sfx, preferred_element_type=F32) + jnp.dot(lo, sfx, preferred_element_type=F32)
           for hi, lo in zip(his, los)]
    ws = [jnp.exp(zs[i] - sps[i] + r2s[i][:, :bk] + carries[i]) for i in n]
    if mask is not None:
        ws = [jnp.where(mask, w, 0.0) for w in ws]
    pvs = [jnp.dot(ws[i].astype(BF16), vs[i], preferred_element_type=F32) for i in n]
    return [carries[i] + r2s[i][:, bk:] for i in n], pvs


def _sb_prompt_kernel(q_ref, k_ref, v_ref, z_ref, o_ref, c_scr, acc_scr, *, n_groups, group):
    blk = SB_BLOCK
    qt = pl.program_id(1)
    sfx = _suffix_matrix(blk)
    row = lax.broadcasted_iota(jnp.int32, (blk, blk), 0)
    col = lax.broadcasted_iota(jnp.int32, (blk, blk), 1)
    causal = col < row
    alive = lambda cs: (functools.reduce(jnp.maximum, [jnp.max(c) for c in cs]) >= SB_LOG_CUTOFF).astype(jnp.int32)

    def kv(kb):
        start = pl.multiple_of(kb * blk, blk)
        return k_ref[pl.ds(start, blk), :], v_ref[pl.ds(start, blk), :]

    def qgroup(ig, _):
        gq0 = (qt * n_groups + ig) * group
        rows = [pl.ds(pl.multiple_of((ig * group + g) * blk, blk), blk) for g in range(group)]
        qs = [q_ref[r, :] for r in rows]
        kvs = [kv(gq0 + g) for g in range(group)]
        zero = jnp.zeros((blk, blk), F32)
        cs, pvs = _sb_tiles(qs, [k for k, _ in kvs], [v for _, v in kvs], [zero] * group, sfx, causal)
        for g in range(group):
            c_scr[g] = cs[g]
            acc_scr[g] = pvs[g]

        def cond(st):
            s, go = st
            return jnp.logical_and(s <= gq0 + group - 1, go > 0)

        def body(st):
            s, _ = st
            kbs = [gq0 + g - s for g in range(group)]
            kvs = [kv(jnp.maximum(kb, 0)) for kb in kbs]
            cs, pvs = _sb_tiles(qs, [k for k, _ in kvs], [v for _, v in kvs],
                                [c_scr[g] for g in range(group)], sfx, None)
            for g in range(group):
                c_scr[g] = cs[g]
                acc_scr[g] += jnp.where(kbs[g] >= 0, pvs[g], 0.0)
            return s + 1, alive(cs)

        lax.while_loop(cond, body, (1, alive(cs)))
        for g in range(group):
            o_ref[rows[g], :] = (acc_scr[g] * _silu(z_ref[rows[g], :])).astype(BF16)
        return 0

    lax.fori_loop(0, n_groups, qgroup, 0)


def _sb_prompt_call(q, k, v, z, *, tq=2048, group=8):
    t = q.shape[0]
    tq = min(tq, t)
    assert t % tq == 0 and tq % (SB_BLOCK * group) == 0
    qspec = pl.BlockSpec((tq, HEAD_DIM), lambda h, i: (i, h))
    kvspec = pl.BlockSpec((t, HEAD_DIM), lambda h, i: (0, h))
    return pl.pallas_call(
        functools.partial(_sb_prompt_kernel, n_groups=tq // (SB_BLOCK * group), group=group),
        grid=(N_HEADS, t // tq),
        in_specs=[qspec, kvspec, kvspec, qspec],
        out_specs=qspec,
        out_shape=jax.ShapeDtypeStruct((t, WIDTH), BF16),
        scratch_shapes=[pltpu.VMEM((group, SB_BLOCK, SB_BLOCK), F32), pltpu.VMEM((group, SB_BLOCK, HEAD_DIM), F32)],
        compiler_params=_cparams("arbitrary", "arbitrary"),
        name="sb_prompt",
    )(q, k, v, z)


def _sb_sample_kernel(q_ref, kn_ref, vn_ref, z_ref, kc_hbm, vc_hbm, o_ref, kbuf, vbuf, sem, c_scr, acc_scr, *, past):
    blk = SB_BLOCK
    nh = N_HEADS
    b = pl.program_id(0)
    tq = q_ref.shape[0]
    half = blk - tq
    n_full = (past - half) // blk
    rem = (past - half) % blk
    sfx = _suffix_matrix(blk)
    row = lax.broadcasted_iota(jnp.int32, (tq, blk), 0)
    col = lax.broadcasted_iota(jnp.int32, (tq, blk), 1)
    heads = [slice(h * HEAD_DIM, (h + 1) * HEAD_DIM) for h in range(nh)]
    alive = lambda cs: (functools.reduce(jnp.maximum, [jnp.max(c) for c in cs]) >= SB_LOG_CUTOFF).astype(jnp.int32)

    def copies(key0, nkeys, slot):
        src = pl.ds(key0 * nh, nkeys * nh)
        dst = pl.ds(0, nkeys * nh)
        return (pltpu.make_async_copy(kc_hbm.at[b, src, :], kbuf.at[slot, dst, :], sem.at[0, slot]),
                pltpu.make_async_copy(vc_hbm.at[b, src, :], vbuf.at[slot, dst, :], sem.at[1, slot]))

    def start(cps):
        for cp in cps:
            cp.start()

    def wait(cps):
        for cp in cps:
            cp.wait()

    def tile_copies(j):
        return copies(past - half - (j + 1) * blk, blk, (j + 1) % 2)

    def cached(buf, slot, h, nkeys):
        return buf[slot, pl.ds(h, nkeys, stride=nh), :].astype(BF16)

    start(copies(past - half, half, 0))
    if n_full:
        start(tile_copies(0))
    wait(copies(past - half, half, 0))

    qs = [q_ref[:, heads[h]] for h in range(nh)]
    k0 = [jnp.concatenate([cached(kbuf, 0, h, half), kn_ref[:, heads[h]]], axis=0) for h in range(nh)]
    v0 = [jnp.concatenate([cached(vbuf, 0, h, half), vn_ref[:, heads[h]]], axis=0) for h in range(nh)]
    cs, pvs = _sb_tiles(qs, k0, v0, [jnp.zeros((tq, blk), F32)] * nh, sfx, col < row + half)
    for h in range(nh):
        c_scr[h] = cs[h]
        acc_scr[h] = pvs[h]

    def sweep(slot, mask):
        cs, pvs = _sb_tiles(qs, [cached(kbuf, slot, h, blk) for h in range(nh)],
                            [cached(vbuf, slot, h, blk) for h in range(nh)],
                            [c_scr[h] for h in range(nh)], sfx, mask)
        for h in range(nh):
            c_scr[h] = cs[h]
            acc_scr[h] += pvs[h]
        return cs

    def cond(st):
        j, go = st
        return jnp.logical_and(j < n_full, go > 0)

    def body(st):
        j, _ = st
        slot = (j + 1) % 2
        wait(tile_copies(j))

        @pl.when(j + 1 < n_full)
        def _():
            start(tile_copies(j + 1))

        return j + 1, alive(sweep(slot, None))

    done, go = lax.while_loop(cond, body, (0, alive(cs)))

    @pl.when(done < n_full)
    def _():
        wait(tile_copies(done))

    if rem:
        @pl.when(go > 0)
        def _():
            cps = copies(0, blk, 0)
            start(cps)
            wait(cps)
            sweep(0, col < rem)

    for h in range(nh):
        o_ref[:, heads[h]] = (acc_scr[h] * _silu(z_ref[:, heads[h]])).astype(BF16)


def _sb_sample_call(q, kn, vn, z, kc, vc):
    nb = kc.shape[0]
    past = kc.shape[1] // N_HEADS
    tq = q.shape[0] // nb
    assert tq % 16 == 0 and tq < SB_BLOCK and past >= SB_BLOCK
    new = pl.BlockSpec((tq, WIDTH), lambda b: (b, 0))
    hbm = pl.BlockSpec(memory_space=pl.ANY)
    return pl.pallas_call(
        functools.partial(_sb_sample_kernel, past=past),
        grid=(nb,),
        in_specs=[new, new, new, new, hbm, hbm],
        out_specs=new,
        out_shape=jax.ShapeDtypeStruct(q.shape, BF16),
        scratch_shapes=[pltpu.VMEM((2, SB_BLOCK * N_HEADS, HEAD_DIM), F32),
                        pltpu.VMEM((2, SB_BLOCK * N_HEADS, HEAD_DIM), F32),
                        pltpu.SemaphoreType.DMA((2, 2)),
                        pltpu.VMEM((N_HEADS, tq, SB_BLOCK), F32), pltpu.VMEM((N_HEADS, tq, HEAD_DIM), F32)],
        compiler_params=_cparams("arbitrary"),
        name="sb_sample",
    )(q, kn, vn, z, kc, vc)


def _prefix_matrix(c):
    t = lax.broadcasted_iota(jnp.int32, (2 * c, c), 0)
    s = lax.broadcasted_iota(jnp.int32, (2 * c, c), 1)
    incl = (t < c) & (s <= t)
    sub = (t >= c) & (s < ((t - c) // HG_SUB) * HG_SUB)
    return jnp.where(incl | sub, 1.0, 0.0).astype(BF16)


def _hgrn_chunk(lf, qh, kh, v, sts, pfx, tril):
    c = lf.shape[0]
    heads = [slice(h * HEAD_DIM, (h + 1) * HEAD_DIM) for h in range(len(sts))]
    p0 = lf.astype(BF16)
    r1 = lf - p0.astype(F32)
    p1 = r1.astype(BF16)
    p2 = (r1 - p1.astype(F32)).astype(BF16)
    br = (jnp.dot(pfx, p0, preferred_element_type=F32) + jnp.dot(pfx, p1, preferred_element_type=F32)
          + jnp.dot(pfx, p2, preferred_element_type=F32))
    b = br[:c]
    r = br[c:]
    b_last = b[c - 1:c, :]
    vb = v
    q_sub = (qh * jnp.exp(b - r)).astype(BF16)
    q_dec = (qh * jnp.exp(b)).astype(BF16)
    k_end = (kh * jnp.exp(b_last - b)).astype(BF16)
    dec = jnp.exp(b_last)
    att = [[] for _ in heads]
    for i in range(c // HG_SUB):
        lo, hi = i * HG_SUB, (i + 1) * HG_SUB
        k_i = (kh[:hi] * jnp.exp(r[lo:lo + 1, :] - b[:hi])).astype(BF16)
        if hi < c:
            k_i = jnp.concatenate([k_i, jnp.zeros((c - hi, k_i.shape[1]), BF16)], axis=0)
        for h, hs in enumerate(heads):
            att[h].append(lax.dot_general(q_sub[lo:hi, hs], k_i[:, hs], (((1,), (1,)), ((), ())),
                                          preferred_element_type=F32))
    att = [jnp.where(tril, jnp.concatenate(a, axis=0), 0.0).astype(BF16) for a in att]
    outs, new_sts = [], []
    for h, hs in enumerate(heads):
        o = jnp.dot(att[h], vb[:, hs], preferred_element_type=F32)
        o = o + lax.dot_general(q_dec[:, hs], sts[h].astype(BF16), (((1,), (1,)), ((), ())),
                                preferred_element_type=F32)
        outs.append(o)
    for h, hs in enumerate(heads):
        new_sts.append(sts[h] * dec[:, hs] + lax.dot_general(vb[:, hs], k_end[:, hs], (((0,), (0,)), ((), ())),
                                                             preferred_element_type=F32))
    return outs, new_sts


def _hgrn_kernel(lf_ref, qh_ref, kh_ref, v_ref, zh_ref, g_ref, s0_ref, o_ref, s_ref, st_scr, *, chunk, n_chunks):
    tt = pl.program_id(2)
    nh = st_scr.shape[0]
    heads = [slice(h * HEAD_DIM, (h + 1) * HEAD_DIM) for h in range(nh)]

    @pl.when(tt == 0)
    def _():
        for h in range(nh):
            st_scr[h] = s0_ref[0, h].T

    pfx = _prefix_matrix(chunk)
    ti = lax.broadcasted_iota(jnp.int32, (chunk, chunk), 0)
    si = lax.broadcasted_iota(jnp.int32, (chunk, chunk), 1)
    tril = si <= ti

    def step(ci, _):
        r0 = pl.multiple_of(ci * chunk, chunk)
        rs = pl.ds(r0, chunk)
        outs, sts = _hgrn_chunk(lf_ref[rs, :], qh_ref[rs, :], kh_ref[rs, :], v_ref[rs, :],
                                [st_scr[h] for h in range(nh)], pfx, tril)
        for h in range(nh):
            st_scr[h] = sts[h]
            o = outs[h]
            ms = jnp.mean(o * o, axis=-1, keepdims=True)
            o_ref[rs, heads[h]] = (o * lax.rsqrt(ms + EPS) * g_ref[h] * _silu(zh_ref[rs, heads[h]])).astype(BF16)
        return 0

    lax.fori_loop(0, n_chunks, step, 0)

    @pl.when(tt == pl.num_programs(2) - 1)
    def _():
        for h in range(nh):
            s_ref[0, h] = st_scr[h].T


def _hgrn_call(lf, qh, kh, v, zh, gain, s0, *, chunk, tile, heads):
    nb = s0.shape[0]
    t = lf.shape[0] // nb
    tile = min(tile, t)
    assert t % tile == 0 and tile % chunk == 0 and chunk % HG_SUB == 0 and N_HEADS % heads == 0
    nt = t // tile
    tok = pl.BlockSpec((tile, heads * HEAD_DIM), lambda b, h, i: (b * nt + i, h))
    state = pl.BlockSpec((1, heads, HEAD_DIM, HEAD_DIM), lambda b, h, i: (b, h, 0, 0))
    return pl.pallas_call(
        functools.partial(_hgrn_kernel, chunk=chunk, n_chunks=tile // chunk),
        grid=(nb, N_HEADS // heads, nt),
        in_specs=[tok, tok, tok, tok, tok, pl.BlockSpec((heads, 1, HEAD_DIM), lambda b, h, i: (h, 0, 0)), state],
        out_specs=[tok, state],
        out_shape=[jax.ShapeDtypeStruct(lf.shape, BF16), jax.ShapeDtypeStruct(s0.shape, F32)],
        scratch_shapes=[pltpu.VMEM((heads, HEAD_DIM, HEAD_DIM), F32)],
        compiler_params=_cparams("arbitrary", "arbitrary", "arbitrary"),
        name="hgrn2",
    )(lf, qh, kh, v, zh, gain, s0)


def _out_kernel(gs_ref, gh_ref, gsb_ref, ghg_ref, x_ref, gate_ref, wsb_ref, whg_ref, wo_ref, y_ref):
    nb, tr, d = x_ref.shape
    y_sb = jnp.dot(gs_ref[...], wsb_ref[...], preferred_element_type=F32)
    y_h = jnp.dot(gh_ref[...], whg_ref[...], preferred_element_type=F32)
    merged = jax.nn.sigmoid(gsb_ref[...]) * y_sb + jax.nn.sigmoid(ghg_ref[...]) * y_h
    upd = jnp.dot(merged.astype(BF16), wo_ref[...], preferred_element_type=F32)
    y_ref[...] = x_ref[...] + gate_ref[...] * upd.reshape(nb, tr, d)


def _out_call(gs, gh, gg, x, gate, wsb, whg, wo, nb, tr):
    n, t, d = x.shape
    tm = nb * tr
    nt = t // tr
    rowblk = lambda w, c: pl.BlockSpec((tm, w), lambda i, j: (i * nt + j if nb == 1 else i, c))
    const = lambda a: pl.BlockSpec(a.shape, lambda i, j: (0, 0), pipeline_mode=pl.Buffered(1))
    return pl.pallas_call(
        _out_kernel,
        grid=(n // nb, nt),
        in_specs=[rowblk(WIDTH, 0), rowblk(WIDTH, 0), rowblk(d, 0), rowblk(d, 1),
                  pl.BlockSpec((nb, tr, d), lambda i, j: (i, j, 0)),
                  pl.BlockSpec((nb, 1, d), lambda i, j: (i, 0, 0)),
                  const(wsb), const(whg), const(wo)],
        out_specs=pl.BlockSpec((nb, tr, d), lambda i, j: (i, j, 0)),
        out_shape=jax.ShapeDtypeStruct(x.shape, F32),
        compiler_params=_cparams("arbitrary", "arbitrary"),
        name="merge_out",
    )(gs, gh, gg, gg, x, gate, wsb, whg, wo)


def _stream(x, shift, scale, gate, p, *, nb, tr, hg_chunk, hg_tile, hg_heads, s0, cache=None):
    n, t, d = x.shape
    rows = n * t
    pn_tr = min(1024, t)
    pn_nb = max(1, min(n, 1024 // pn_tr))
    h = _prenorm_call(x, p["norm_gain"], scale, shift, pn_nb, pn_tr).reshape(rows, d)
    w_in = p["w_in"]
    tm = min(1024, rows)
    proj = functools.partial(_proj_call, h, w_in, tm=tm)
    (q_sb,) = proj(0 * WIDTH, WIDTH, "norm_q", (p["q_gain"],))
    k_sb, k_bf = proj(1 * WIDTH, WIDTH, "norm_k", (p["k_gain"],))
    v_sb, v_bf = proj(2 * WIDTH, WIDTH, "copy2")
    (z_sb,) = proj(3 * WIDTH, WIDTH, "plain")
    logf, k_h = proj(4 * WIDTH, WIDTH, "forget", (p["lb_raw"],))
    (i_h,) = proj(5 * WIDTH, WIDTH, "plain_bf16")
    (q_h,) = proj(6 * WIDTH, WIDTH, "silu")
    (z_h,) = proj(7 * WIDTH, WIDTH, "plain")
    (gg,) = proj(8 * WIDTH, 2 * d, "plain")

    if cache is None:
        assert n == 1
        gs = _sb_prompt_call(q_sb, k_bf, v_bf, z_sb)
    else:
        gs = _sb_sample_call(q_sb, k_bf, v_bf, z_sb, cache[0], cache[1])
    gh, s_new = _hgrn_call(logf, q_h, k_h, i_h, z_h, p["onorm_gain"], s0, chunk=hg_chunk, tile=hg_tile,
                           heads=hg_heads)
    y = _out_call(gs, gh, gg, x, gate, p["w_br_sb"], p["w_br_hg"], p["w_out"], nb, tr)
    k_new = k_sb.reshape(1, n, t, N_HEADS, HEAD_DIM)
    v_new = v_sb.reshape(1, n, t, N_HEADS, HEAD_DIM)
    return y, k_new, v_new, s_new[None]


def kernel(x_prompt, x_sample, cache_sb_k, cache_sb_v, state_hgrn, c_prompt, c_sample, norm_gain, w_ada, b_ada, w_in, q_norm_gain, k_norm_gain, hgrn_lb_raw, hgrn_onorm_gain, w_branch_sb, w_branch_hgrn, w_out):
    assert w_in.shape[0] == 1, "single-layer trunk"
    n_p, t_p, d = x_prompt.shape
    n_s, t_s, _ = x_sample.shape
    past = cache_sb_k.shape[2]

    c_all = jnp.concatenate([c_prompt, c_sample], axis=0)
    pad = (-c_all.shape[0]) % 8
    c_all = jnp.pad(c_all, ((0, pad), (0, 0)))
    mod = _ada_call(c_all, w_ada[0], b_ada[0].reshape(1, 3 * d))
    shift, scale, gate = (mod[:, i * d:(i + 1) * d] for i in range(3))
    vec = lambda a, lo, hi: a[lo:hi].reshape(hi - lo, 1, d)

    p = {
        "norm_gain": norm_gain[0].reshape(1, 1, d),
        "w_in": w_in[0],
        "q_gain": q_norm_gain[0].reshape(1, HEAD_DIM),
        "k_gain": k_norm_gain[0].reshape(1, HEAD_DIM),
        "lb_raw": hgrn_lb_raw,
        "onorm_gain": hgrn_onorm_gain[0].reshape(N_HEADS, 1, HEAD_DIM),
        "w_br_sb": w_branch_sb[0].astype(BF16),
        "w_br_hg": w_branch_hgrn[0].astype(BF16),
        "w_out": w_out[0].astype(BF16),
    }

    y_p, k_p, v_p, s_p = _stream(
        x_prompt, vec(shift, 0, n_p), vec(scale, 0, n_p), vec(gate, 0, n_p), p,
        nb=1, tr=min(256, t_p), hg_chunk=min(128, t_p), hg_tile=512, hg_heads=8,
        s0=jnp.zeros((n_p, N_HEADS, HEAD_DIM, HEAD_DIM), F32))
    nb_s = max(1, min(n_s, 256 // t_s))
    y_s, k_s, v_s, s_s = _stream(
        x_sample, vec(shift, n_p, n_p + n_s), vec(scale, n_p, n_p + n_s), vec(gate, n_p, n_p + n_s), p,
        nb=nb_s, tr=t_s, hg_chunk=t_s, hg_tile=t_s, hg_heads=N_HEADS, s0=state_hgrn[0],
        cache=(cache_sb_k.reshape(n_s, past * N_HEADS, HEAD_DIM), cache_sb_v.reshape(n_s, past * N_HEADS, HEAD_DIM)))
    return (y_p, y_s, k_p, v_p, s_p, k_s, v_s, s_s)
```

```python
import functools

import jax
import jax.numpy as jnp
from jax import lax
from jax.experimental import pallas as pl
from jax.experimental.pallas import tpu as pltpu

F32 = jnp.float32
BF16 = jnp.bfloat16

N_HEADS = 8
HEAD_DIM = 128
WIDTH = N_HEADS * HEAD_DIM
HG_SUB = 16
EPS = 1e-6
SB_BLOCK = 128
SB_LOG_CUTOFF = -88.0
VMEM_LIMIT = 56 * 1024 * 1024


def _cparams(*sem):
    return pltpu.CompilerParams(dimension_semantics=sem, vmem_limit_bytes=VMEM_LIMIT)


def _silu(x):
    return x * jax.nn.sigmoid(x)


def _ada_kernel(c_ref, w_ref, b_ref, o_ref):
    c = c_ref[...]
    a = _silu(c).astype(BF16)
    o_ref[...] = jnp.dot(a, w_ref[...].astype(BF16), preferred_element_type=F32) + b_ref[...]


def _ada_call(c, w, b):
    r, d = c.shape
    n = w.shape[1]
    tn = 1024
    return pl.pallas_call(
        _ada_kernel,
        grid=(n // tn,),
        in_specs=[pl.BlockSpec((r, d), lambda j: (0, 0)),
                  pl.BlockSpec((d, tn), lambda j: (0, j)),
                  pl.BlockSpec((1, tn), lambda j: (0, j))],
        out_specs=pl.BlockSpec((r, tn), lambda j: (0, j)),
        out_shape=jax.ShapeDtypeStruct((r, n), F32),
        compiler_params=_cparams("arbitrary"),
        name="ada_mod",
    )(c, w, b)


def _prenorm_kernel(x_ref, g_ref, sc_ref, sh_ref, h_ref):
    x = x_ref[...]
    ms = jnp.mean(x * x, axis=-1, keepdims=True)
    xn = x * lax.rsqrt(ms + EPS)
    h = xn * g_ref[...] * (1.0 + sc_ref[...]) + sh_ref[...]
    h_ref[...] = h.astype(BF16)


def _prenorm_call(x, gain, scale, shift, nb, tr):
    n, t, d = x.shape
    vec = pl.BlockSpec((nb, 1, d), lambda i, j: (i, 0, 0))
    return pl.pallas_call(
        _prenorm_kernel,
        grid=(n // nb, t // tr),
        in_specs=[pl.BlockSpec((nb, tr, d), lambda i, j: (i, j, 0)),
                  pl.BlockSpec((1, 1, d), lambda i, j: (0, 0, 0)), vec, vec],
        out_specs=pl.BlockSpec((nb, tr, d), lambda i, j: (i, j, 0)),
        out_shape=jax.ShapeDtypeStruct((n, t, d), BF16),
        compiler_params=_cparams("arbitrary", "arbitrary"),
        name="prenorm",
    )(x, gain, scale, shift)


def _head_rms(y, gain):
    outs = []
    for g in range(N_HEADS):
        yh = y[:, g * HEAD_DIM:(g + 1) * HEAD_DIM]
        ms = jnp.mean(yh * yh, axis=-1, keepdims=True)
        outs.append(yh * lax.rsqrt(ms + EPS) * gain)
    return outs


def _proj_kernel(*refs, kind):
    h_ref, w_ref = refs[0], refs[1]
    wb_ref = refs[-1]

    @pl.when(pl.program_id(1) == 0)
    def _():
        wb_ref[...] = w_ref[...].astype(BF16)

    y = jnp.dot(h_ref[...], wb_ref[...], preferred_element_type=F32)
    tm = y.shape[0]
    if kind == "plain":
        refs[2][...] = y
    elif kind == "plain_bf16":
        refs[2][...] = y.astype(BF16)
    elif kind == "silu":
        refs[2][...] = _silu(y)
    elif kind == "copy2":
        for g in range(N_HEADS):
            refs[2][pl.ds(g, tm, stride=N_HEADS), :] = y[:, g * HEAD_DIM:(g + 1) * HEAD_DIM]
        refs[3][...] = y.astype(BF16)
    elif kind == "norm_q":
        gain = refs[2][...]
        for g, o in enumerate(_head_rms(y, gain)):
            refs[3][:, g * HEAD_DIM:(g + 1) * HEAD_DIM] = o.astype(BF16)
    elif kind == "norm_k":
        gain = refs[2][...]
        for g, o in enumerate(_head_rms(y, gain)):
            refs[3][pl.ds(g, tm, stride=N_HEADS), :] = o
            refs[4][:, g * HEAD_DIM:(g + 1) * HEAD_DIM] = o.astype(BF16)
    elif kind == "forget":
        raw = refs[2][...]
        e = jnp.exp(raw - jnp.max(raw, axis=0, keepdims=True))
        lb = e[0:1, :] / jnp.sum(e, axis=0, keepdims=True)
        f = lb + (1.0 - lb) * jax.nn.sigmoid(y)
        refs[3][...] = jnp.log(f)
        refs[4][...] = 1.0 - f
    else:
        raise ValueError(kind)


def _proj_call(h, w_in, col0, ncols, kind, extra=(), *, tm=512, tn=1024):
    rows, d = h.shape
    assert col0 % tn == 0 and ncols % tn == 0 and rows % tm == 0
    jb = col0 // tn
    grid = (ncols // tn, rows // tm)
    tile = lambda: pl.BlockSpec((tm, tn), lambda j, i: (i, j))
    in_specs = [pl.BlockSpec((tm, d), lambda j, i: (i, 0)),
                pl.BlockSpec((d, tn), lambda j, i: (0, jb + j))]
    for e in extra:
        in_specs.append(pl.BlockSpec(e.shape, lambda j, i: (0, 0)))
    if kind in ("plain", "silu"):
        out_dt = (F32,)
    elif kind in ("norm_q", "plain_bf16"):
        out_dt = (BF16,)
    elif kind in ("copy2", "norm_k"):
        out_dt = (F32, BF16)
    else:
        out_dt = (F32, F32)
    out_specs = [tile() for _ in out_dt]
    out_shape = [jax.ShapeDtypeStruct((rows, ncols), dt) for dt in out_dt]
    if kind in ("copy2", "norm_k"):
        assert ncols == WIDTH
        out_specs[0] = pl.BlockSpec((tm * N_HEADS, HEAD_DIM), lambda j, i: (i, 0))
        out_shape[0] = jax.ShapeDtypeStruct((rows * N_HEADS, HEAD_DIM), F32)
    outs = pl.pallas_call(
        functools.partial(_proj_kernel, kind=kind),
        grid=grid,
        in_specs=in_specs,
        out_specs=out_specs,
        out_shape=out_shape,
        scratch_shapes=[pltpu.VMEM((d, tn), BF16)],
        compiler_params=_cparams("arbitrary", "arbitrary"),
        name="proj_" + kind,
    )(h, w_in, *extra)
    return outs


def _suffix_matrix(bk):
    j = lax.broadcasted_iota(jnp.int32, (bk, 2 * bk), 0)
    s = lax.broadcasted_iota(jnp.int32, (bk, 2 * bk), 1)
    return jnp.where((j > s) | (s >= bk), -1.0, 0.0).astype(BF16)


def _sb_tiles(qs, ks, vs, carries, sfx, mask):
    bk = ks[0].shape[0]
    n = range(len(qs))
    masks = mask if isinstance(mask, (list, tuple)) else [mask] * len(qs)
    zs = [lax.dot_general(qs[i], ks[i], (((1,), (1,)), ((), ())), preferred_element_type=F32) * HEAD_DIM ** -0.5
          for i in n]
    sps = [jnp.maximum(z, 0.0) + jnp.log(1.0 + jnp.exp(-jnp.abs(z))) for z in zs]
    l1m = [sp if m is None else jnp.where(m, sp, 0.0) for sp, m in zip(sps, masks)]
    his = [x.astype(BF16) for x in l1m]
    los = [(x - hi.astype(F32)).astype(BF16) for x, hi in zip(l1m, his)]
    r2s = [jnp.dot(hi, sfx, preferred_element_type=F32) + jnp.dot(lo, sfx, preferred_element_type=F32)
           for hi, lo in zip(his, los)]
    carries = list(carries)
    new = []
    for i in n:
        if isinstance(carries[i], int):
            carries[i] = new[carries[i]]
        new.append(carries[i] + r2s[i][:, bk:])
    ws = [jnp.exp(zs[i] - sps[i] + r2s[i][:, :bk] + carries[i]) for i in n]
    ws = [w if m is None else jnp.where(m, w, 0.0) for w, m in zip(ws, masks)]
    pvs = [jnp.dot(ws[i].astype(BF16), vs[i], preferred_element_type=F32) for i in n]
    return new, pvs


def _sb_prompt_kernel(q_ref, k_ref, v_ref, z_ref, o_ref, c_scr, acc_scr, *, n_groups, group, ahead):
    blk = SB_BLOCK
    qt = pl.program_id(1)
    sfx = _suffix_matrix(blk)
    row = lax.broadcasted_iota(jnp.int32, (blk, blk), 0)
    col = lax.broadcasted_iota(jnp.int32, (blk, blk), 1)
    causal = col < row
    alive = lambda cs: (functools.reduce(jnp.maximum, [jnp.max(c) for c in cs]) >= SB_LOG_CUTOFF).astype(jnp.int32)

    def kv(kb):
        start = pl.multiple_of(kb * blk, blk)
        return k_ref[pl.ds(start, blk), :], v_ref[pl.ds(start, blk), :]

    def qgroup(ig, _):
        gq0 = (qt * n_groups + ig) * group
        rows = [pl.ds(pl.multiple_of((ig * group + g) * blk, blk), blk) for g in range(group)]
        qs = [q_ref[r, :] for r in rows]
        kbs = [gq0 + g - s for s in range(1 + ahead) for g in range(group)]
        kvs = [kv(jnp.maximum(kb, 0)) for kb in kbs]
        zero = jnp.zeros((blk, blk), F32)
        carries = [zero] * group + list(range(ahead * group))
        masks = [causal] * group + [None] * (ahead * group)
        cs, pvs = _sb_tiles(qs * (1 + ahead), [k for k, _ in kvs], [v for _, v in kvs], carries, sfx, masks)
        for g in range(group):
            acc = pvs[g]
            for s in range(1, 1 + ahead):
                acc = acc + jnp.where(kbs[s * group + g] >= 0, pvs[s * group + g], 0.0)
            c_scr[g] = cs[ahead * group + g]
            acc_scr[g] = acc
        cs = cs[ahead * group:]

        def cond(st):
            s, go = st
            return jnp.logical_and(s <= gq0 + group - 1, go > 0)

        def body(st):
            s, _ = st
            kbs = [gq0 + g - s for g in range(group)]
            kvs = [kv(jnp.maximum(kb, 0)) for kb in kbs]
            cs, pvs = _sb_tiles(qs, [k for k, _ in kvs], [v for _, v in kvs],
                                [c_scr[g] for g in range(group)], sfx, None)
            for g in range(group):
                c_scr[g] = cs[g]
                acc_scr[g] += jnp.where(kbs[g] >= 0, pvs[g], 0.0)
            return s + 1, alive(cs)

        lax.while_loop(cond, body, (1 + ahead, alive(cs)))
        for g in range(group):
            o_ref[rows[g], :] = (acc_scr[g] * _silu(z_ref[rows[g], :])).astype(BF16)
        return 0

    lax.fori_loop(0, n_groups, qgroup, 0)


def _sb_prompt_call(q, k, v, z, *, tq=2048, group=8, ahead=2):
    t = q.shape[0]
    tq = min(tq, t)
    assert t % tq == 0 and tq % (SB_BLOCK * group) == 0
    qspec = pl.BlockSpec((tq, HEAD_DIM), lambda h, i: (i, h))
    kvspec = pl.BlockSpec((t, HEAD_DIM), lambda h, i: (0, h))
    return pl.pallas_call(
        functools.partial(_sb_prompt_kernel, n_groups=tq // (SB_BLOCK * group), group=group, ahead=ahead),
        grid=(N_HEADS, t // tq),
        in_specs=[qspec, kvspec, kvspec, qspec],
        out_specs=qspec,
        out_shape=jax.ShapeDtypeStruct((t, WIDTH), BF16),
        scratch_shapes=[pltpu.VMEM((group, SB_BLOCK, SB_BLOCK), F32), pltpu.VMEM((group, SB_BLOCK, HEAD_DIM), F32)],
        compiler_params=_cparams("arbitrary", "arbitrary"),
        name="sb_prompt",
    )(q, k, v, z)


def _sb_sample_kernel(q_ref, kn_ref, vn_ref, z_ref, kc_hbm, vc_hbm, o_ref, kbuf, vbuf, sem, c_scr, acc_scr, *, past):
    blk = SB_BLOCK
    nh = N_HEADS
    b = pl.program_id(0)
    tq = q_ref.shape[0]
    half = blk - tq
    n_full = (past - half) // blk
    rem = (past - half) % blk
    sfx = _suffix_matrix(blk)
    row = lax.broadcasted_iota(jnp.int32, (tq, blk), 0)
    col = lax.broadcasted_iota(jnp.int32, (tq, blk), 1)
    heads = [slice(h * HEAD_DIM, (h + 1) * HEAD_DIM) for h in range(nh)]
    alive = lambda cs: (functools.reduce(jnp.maximum, [jnp.max(c) for c in cs]) >= SB_LOG_CUTOFF).astype(jnp.int32)

    def copies(key0, nkeys, slot):
        src = pl.ds(key0 * nh, nkeys * nh)
        dst = pl.ds(0, nkeys * nh)
        return (pltpu.make_async_copy(kc_hbm.at[b, src, :], kbuf.at[slot, dst, :], sem.at[0, slot]),
                pltpu.make_async_copy(vc_hbm.at[b, src, :], vbuf.at[slot, dst, :], sem.at[1, slot]))

    def start(cps):
        for cp in cps:
            cp.start()

    def wait(cps):
        for cp in cps:
            cp.wait()

    def tile_copies(j):
        return copies(past - half - (j + 1) * blk, blk, (j + 1) % 2)

    def cached(buf, slot, h, nkeys):
        return buf[slot, pl.ds(h, nkeys, stride=nh), :].astype(BF16)

    start(copies(past - half, half, 0))
    if n_full:
        start(tile_copies(0))
    wait(copies(past - half, half, 0))

    qs = [q_ref[:, heads[h]] for h in range(nh)]
    k0 = [jnp.concatenate([cached(kbuf, 0, h, half), kn_ref[:, heads[h]]], axis=0) for h in range(nh)]
    v0 = [jnp.concatenate([cached(vbuf, 0, h, half), vn_ref[:, heads[h]]], axis=0) for h in range(nh)]
    cs, pvs = _sb_tiles(qs, k0, v0, [jnp.zeros((tq, blk), F32)] * nh, sfx, col < row + half)
    for h in range(nh):
        c_scr[h] = cs[h]
        acc_scr[h] = pvs[h]

    def sweep(slot, mask):
        cs, pvs = _sb_tiles(qs, [cached(kbuf, slot, h, blk) for h in range(nh)],
                            [cached(vbuf, slot, h, blk) for h in range(nh)],
                            [c_scr[h] for h in range(nh)], sfx, mask)
        for h in range(nh):
            c_scr[h] = cs[h]
            acc_scr[h] += pvs[h]
        return cs

    def cond(st):
        j, go = st
        return jnp.logical_and(j < n_full, go > 0)

    def body(st):
        j, _ = st
        slot = (j + 1) % 2
        wait(tile_copies(j))

        @pl.when(j + 1 < n_full)
        def _():
            start(tile_copies(j + 1))

        return j + 1, alive(sweep(slot, None))

    done, go = lax.while_loop(cond, body, (0, alive(cs)))

    @pl.when(done < n_full)
    def _():
        wait(tile_copies(done))

    if rem:
        @pl.when(go > 0)
        def _():
            cps = copies(0, blk, 0)
            start(cps)
            wait(cps)
            sweep(0, col < rem)

    for h in range(nh):
        o_ref[:, heads[h]] = (acc_scr[h] * _silu(z_ref[:, heads[h]])).astype(BF16)


def _sb_sample_call(q, kn, vn, z, kc, vc):
    nb = kc.shape[0]
    past = kc.shape[1] // N_HEADS
    tq = q.shape[0] // nb
    assert tq % 16 == 0 and tq < SB_BLOCK and past >= SB_BLOCK
    new = pl.BlockSpec((tq, WIDTH), lambda b: (b, 0))
    hbm = pl.BlockSpec(memory_space=pl.ANY)
    return pl.pallas_call(
        functools.partial(_sb_sample_kernel, past=past),
        grid=(nb,),
        in_specs=[new, new, new, new, hbm, hbm],
        out_specs=new,
        out_shape=jax.ShapeDtypeStruct(q.shape, BF16),
        scratch_shapes=[pltpu.VMEM((2, SB_BLOCK * N_HEADS, HEAD_DIM), F32),
                        pltpu.VMEM((2, SB_BLOCK * N_HEADS, HEAD_DIM), F32),
                        pltpu.SemaphoreType.DMA((2, 2)),
                        pltpu.VMEM((N_HEADS, tq, SB_BLOCK), F32), pltpu.VMEM((N_HEADS, tq, HEAD_DIM), F32)],
        compiler_params=_cparams("arbitrary"),
        name="sb_sample",
    )(q, kn, vn, z, kc, vc)


def _prefix_matrix(c):
    t = lax.broadcasted_iota(jnp.int32, (2 * c, c), 0)
    s = lax.broadcasted_iota(jnp.int32, (2 * c, c), 1)
    incl = (t < c) & (s <= t)
    sub = (t >= c) & (s < ((t - c) // HG_SUB) * HG_SUB)
    return jnp.where(incl | sub, 1.0, 0.0).astype(BF16)


def _hgrn_chunk(lf, qh, kh, v, sts, pfx, tril):
    c = lf.shape[0]
    n_sub = c // HG_SUB
    heads = [slice(h * HEAD_DIM, (h + 1) * HEAD_DIM) for h in range(len(sts))]
    p0 = lf.astype(BF16)
    r1 = lf - p0.astype(F32)
    p1 = r1.astype(BF16)
    p2 = (r1 - p1.astype(F32)).astype(BF16)
    br = (jnp.dot(pfx, p0, preferred_element_type=F32) + jnp.dot(pfx, p1, preferred_element_type=F32)
          + jnp.dot(pfx, p2, preferred_element_type=F32))
    b = br[:c]
    r = br[c:]
    b_last = b[c - 1:c, :]
    vb = v
    q_sub = (qh * jnp.exp(b - r)).astype(BF16)
    q_dec = (qh * jnp.exp(b)).astype(BF16)
    k_end = (kh * jnp.exp(b_last - b)).astype(BF16)
    dec = jnp.exp(b_last)
    att = [[] for _ in heads]
    for i in range(n_sub):
        lo, hi = i * HG_SUB, (i + 1) * HG_SUB
        k_i = (kh[:hi] * jnp.exp(r[lo:lo + 1, :] - b[:hi])).astype(BF16)
        if hi < c:
            k_i = jnp.concatenate([k_i, jnp.zeros((c - hi, k_i.shape[1]), BF16)], axis=0)
        for h, hs in enumerate(heads):
            att[h].append(lax.dot_general(q_sub[lo:hi, hs], k_i[:, hs], (((1,), (1,)), ((), ())),
                                          preferred_element_type=F32))
    att = [jnp.where(tril, jnp.concatenate(a, axis=0), 0.0).astype(BF16) for a in att]
    outs, new_sts = [], []
    for h, hs in enumerate(heads):
        o = jnp.dot(att[h], vb[:, hs], preferred_element_type=F32)
        o = o + lax.dot_general(q_dec[:, hs], sts[h].astype(BF16), (((1,), (1,)), ((), ())),
                                preferred_element_type=F32)
        outs.append(o)
    for h, hs in enumerate(heads):
        new_sts.append(sts[h] * dec[:, hs] + lax.dot_general(vb[:, hs], k_end[:, hs], (((0,), (0,)), ((), ())),
                                                             preferred_element_type=F32))
    return outs, new_sts


def _hgrn_kernel(lf_ref, qh_ref, kh_ref, v_ref, zh_ref, g_ref, s0_ref, o_ref, s_ref, st_scr, *, chunk, n_chunks):
    tt = pl.program_id(2)
    nh = st_scr.shape[0]
    heads = [slice(h * HEAD_DIM, (h + 1) * HEAD_DIM) for h in range(nh)]

    @pl.when(tt == 0)
    def _():
        for h in range(nh):
            st_scr[h] = s0_ref[0, h].T

    pfx = _prefix_matrix(chunk)
    ti = lax.broadcasted_iota(jnp.int32, (chunk, chunk), 0)
    si = lax.broadcasted_iota(jnp.int32, (chunk, chunk), 1)
    tril = si <= ti

    def step(ci, _):
        r0 = pl.multiple_of(ci * chunk, chunk)
        rs = pl.ds(r0, chunk)
        outs, sts = _hgrn_chunk(lf_ref[rs, :], qh_ref[rs, :], kh_ref[rs, :], v_ref[rs, :],
                                [st_scr[h] for h in range(nh)], pfx, tril)
        for h in range(nh):
            st_scr[h] = sts[h]
            o = outs[h]
            ms = jnp.mean(o * o, axis=-1, keepdims=True)
            o_ref[rs, heads[h]] = (o * lax.rsqrt(ms + EPS) * g_ref[h] * _silu(zh_ref[rs, heads[h]])).astype(BF16)
        return 0

    lax.fori_loop(0, n_chunks, step, 0)

    @pl.when(tt == pl.num_programs(2) - 1)
    def _():
        for h in range(nh):
            s_ref[0, h] = st_scr[h].T


def _hgrn_call(lf, qh, kh, v, zh, gain, s0, *, chunk, tile, heads):
    nb = s0.shape[0]
    t = lf.shape[0] // nb
    tile = min(tile, t)
    assert t % tile == 0 and tile % chunk == 0 and chunk % HG_SUB == 0 and N_HEADS % heads == 0
    nt = t // tile
    tok = pl.BlockSpec((tile, heads * HEAD_DIM), lambda b, h, i: (b * nt + i, h))
    state = pl.BlockSpec((1, heads, HEAD_DIM, HEAD_DIM), lambda b, h, i: (b, h, 0, 0))
    return pl.pallas_call(
        functools.partial(_hgrn_kernel, chunk=chunk, n_chunks=tile // chunk),
        grid=(nb, N_HEADS // heads, nt),
        in_specs=[tok, tok, tok, tok, tok, pl.BlockSpec((heads, 1, HEAD_DIM), lambda b, h, i: (h, 0, 0)), state],
        out_specs=[tok, state],
        out_shape=[jax.ShapeDtypeStruct(lf.shape, BF16), jax.ShapeDtypeStruct(s0.shape, F32)],
        scratch_shapes=[pltpu.VMEM((heads, HEAD_DIM, HEAD_DIM), F32)],
        compiler_params=_cparams("arbitrary", "arbitrary", "arbitrary"),
        name="hgrn2",
    )(lf, qh, kh, v, zh, gain, s0)


def _out_kernel(gs_ref, gh_ref, gsb_ref, ghg_ref, x_ref, gate_ref, wsb_ref, whg_ref, wo_ref, y_ref):
    nb, tr, d = x_ref.shape
    y_sb = jnp.dot(gs_ref[...], wsb_ref[...], preferred_element_type=F32)
    y_h = jnp.dot(gh_ref[...], whg_ref[...], preferred_element_type=F32)
    merged = jax.nn.sigmoid(gsb_ref[...]) * y_sb + jax.nn.sigmoid(ghg_ref[...]) * y_h
    upd = jnp.dot(merged.astype(BF16), wo_ref[...], preferred_element_type=F32)
    y_ref[...] = x_ref[...] + gate_ref[...] * upd.reshape(nb, tr, d)


def _out_call(gs, gh, gg, x, gate, wsb, whg, wo, nb, tr):
    n, t, d = x.shape
    tm = nb * tr
    nt = t // tr
    rowblk = lambda w, c: pl.BlockSpec((tm, w), lambda i, j: (i * nt + j if nb == 1 else i, c))
    const = lambda a: pl.BlockSpec(a.shape, lambda i, j: (0, 0), pipeline_mode=pl.Buffered(1))
    return pl.pallas_call(
        _out_kernel,
        grid=(n // nb, nt),
        in_specs=[rowblk(WIDTH, 0), rowblk(WIDTH, 0), rowblk(d, 0), rowblk(d, 1),
                  pl.BlockSpec((nb, tr, d), lambda i, j: (i, j, 0)),
                  pl.BlockSpec((nb, 1, d), lambda i, j: (i, 0, 0)),
                  const(wsb), const(whg), const(wo)],
        out_specs=pl.BlockSpec((nb, tr, d), lambda i, j: (i, j, 0)),
        out_shape=jax.ShapeDtypeStruct(x.shape, F32),
        compiler_params=_cparams("arbitrary", "arbitrary"),
        name="merge_out",
    )(gs, gh, gg, gg, x, gate, wsb, whg, wo)


def _stream(x, shift, scale, gate, p, *, nb, tr, hg_chunk, hg_tile, hg_heads, s0, cache=None):
    n, t, d = x.shape
    rows = n * t
    pn_tr = min(1024, t)
    pn_nb = max(1, min(n, 1024 // pn_tr))
    h = _prenorm_call(x, p["norm_gain"], scale, shift, pn_nb, pn_tr).reshape(rows, d)
    w_in = p["w_in"]
    tm = min(1024, rows)
    proj = functools.partial(_proj_call, h, w_in, tm=tm)
    (q_sb,) = proj(0 * WIDTH, WIDTH, "norm_q", (p["q_gain"],))
    k_sb, k_bf = proj(1 * WIDTH, WIDTH, "norm_k", (p["k_gain"],))
    v_sb, v_bf = proj(2 * WIDTH, WIDTH, "copy2")
    (z_sb,) = proj(3 * WIDTH, WIDTH, "plain")
    logf, k_h = proj(4 * WIDTH, WIDTH, "forget", (p["lb_raw"],))
    (i_h,) = proj(5 * WIDTH, WIDTH, "plain_bf16")
    (q_h,) = proj(6 * WIDTH, WIDTH, "silu")
    (z_h,) = proj(7 * WIDTH, WIDTH, "plain")
    (gg,) = proj(8 * WIDTH, 2 * d, "plain")

    if cache is None:
        assert n == 1
        gs = _sb_prompt_call(q_sb, k_bf, v_bf, z_sb)
    else:
        gs = _sb_sample_call(q_sb, k_bf, v_bf, z_sb, cache[0], cache[1])
    gh, s_new = _hgrn_call(logf, q_h, k_h, i_h, z_h, p["onorm_gain"], s0, chunk=hg_chunk, tile=hg_tile,
                           heads=hg_heads)
    y = _out_call(gs, gh, gg, x, gate, p["w_br_sb"], p["w_br_hg"], p["w_out"], nb, tr)
    k_new = k_sb.reshape(1, n, t, N_HEADS, HEAD_DIM)
    v_new = v_sb.reshape(1, n, t, N_HEADS, HEAD_DIM)
    return y, k_new, v_new, s_new[None]


def kernel(x_prompt, x_sample, cache_sb_k, cache_sb_v, state_hgrn, c_prompt, c_sample, norm_gain, w_ada, b_ada, w_in, q_norm_gain, k_norm_gain, hgrn_lb_raw, hgrn_onorm_gain, w_branch_sb, w_branch_hgrn, w_out):
    assert w_in.shape[0] == 1, "single-layer trunk"
    n_p, t_p, d = x_prompt.shape
    n_s, t_s, _ = x_sample.shape
    past = cache_sb_k.shape[2]

    c_all = jnp.concatenate([c_prompt, c_sample], axis=0)
    pad = (-c_all.shape[0]) % 8
    c_all = jnp.pad(c_all, ((0, pad), (0, 0)))
    mod = _ada_call(c_all, w_ada[0], b_ada[0].reshape(1, 3 * d))
    shift, scale, gate = (mod[:, i * d:(i + 1) * d] for i in range(3))
    vec = lambda a, lo, hi: a[lo:hi].reshape(hi - lo, 1, d)

    p = {
        "norm_gain": norm_gain[0].reshape(1, 1, d),
        "w_in": w_in[0],
        "q_gain": q_norm_gain[0].reshape(1, HEAD_DIM),
        "k_gain": k_norm_gain[0].reshape(1, HEAD_DIM),
        "lb_raw": hgrn_lb_raw,
        "onorm_gain": hgrn_onorm_gain[0].reshape(N_HEADS, 1, HEAD_DIM),
        "w_br_sb": w_branch_sb[0].astype(BF16),
        "w_br_hg": w_branch_hgrn[0].astype(BF16),
        "w_out": w_out[0].astype(BF16),
    }

    y_p, k_p, v_p, s_p = _stream(
        x_prompt, vec(shift, 0, n_p), vec(scale, 0, n_p), vec(gate, 0, n_p), p,
        nb=1, tr=min(256, t_p), hg_chunk=min(128, t_p), hg_tile=512, hg_heads=8,
        s0=jnp.zeros((n_p, N_HEADS, HEAD_DIM, HEAD_DIM), F32))
    nb_s = max(1, min(n_s, 256 // t_s))
    y_s, k_s, v_s, s_s = _stream(
        x_sample, vec(shift, n_p, n_p + n_s), vec(scale, n_p, n_p + n_s), vec(gate, n_p, n_p + n_s), p,
        nb=nb_s, tr=t_s, hg_chunk=t_s, hg_tile=t_s, hg_heads=N_HEADS, s0=state_hgrn[0],
        cache=(cache_sb_k.reshape(n_s, past * N_HEADS, HEAD_DIM), cache_sb_v.reshape(n_s, past * N_HEADS, HEAD_DIM)))
    return (y_p, y_s, k_p, v_p, s_p, k_s, v_s, s_s)
```

```python
import functools

import jax
import jax.numpy as jnp
from jax import lax
from jax.experimental import pallas as pl
from jax.experimental.pallas import tpu as pltpu

F32 = jnp.float32
BF16 = jnp.bfloat16

N_HEADS = 8
HEAD_DIM = 128
WIDTH = N_HEADS * HEAD_DIM
HG_SUB = 16
EPS = 1e-6
SB_BLOCK = 128
SB_LOG_CUTOFF = -88.0
VMEM_LIMIT = 56 * 1024 * 1024


def _cparams(*sem):
    return pltpu.CompilerParams(dimension_semantics=sem, vmem_limit_bytes=VMEM_LIMIT)


def _silu(x):
    return x * jax.nn.sigmoid(x)


def _ada_kernel(c_ref, w_ref, b_ref, o_ref):
    c = c_ref[...]
    a = _silu(c).astype(BF16)
    o_ref[...] = jnp.dot(a, w_ref[...].astype(BF16), preferred_element_type=F32) + b_ref[...]


def _ada_call(c, w, b):
    r, d = c.shape
    n = w.shape[1]
    tn = 1024
    return pl.pallas_call(
        _ada_kernel,
        grid=(n // tn,),
        in_specs=[pl.BlockSpec((r, d), lambda j: (0, 0)),
                  pl.BlockSpec((d, tn), lambda j: (0, j)),
                  pl.BlockSpec((1, tn), lambda j: (0, j))],
        out_specs=pl.BlockSpec((r, tn), lambda j: (0, j)),
        out_shape=jax.ShapeDtypeStruct((r, n), F32),
        compiler_params=_cparams("arbitrary"),
        name="ada_mod",
    )(c, w, b)


def _prenorm_kernel(x_ref, g_ref, sc_ref, sh_ref, h_ref):
    x = x_ref[...]
    ms = jnp.mean(x * x, axis=-1, keepdims=True)
    xn = x * lax.rsqrt(ms + EPS)
    h = xn * g_ref[...] * (1.0 + sc_ref[...]) + sh_ref[...]
    h_ref[...] = h.astype(BF16)


def _prenorm_call(x, gain, scale, shift, nb, tr):
    n, t, d = x.shape
    vec = pl.BlockSpec((nb, 1, d), lambda i, j: (i, 0, 0))
    return pl.pallas_call(
        _prenorm_kernel,
        grid=(n // nb, t // tr),
        in_specs=[pl.BlockSpec((nb, tr, d), lambda i, j: (i, j, 0)),
                  pl.BlockSpec((1, 1, d), lambda i, j: (0, 0, 0)), vec, vec],
        out_specs=pl.BlockSpec((nb, tr, d), lambda i, j: (i, j, 0)),
        out_shape=jax.ShapeDtypeStruct((n, t, d), BF16),
        compiler_params=_cparams("arbitrary", "arbitrary"),
        name="prenorm",
    )(x, gain, scale, shift)


def _head_rms(y, gain):
    outs = []
    for g in range(N_HEADS):
        yh = y[:, g * HEAD_DIM:(g + 1) * HEAD_DIM]
        ms = jnp.mean(yh * yh, axis=-1, keepdims=True)
        outs.append(yh * lax.rsqrt(ms + EPS) * gain)
    return outs


def _proj_kernel(*refs, kind):
    h_ref, w_ref = refs[0], refs[1]
    wb_ref = refs[-1]

    @pl.when(pl.program_id(1) == 0)
    def _():
        wb_ref[...] = w_ref[...].astype(BF16)

    y = jnp.dot(h_ref[...], wb_ref[...], preferred_element_type=F32)
    tm = y.shape[0]
    if kind == "plain":
        refs[2][...] = y
    elif kind == "plain_bf16":
        refs[2][...] = y.astype(BF16)
    elif kind == "silu":
        refs[2][...] = _silu(y)
    elif kind == "copy2":
        for g in range(N_HEADS):
            refs[2][pl.ds(g, tm, stride=N_HEADS), :] = y[:, g * HEAD_DIM:(g + 1) * HEAD_DIM]
        refs[3][...] = y.astype(BF16)
    elif kind == "norm_q":
        gain = refs[2][...]
        for g, o in enumerate(_head_rms(y, gain)):
            refs[3][:, g * HEAD_DIM:(g + 1) * HEAD_DIM] = o.astype(BF16)
    elif kind == "norm_k":
        gain = refs[2][...]
        for g, o in enumerate(_head_rms(y, gain)):
            refs[3][pl.ds(g, tm, stride=N_HEADS), :] = o
            refs[4][:, g * HEAD_DIM:(g + 1) * HEAD_DIM] = o.astype(BF16)
    elif kind == "forget":
        raw = refs[2][...]
        e = jnp.exp(raw - jnp.max(raw, axis=0, keepdims=True))
        lb = e[0:1, :] / jnp.sum(e, axis=0, keepdims=True)
        f = lb + (1.0 - lb) * jax.nn.sigmoid(y)
        refs[3][...] = jnp.log(f)
        refs[4][...] = 1.0 - f
    else:
        raise ValueError(kind)


def _proj_call(h, w_in, col0, ncols, kind, extra=(), *, tm=512, tn=1024):
    rows, d = h.shape
    assert col0 % tn == 0 and ncols % tn == 0 and rows % tm == 0
    jb = col0 // tn
    grid = (ncols // tn, rows // tm)
    tile = lambda: pl.BlockSpec((tm, tn), lambda j, i: (i, j))
    in_specs = [pl.BlockSpec((tm, d), lambda j, i: (i, 0)),
                pl.BlockSpec((d, tn), lambda j, i: (0, jb + j))]
    for e in extra:
        in_specs.append(pl.BlockSpec(e.shape, lambda j, i: (0, 0)))
    if kind in ("plain", "silu"):
        out_dt = (F32,)
    elif kind in ("norm_q", "plain_bf16"):
        out_dt = (BF16,)
    elif kind in ("copy2", "norm_k"):
        out_dt = (F32, BF16)
    else:
        out_dt = (F32, F32)
    out_specs = [tile() for _ in out_dt]
    out_shape = [jax.ShapeDtypeStruct((rows, ncols), dt) for dt in out_dt]
    if kind in ("copy2", "norm_k"):
        assert ncols == WIDTH
        out_specs[0] = pl.BlockSpec((tm * N_HEADS, HEAD_DIM), lambda j, i: (i, 0))
        out_shape[0] = jax.ShapeDtypeStruct((rows * N_HEADS, HEAD_DIM), F32)
    outs = pl.pallas_call(
        functools.partial(_proj_kernel, kind=kind),
        grid=grid,
        in_specs=in_specs,
        out_specs=out_specs,
        out_shape=out_shape,
        scratch_shapes=[pltpu.VMEM((d, tn), BF16)],
        compiler_params=_cparams("arbitrary", "arbitrary"),
        name="proj_" + kind,
    )(h, w_in, *extra)
    return outs


def _prenorm_q_kernel(x_ref, g_ref, sc_ref, sh_ref, w_ref, qg_ref, h_ref, q_ref, wb_ref, h2_ref):
    s = pl.program_id(0)

    @pl.when(s == 0)
    def _():
        wb_ref[...] = w_ref[...].astype(BF16)
        h2_ref[1] = jnp.zeros(h2_ref.shape[1:], BF16)

    y = jnp.dot(h2_ref[(s + 1) % 2], wb_ref[...], preferred_element_type=F32)
    for g, o in enumerate(_head_rms(y, qg_ref[...])):
        q_ref[:, g * HEAD_DIM:(g + 1) * HEAD_DIM] = o.astype(BF16)

    x = x_ref[0]
    ms = jnp.mean(x * x, axis=-1, keepdims=True)
    hn = (x * lax.rsqrt(ms + EPS) * g_ref[0] * (1.0 + sc_ref[0]) + sh_ref[0]).astype(BF16)
    h2_ref[s % 2] = hn
    h_ref[...] = hn


def _prenorm_q_call(x, gain, scale, shift, w_in, q_gain, *, tm):
    n, t, d = x.shape
    assert n == 1 and t % tm == 0
    nrow = t // tm
    this = lambda s: jnp.minimum(s, nrow - 1)
    prev = lambda s: jnp.maximum(s - 1, 0)
    vec = pl.BlockSpec((1, 1, d), lambda s: (0, 0, 0))
    return pl.pallas_call(
        _prenorm_q_kernel,
        grid=(nrow + 1,),
        in_specs=[pl.BlockSpec((1, tm, d), lambda s: (0, this(s), 0)), vec, vec, vec,
                  pl.BlockSpec((d, WIDTH), lambda s: (0, 0), pipeline_mode=pl.Buffered(1)),
                  pl.BlockSpec(q_gain.shape, lambda s: (0, 0))],
        out_specs=[pl.BlockSpec((tm, d), lambda s: (this(s), 0)),
                   pl.BlockSpec((tm, WIDTH), lambda s: (prev(s), 0))],
        out_shape=[jax.ShapeDtypeStruct((t, d), BF16), jax.ShapeDtypeStruct((t, WIDTH), BF16)],
        scratch_shapes=[pltpu.VMEM((d, WIDTH), BF16), pltpu.VMEM((2, tm, d), BF16)],
        compiler_params=_cparams("arbitrary"),
        name="prenorm_q",
    )(x, gain, scale, shift, w_in, q_gain)


def _suffix_matrix(bk):
    j = lax.broadcasted_iota(jnp.int32, (2 * bk, 2 * bk), 0) % bk
    s = lax.broadcasted_iota(jnp.int32, (2 * bk, 2 * bk), 1)
    return jnp.where((j > s) | (s >= bk), -1.0, 0.0).astype(BF16)


def _sb_tiles(qs, ks, vs, carries, sfx, masks=None, valid=None):
    bk = sfx.shape[0] // 2
    n = range(len(qs))
    spans = [range(ks[i].shape[0] // bk) for i in n]
    lanes = lambda x, t: x[:, t * bk:(t + 1) * bk]
    mask_of = lambda i, t: None if masks is None or masks[i] is None else masks[i][t]
    zs = [lax.dot_general(qs[i], ks[i], (((1,), (1,)), ((), ())), preferred_element_type=F32) * HEAD_DIM ** -0.5
          for i in n]
    sps = [jnp.maximum(z, 0.0) + jnp.log(1.0 + jnp.exp(-jnp.abs(z))) for z in zs]
    r2s = []
    for i in n:
        r2 = []
        for t in spans[i]:
            m = mask_of(i, t)
            l1m = lanes(sps[i], t) if m is None else jnp.where(m, lanes(sps[i], t), 0.0)
            hi = l1m.astype(BF16)
            lo = (l1m - hi.astype(F32)).astype(BF16)
            r2.append(jnp.dot(jnp.concatenate([hi, lo], axis=1), sfx, preferred_element_type=F32))
        r2s.append(r2)
    new, wss = [], []
    for i in n:
        c = carries[i]
        ws = []
        for t in spans[i]:
            w = jnp.exp(lanes(zs[i], t) - lanes(sps[i], t) + r2s[i][t][:, :bk] + c)
            m = mask_of(i, t)
            if m is not None:
                w = jnp.where(m, w, 0.0)
            if valid is not None and valid[i][t] is not None:
                w = jnp.where(valid[i][t], w, 0.0)
            ws.append(w.astype(BF16))
            c = c + r2s[i][t][:, bk:]
        new.append(c)
        wss.append(ws[0] if len(ws) == 1 else jnp.concatenate(ws, axis=1))
    pvs = [jnp.dot(wss[i], vs[i], preferred_element_type=F32) for i in n]
    return new, pvs


def _sb_prompt_kernel(q_ref, k_ref, v_ref, z_ref, o_ref, c_scr, acc_scr, *, n_groups, group, ahead):
    blk = SB_BLOCK
    qt = pl.program_id(1)
    sfx = _suffix_matrix(blk)
    row = lax.broadcasted_iota(jnp.int32, (blk, blk), 0)
    col = lax.broadcasted_iota(jnp.int32, (blk, blk), 1)
    causal = col < row
    alive = lambda cs: (functools.reduce(jnp.maximum, [jnp.max(c) for c in cs]) >= SB_LOG_CUTOFF).astype(jnp.int32)

    def kv(kb):
        start = pl.multiple_of(kb * blk, blk)
        return k_ref[pl.ds(start, blk), :], v_ref[pl.ds(start, blk), :]

    def qgroup(ig, _):
        gq0 = (qt * n_groups + ig) * group
        rows = [pl.ds(pl.multiple_of((ig * group + g) * blk, blk), blk) for g in range(group)]
        qs = [q_ref[r, :] for r in rows]
        kbs = [[gq0 + g - s for s in range(1 + ahead)] for g in range(group)]
        kvs = [[kv(jnp.maximum(kb, 0)) for kb in kbs[g]] for g in range(group)]
        cat = lambda xs: jnp.concatenate(xs, axis=0)
        cs, pvs = _sb_tiles(qs, [cat([k for k, _ in kvs[g]]) for g in range(group)],
                            [cat([v for _, v in kvs[g]]) for g in range(group)],
                            [jnp.zeros((blk, blk), F32)] * group, sfx,
                            masks=[[causal] + [None] * ahead] * group,
                            valid=[[None] + [kb >= 0 for kb in kbs[g][1:]] for g in range(group)])
        for g in range(group):
            c_scr[g] = cs[g]
            acc_scr[g] = pvs[g]

        def cond(st):
            s, go = st
            return jnp.logical_and(s <= gq0 + group - 1, go > 0)

        def body(st):
            s, _ = st
            kbs = [gq0 + g - s for g in range(group)]
            kvs = [kv(jnp.maximum(kb, 0)) for kb in kbs]
            cs, pvs = _sb_tiles(qs, [k for k, _ in kvs], [v for _, v in kvs],
                                [c_scr[g] for g in range(group)], sfx, valid=[[kb >= 0] for kb in kbs])
            for g in range(group):
                c_scr[g] = cs[g]
                acc_scr[g] += pvs[g]
            return s + 1, alive(cs)

        lax.while_loop(cond, body, (1 + ahead, alive(cs)))
        for g in range(group):
            o_ref[rows[g], :] = (acc_scr[g] * _silu(z_ref[rows[g], :])).astype(BF16)
        return 0

    lax.fori_loop(0, n_groups, qgroup, 0)


def _sb_prompt_call(q, k, v, z, *, tq=2048, group=8, ahead=2):
    t = q.shape[0]
    tq = min(tq, t)
    assert t % tq == 0 and tq % (SB_BLOCK * group) == 0
    qspec = pl.BlockSpec((tq, HEAD_DIM), lambda h, i: (i, h))
    kvspec = pl.BlockSpec((t, HEAD_DIM), lambda h, i: (0, h))
    return pl.pallas_call(
        functools.partial(_sb_prompt_kernel, n_groups=tq // (SB_BLOCK * group), group=group, ahead=ahead),
        grid=(N_HEADS, t // tq),
        in_specs=[qspec, kvspec, kvspec, qspec],
        out_specs=qspec,
        out_shape=jax.ShapeDtypeStruct((t, WIDTH), BF16),
        scratch_shapes=[pltpu.VMEM((group, SB_BLOCK, SB_BLOCK), F32), pltpu.VMEM((group, SB_BLOCK, HEAD_DIM), F32)],
        compiler_params=_cparams("arbitrary", "arbitrary"),
        name="sb_prompt",
    )(q, k, v, z)


def _sb_sample_kernel(q_ref, kn_ref, vn_ref, z_ref, kc_hbm, vc_hbm, o_ref, kbuf, vbuf, sem, c_scr, acc_scr, *, past):
    blk = SB_BLOCK
    nh = N_HEADS
    b = pl.program_id(0)
    tq = q_ref.shape[0]
    half = blk - tq
    n_full = (past - half) // blk
    rem = (past - half) % blk
    sfx = _suffix_matrix(blk)
    row = lax.broadcasted_iota(jnp.int32, (tq, blk), 0)
    col = lax.broadcasted_iota(jnp.int32, (tq, blk), 1)
    heads = [slice(h * HEAD_DIM, (h + 1) * HEAD_DIM) for h in range(nh)]
    alive = lambda cs: (functools.reduce(jnp.maximum, [jnp.max(c) for c in cs]) >= SB_LOG_CUTOFF).astype(jnp.int32)

    def copies(key0, nkeys, slot):
        src = pl.ds(key0 * nh, nkeys * nh)
        dst = pl.ds(0, nkeys * nh)
        return (pltpu.make_async_copy(kc_hbm.at[b, src, :], kbuf.at[slot, dst, :], sem.at[0, slot]),
                pltpu.make_async_copy(vc_hbm.at[b, src, :], vbuf.at[slot, dst, :], sem.at[1, slot]))

    def start(cps):
        for cp in cps:
            cp.start()

    def wait(cps):
        for cp in cps:
            cp.wait()

    def tile_copies(j):
        return copies(past - half - (j + 1) * blk, blk, (j + 1) % 2)

    def cached(buf, slot, h, nkeys):
        return buf[slot, pl.ds(h, nkeys, stride=nh), :].astype(BF16)

    start(copies(past - half, half, 0))
    if n_full:
        start(tile_copies(0))
    wait(copies(past - half, half, 0))

    qs = [q_ref[:, heads[h]] for h in range(nh)]
    k0 = [jnp.concatenate([cached(kbuf, 0, h, half), kn_ref[:, heads[h]]], axis=0) for h in range(nh)]
    v0 = [jnp.concatenate([cached(vbuf, 0, h, half), vn_ref[:, heads[h]]], axis=0) for h in range(nh)]
    cs, pvs = _sb_tiles(qs, k0, v0, [jnp.zeros((tq, blk), F32)] * nh, sfx, masks=[[col < row + half]] * nh)
    for h in range(nh):
        c_scr[h] = cs[h]
        acc_scr[h] = pvs[h]

    def sweep(slot, mask):
        cs, pvs = _sb_tiles(qs, [cached(kbuf, slot, h, blk) for h in range(nh)],
                            [cached(vbuf, slot, h, blk) for h in range(nh)],
                            [c_scr[h] for h in range(nh)], sfx, masks=None if mask is None else [[mask]] * nh)
        for h in range(nh):
            c_scr[h] = cs[h]
            acc_scr[h] += pvs[h]
        return cs

    def cond(st):
        j, go = st
        return jnp.logical_and(j < n_full, go > 0)

    def body(st):
        j, _ = st
        slot = (j + 1) % 2
        wait(tile_copies(j))

        @pl.when(j + 1 < n_full)
        def _():
            start(tile_copies(j + 1))

        return j + 1, alive(sweep(slot, None))

    done, go = lax.while_loop(cond, body, (0, alive(cs)))

    @pl.when(done < n_full)
    def _():
        wait(tile_copies(done))

    if rem:
        @pl.when(go > 0)
        def _():
            cps = copies(0, blk, 0)
            start(cps)
            wait(cps)
            sweep(0, col < rem)

    for h in range(nh):
        o_ref[:, heads[h]] = (acc_scr[h] * _silu(z_ref[:, heads[h]])).astype(BF16)


def _sb_sample_call(q, kn, vn, z, kc, vc):
    nb = kc.shape[0]
    past = kc.shape[1] // N_HEADS
    tq = q.shape[0] // nb
    assert tq % 16 == 0 and tq < SB_BLOCK and past >= SB_BLOCK
    new = pl.BlockSpec((tq, WIDTH), lambda b: (b, 0))
    hbm = pl.BlockSpec(memory_space=pl.ANY)
    return pl.pallas_call(
        functools.partial(_sb_sample_kernel, past=past),
        grid=(nb,),
        in_specs=[new, new, new, new, hbm, hbm],
        out_specs=new,
        out_shape=jax.ShapeDtypeStruct(q.shape, BF16),
        scratch_shapes=[pltpu.VMEM((2, SB_BLOCK * N_HEADS, HEAD_DIM), F32),
                        pltpu.VMEM((2, SB_BLOCK * N_HEADS, HEAD_DIM), F32),
                        pltpu.SemaphoreType.DMA((2, 2)),
                        pltpu.VMEM((N_HEADS, tq, SB_BLOCK), F32), pltpu.VMEM((N_HEADS, tq, HEAD_DIM), F32)],
        compiler_params=_cparams("arbitrary"),
        name="sb_sample",
    )(q, kn, vn, z, kc, vc)


def _prefix_matrix(c):
    t = lax.broadcasted_iota(jnp.int32, (2 * c, c), 0)
    s = lax.broadcasted_iota(jnp.int32, (2 * c, c), 1)
    incl = (t < c) & (s <= t)
    sub = (t >= c) & (s < ((t - c) // HG_SUB) * HG_SUB)
    return jnp.where(incl | sub, 1.0, 0.0).astype(BF16)


def _hgrn_chunk(lf, qh, kh, v, sts, pfx, tril):
    c = lf.shape[0]
    n_sub = c // HG_SUB
    heads = [slice(h * HEAD_DIM, (h + 1) * HEAD_DIM) for h in range(len(sts))]
    p0 = lf.astype(BF16)
    r1 = lf - p0.astype(F32)
    p1 = r1.astype(BF16)
    p2 = (r1 - p1.astype(F32)).astype(BF16)
    br = (jnp.dot(pfx, p0, preferred_element_type=F32) + jnp.dot(pfx, p1, preferred_element_type=F32)
          + jnp.dot(pfx, p2, preferred_element_type=F32))
    b = br[:c]
    r = br[c:]
    b_last = b[c - 1:c, :]
    vb = v
    q_sub = (qh * jnp.exp(b - r)).astype(BF16)
    q_dec = (qh * jnp.exp(b)).astype(BF16)
    k_end = (kh * jnp.exp(b_last - b)).astype(BF16)
    dec = jnp.exp(b_last)
    att = [[] for _ in heads]
    for i in range(n_sub):
        lo, hi = i * HG_SUB, (i + 1) * HG_SUB
        k_i = (kh[:hi] * jnp.exp(r[lo:lo + 1, :] - b[:hi])).astype(BF16)
        if hi < c:
            k_i = jnp.concatenate([k_i, jnp.zeros((c - hi, k_i.shape[1]), BF16)], axis=0)
        for h, hs in enumerate(heads):
            att[h].append(lax.dot_general(q_sub[lo:hi, hs], k_i[:, hs], (((1,), (1,)), ((), ())),
                                          preferred_element_type=F32))
    att = [jnp.where(tril, jnp.concatenate(a, axis=0), 0.0).astype(BF16) for a in att]
    outs, new_sts = [], []
    for h, hs in enumerate(heads):
        o = jnp.dot(att[h], vb[:, hs], preferred_element_type=F32)
        o = o + lax.dot_general(q_dec[:, hs], sts[h].astype(BF16), (((1,), (1,)), ((), ())),
                                preferred_element_type=F32)
        outs.append(o)
    for h, hs in enumerate(heads):
        new_sts.append(sts[h] * dec[:, hs] + lax.dot_general(vb[:, hs], k_end[:, hs], (((0,), (0,)), ((), ())),
                                                             preferred_element_type=F32))
    return outs, new_sts


def _hgrn_kernel(lf_ref, qh_ref, kh_ref, v_ref, zh_ref, g_ref, s0_ref, o_ref, s_ref, st_scr, *, chunk, n_chunks):
    tt = pl.program_id(2)
    nh = st_scr.shape[0]
    heads = [slice(h * HEAD_DIM, (h + 1) * HEAD_DIM) for h in range(nh)]

    @pl.when(tt == 0)
    def _():
        for h in range(nh):
            st_scr[h] = s0_ref[0, h].T

    pfx = _prefix_matrix(chunk)
    ti = lax.broadcasted_iota(jnp.int32, (chunk, chunk), 0)
    si = lax.broadcasted_iota(jnp.int32, (chunk, chunk), 1)
    tril = si <= ti

    def step(ci, _):
        r0 = pl.multiple_of(ci * chunk, chunk)
        rs = pl.ds(r0, chunk)
        outs, sts = _hgrn_chunk(lf_ref[rs, :], qh_ref[rs, :], kh_ref[rs, :], v_ref[rs, :],
                                [st_scr[h] for h in range(nh)], pfx, tril)
        for h in range(nh):
            st_scr[h] = sts[h]
            o = outs[h]
            ms = jnp.mean(o * o, axis=-1, keepdims=True)
            o_ref[rs, heads[h]] = (o * lax.rsqrt(ms + EPS) * g_ref[h] * _silu(zh_ref[rs, heads[h]])).astype(BF16)
        return 0

    lax.fori_loop(0, n_chunks, step, 0)

    @pl.when(tt == pl.num_programs(2) - 1)
    def _():
        for h in range(nh):
            s_ref[0, h] = st_scr[h].T


def _hgrn_call(lf, qh, kh, v, zh, gain, s0, *, chunk, tile, heads):
    nb = s0.shape[0]
    t = lf.shape[0] // nb
    tile = min(tile, t)
    assert t % tile == 0 and tile % chunk == 0 and chunk % HG_SUB == 0 and N_HEADS % heads == 0
    nt = t // tile
    tok = pl.BlockSpec((tile, heads * HEAD_DIM), lambda b, h, i: (b * nt + i, h))
    state = pl.BlockSpec((1, heads, HEAD_DIM, HEAD_DIM), lambda b, h, i: (b, h, 0, 0))
    return pl.pallas_call(
        functools.partial(_hgrn_kernel, chunk=chunk, n_chunks=tile // chunk),
        grid=(nb, N_HEADS // heads, nt),
        in_specs=[tok, tok, tok, tok, tok, pl.BlockSpec((heads, 1, HEAD_DIM), lambda b, h, i: (h, 0, 0)), state],
        out_specs=[tok, state],
        out_shape=[jax.ShapeDtypeStruct(lf.shape, BF16), jax.ShapeDtypeStruct(s0.shape, F32)],
        scratch_shapes=[pltpu.VMEM((heads, HEAD_DIM, HEAD_DIM), F32)],
        compiler_params=_cparams("arbitrary", "arbitrary", "arbitrary"),
        name="hgrn2",
    )(lf, qh, kh, v, zh, gain, s0)


def _out_kernel(gs_ref, gh_ref, gsb_ref, ghg_ref, x_ref, gate_ref, wsb_ref, whg_ref, wo_ref, y_ref):
    nb, tr, d = x_ref.shape
    y_sb = jnp.dot(gs_ref[...], wsb_ref[...], preferred_element_type=F32)
    y_h = jnp.dot(gh_ref[...], whg_ref[...], preferred_element_type=F32)
    merged = jax.nn.sigmoid(gsb_ref[...]) * y_sb + jax.nn.sigmoid(ghg_ref[...]) * y_h
    upd = jnp.dot(merged.astype(BF16), wo_ref[...], preferred_element_type=F32)
    y_ref[...] = x_ref[...] + gate_ref[...] * upd.reshape(nb, tr, d)


def _out_call(gs, gh, gg, x, gate, wsb, whg, wo, nb, tr):
    n, t, d = x.shape
    tm = nb * tr
    nt = t // tr
    rowblk = lambda w, c: pl.BlockSpec((tm, w), lambda i, j: (i * nt + j if nb == 1 else i, c))
    const = lambda a: pl.BlockSpec(a.shape, lambda i, j: (0, 0), pipeline_mode=pl.Buffered(1))
    return pl.pallas_call(
        _out_kernel,
        grid=(n // nb, nt),
        in_specs=[rowblk(WIDTH, 0), rowblk(WIDTH, 0), rowblk(d, 0), rowblk(d, 1),
                  pl.BlockSpec((nb, tr, d), lambda i, j: (i, j, 0)),
                  pl.BlockSpec((nb, 1, d), lambda i, j: (i, 0, 0)),
                  const(wsb), const(whg), const(wo)],
        out_specs=pl.BlockSpec((nb, tr, d), lambda i, j: (i, j, 0)),
        out_shape=jax.ShapeDtypeStruct(x.shape, F32),
        compiler_params=_cparams("arbitrary", "arbitrary"),
        name="merge_out",
    )(gs, gh, gg, gg, x, gate, wsb, whg, wo)


def _stream(x, shift, scale, gate, p, *, nb, tr, hg_chunk, hg_tile, hg_heads, s0, cache=None):
    n, t, d = x.shape
    rows = n * t
    w_in = p["w_in"]
    tm = min(1024, rows)
    if n == 1:
        h, q_sb = _prenorm_q_call(x, p["norm_gain"], scale, shift, w_in, p["q_gain"], tm=tm)
        proj = functools.partial(_proj_call, h, w_in, tm=tm)
    else:
        pn_tr = min(1024, t)
        pn_nb = max(1, min(n, 1024 // pn_tr))
        h = _prenorm_call(x, p["norm_gain"], scale, shift, pn_nb, pn_tr).reshape(rows, d)
        proj = functools.partial(_proj_call, h, w_in, tm=tm)
        (q_sb,) = proj(0 * WIDTH, WIDTH, "norm_q", (p["q_gain"],))
    k_sb, k_bf = proj(1 * WIDTH, WIDTH, "norm_k", (p["k_gain"],))
    v_sb, v_bf = proj(2 * WIDTH, WIDTH, "copy2")
    (z_sb,) = proj(3 * WIDTH, WIDTH, "plain")
    logf, k_h = proj(4 * WIDTH, WIDTH, "forget", (p["lb_raw"],))
    (i_h,) = proj(5 * WIDTH, WIDTH, "plain_bf16")
    (q_h,) = proj(6 * WIDTH, WIDTH, "silu")
    (z_h,) = proj(7 * WIDTH, WIDTH, "plain")
    (gg,) = proj(8 * WIDTH, 2 * d, "plain")

    if cache is None:
        assert n == 1
        gs = _sb_prompt_call(q_sb, k_bf, v_bf, z_sb)
    else:
        gs = _sb_sample_call(q_sb, k_bf, v_bf, z_sb, cache[0], cache[1])
    gh, s_new = _hgrn_call(logf, q_h, k_h, i_h, z_h, p["onorm_gain"], s0, chunk=hg_chunk, tile=hg_tile,
                           heads=hg_heads)
    y = _out_call(gs, gh, gg, x, gate, p["w_br_sb"], p["w_br_hg"], p["w_out"], nb, tr)
    k_new = k_sb.reshape(1, n, t, N_HEADS, HEAD_DIM)
    v_new = v_sb.reshape(1, n, t, N_HEADS, HEAD_DIM)
    return y, k_new, v_new, s_new[None]


def kernel(x_prompt, x_sample, cache_sb_k, cache_sb_v, state_hgrn, c_prompt, c_sample, norm_gain, w_ada, b_ada, w_in, q_norm_gain, k_norm_gain, hgrn_lb_raw, hgrn_onorm_gain, w_branch_sb, w_branch_hgrn, w_out):
    assert w_in.shape[0] == 1, "single-layer trunk"
    n_p, t_p, d = x_prompt.shape
    n_s, t_s, _ = x_sample.shape
    past = cache_sb_k.shape[2]

    c_all = jnp.concatenate([c_prompt, c_sample], axis=0)
    pad = (-c_all.shape[0]) % 8
    c_all = jnp.pad(c_all, ((0, pad), (0, 0)))
    mod = _ada_call(c_all, w_ada[0], b_ada[0].reshape(1, 3 * d))
    shift, scale, gate = (mod[:, i * d:(i + 1) * d] for i in range(3))
    vec = lambda a, lo, hi: a[lo:hi].reshape(hi - lo, 1, d)

    p = {
        "norm_gain": norm_gain[0].reshape(1, 1, d),
        "w_in": w_in[0],
        "q_gain": q_norm_gain[0].reshape(1, HEAD_DIM),
        "k_gain": k_norm_gain[0].reshape(1, HEAD_DIM),
        "lb_raw": hgrn_lb_raw,
        "onorm_gain": hgrn_onorm_gain[0].reshape(N_HEADS, 1, HEAD_DIM),
        "w_br_sb": w_branch_sb[0].astype(BF16),
        "w_br_hg": w_branch_hgrn[0].astype(BF16),
        "w_out": w_out[0].astype(BF16),
    }

    y_p, k_p, v_p, s_p = _stream(
        x_prompt, vec(shift, 0, n_p), vec(scale, 0, n_p), vec(gate, 0, n_p), p,
        nb=1, tr=min(256, t_p), hg_chunk=min(128, t_p), hg_tile=512, hg_heads=8,
        s0=jnp.zeros((n_p, N_HEADS, HEAD_DIM, HEAD_DIM), F32))
    nb_s = max(1, min(n_s, 256 // t_s))
    y_s, k_s, v_s, s_s = _stream(
        x_sample, vec(shift, n_p, n_p + n_s), vec(scale, n_p, n_p + n_s), vec(gate, n_p, n_p + n_s), p,
        nb=nb_s, tr=t_s, hg_chunk=t_s, hg_tile=t_s, hg_heads=N_HEADS, s0=state_hgrn[0],
        cache=(cache_sb_k.reshape(n_s, past * N_HEADS, HEAD_DIM), cache_sb_v.reshape(n_s, past * N_HEADS, HEAD_DIM)))
    return (y_p, y_s, k_p, v_p, s_p, k_s, v_s, s_s)
```

```python
import functools
import math

import jax
import jax.numpy as jnp
from jax import lax
from jax.experimental import pallas as pl
from jax.experimental.pallas import tpu as pltpu

F32 = jnp.float32
BF16 = jnp.bfloat16

N_HEADS = 8
HEAD_DIM = 128
WIDTH = N_HEADS * HEAD_DIM
HG_SUB = 16
EPS = 1e-6
SB_BLOCK = 128
SB_LOG_CUTOFF = -88.0
VMEM_LIMIT = 56 * 1024 * 1024


def _cparams(*sem):
    return pltpu.CompilerParams(dimension_semantics=sem, vmem_limit_bytes=VMEM_LIMIT)


def _silu(x):
    return x * jax.nn.sigmoid(x)


def _ada_kernel(c_ref, w_ref, b_ref, o_ref):
    c = c_ref[...]
    a = _silu(c).astype(BF16)
    o_ref[...] = jnp.dot(a, w_ref[...].astype(BF16), preferred_element_type=F32) + b_ref[...]


def _ada_call(c, w, b):
    r, d = c.shape
    n = w.shape[1]
    tn = 1024
    return pl.pallas_call(
        _ada_kernel,
        grid=(n // tn,),
        in_specs=[pl.BlockSpec((r, d), lambda j: (0, 0)),
                  pl.BlockSpec((d, tn), lambda j: (0, j)),
                  pl.BlockSpec((1, tn), lambda j: (0, j))],
        out_specs=pl.BlockSpec((r, tn), lambda j: (0, j)),
        out_shape=jax.ShapeDtypeStruct((r, n), F32),
        compiler_params=_cparams("arbitrary"),
        name="ada_mod",
    )(c, w, b)


def _prenorm_kernel(x_ref, g_ref, sc_ref, sh_ref, hall_ref, h_ref):
    del hall_ref
    x = x_ref[...]
    ms = jnp.mean(x * x, axis=-1, keepdims=True)
    xn = x * lax.rsqrt(ms + EPS)
    h = xn * g_ref[...] * (1.0 + sc_ref[...]) + sh_ref[...]
    h_ref[...] = h.astype(BF16).reshape(h_ref.shape)


def _prenorm_call(x, gain, scale, shift, nb, tr, h_all, row0):
    n, t, d = x.shape
    tm = nb * tr
    assert row0 % tm == 0 and (nb == 1 or tr == t)
    vec = pl.BlockSpec((nb, 1, d), lambda i, j: (i, 0, 0))
    return pl.pallas_call(
        _prenorm_kernel,
        grid=(n // nb, t // tr),
        in_specs=[pl.BlockSpec((nb, tr, d), lambda i, j: (i, j, 0)),
                  pl.BlockSpec((1, 1, d), lambda i, j: (0, 0, 0)), vec, vec,
                  pl.BlockSpec(memory_space=pl.ANY)],
        out_specs=pl.BlockSpec((tm, d), lambda i, j: (row0 // tm + i * (t // tr) + j, 0)),
        out_shape=jax.ShapeDtypeStruct(h_all.shape, BF16),
        input_output_aliases={4: 0},
        compiler_params=_cparams("arbitrary", "arbitrary"),
        name="prenorm",
    )(x, gain, scale, shift, h_all)


def _head_rms(y, gain):
    outs = []
    for g in range(N_HEADS):
        yh = y[:, g * HEAD_DIM:(g + 1) * HEAD_DIM]
        ms = jnp.mean(yh * yh, axis=-1, keepdims=True)
        outs.append(yh * lax.rsqrt(ms + EPS) * gain)
    return outs


def _proj_kernel(*refs, kind):
    h_ref, w_ref = refs[0], refs[1]
    wb_ref = refs[-1]

    @pl.when(pl.program_id(1) == 0)
    def _():
        wb_ref[...] = w_ref[...].astype(BF16)

    y = jnp.dot(h_ref[...], wb_ref[...], preferred_element_type=F32)
    tm = y.shape[0]
    if kind == "plain":
        refs[2][...] = y
    elif kind == "plain_bf16":
        refs[2][...] = y.astype(BF16)
    elif kind == "silu":
        refs[2][...] = _silu(y)
    elif kind == "copy2":
        for g in range(N_HEADS):
            refs[2][pl.ds(g, tm, stride=N_HEADS), :] = y[:, g * HEAD_DIM:(g + 1) * HEAD_DIM]
        refs[3][...] = y.astype(BF16)
    elif kind == "norm_q":
        gain = refs[2][...]
        for g, o in enumerate(_head_rms(y, gain)):
            refs[3][:, g * HEAD_DIM:(g + 1) * HEAD_DIM] = o.astype(BF16)
    elif kind == "norm_k":
        gain = refs[2][...]
        for g, o in enumerate(_head_rms(y, gain)):
            refs[3][pl.ds(g, tm, stride=N_HEADS), :] = o
            refs[4][:, g * HEAD_DIM:(g + 1) * HEAD_DIM] = o.astype(BF16)
    elif kind == "forget":
        raw = refs[2][...]
        e = jnp.exp(raw - jnp.max(raw, axis=0, keepdims=True))
        lb = e[0:1, :] / jnp.sum(e, axis=0, keepdims=True)
        f = lb + (1.0 - lb) * jax.nn.sigmoid(y)
        refs[3][...] = jnp.log(f)
        refs[4][...] = 1.0 - f
    else:
        raise ValueError(kind)


def _proj_call(h, w_in, col0, ncols, kind, extra=(), *, tm=512, tn=1024, row0=0, rows=None):
    d = h.shape[1]
    rows = h.shape[0] - row0 if rows is None else rows
    assert col0 % tn == 0 and ncols % tn == 0 and rows % tm == 0 and row0 % tm == 0
    jb, ib = col0 // tn, row0 // tm
    grid = (ncols // tn, rows // tm)
    tile = lambda: pl.BlockSpec((tm, tn), lambda j, i: (i, j))
    in_specs = [pl.BlockSpec((tm, d), lambda j, i: (ib + i, 0)),
                pl.BlockSpec((d, tn), lambda j, i: (0, jb + j))]
    for e in extra:
        in_specs.append(pl.BlockSpec(e.shape, lambda j, i: (0, 0)))
    if kind in ("plain", "silu"):
        out_dt = (F32,)
    elif kind in ("norm_q", "plain_bf16"):
        out_dt = (BF16,)
    elif kind in ("copy2", "norm_k"):
        out_dt = (F32, BF16)
    else:
        out_dt = (F32, F32)
    out_specs = [tile() for _ in out_dt]
    out_shape = [jax.ShapeDtypeStruct((rows, ncols), dt) for dt in out_dt]
    if kind in ("copy2", "norm_k"):
        assert ncols == WIDTH
        out_specs[0] = pl.BlockSpec((tm * N_HEADS, HEAD_DIM), lambda j, i: (i, 0))
        out_shape[0] = jax.ShapeDtypeStruct((rows * N_HEADS, HEAD_DIM), F32)
    outs = pl.pallas_call(
        functools.partial(_proj_kernel, kind=kind),
        grid=grid,
        in_specs=in_specs,
        out_specs=out_specs,
        out_shape=out_shape,
        scratch_shapes=[pltpu.VMEM((d, tn), BF16)],
        compiler_params=_cparams("arbitrary", "arbitrary"),
        name="proj_" + kind,
    )(h, w_in, *extra)
    return outs


def _prenorm_q_kernel(x_ref, g_ref, sc_ref, sh_ref, w_ref, qg_ref, h_ref, q_ref, wb_ref, h2_ref, *, nrow, extra):
    s = pl.program_id(0)

    @pl.when(s == 0)
    def _():
        wb_ref[...] = w_ref[...].astype(BF16)
        h2_ref[1] = jnp.zeros(h2_ref.shape[1:], BF16)

    y = jnp.dot(h2_ref[(s + 1) % 2], wb_ref[...], preferred_element_type=F32)
    for g, o in enumerate(_head_rms(y, qg_ref[...])):
        q_ref[:, g * HEAD_DIM:(g + 1) * HEAD_DIM] = o.astype(BF16)

    x = x_ref[0]
    ms = jnp.mean(x * x, axis=-1, keepdims=True)
    hn = (x * lax.rsqrt(ms + EPS) * g_ref[0] * (1.0 + sc_ref[0]) + sh_ref[0]).astype(BF16)
    h2_ref[s % 2] = hn
    h_ref[...] = jnp.where(s < nrow, hn, jnp.zeros_like(hn)) if extra else hn


def _prenorm_q_call(x, gain, scale, shift, w_in, q_gain, *, tm, extra_rows=0):
    n, t, d = x.shape
    assert n == 1 and t % tm == 0 and extra_rows % tm == 0
    nrow, extra = t // tm, extra_rows // tm
    steps = nrow + max(1, extra)
    this = lambda s: jnp.minimum(s, nrow - 1)
    prev = lambda s: jnp.minimum(jnp.maximum(s - 1, 0), nrow - 1)
    vec = pl.BlockSpec((1, 1, d), lambda s: (0, 0, 0))
    return pl.pallas_call(
        functools.partial(_prenorm_q_kernel, nrow=nrow, extra=extra),
        grid=(steps,),
        in_specs=[pl.BlockSpec((1, tm, d), lambda s: (0, this(s), 0)), vec, vec, vec,
                  pl.BlockSpec((d, WIDTH), lambda s: (0, 0), pipeline_mode=pl.Buffered(1)),
                  pl.BlockSpec(q_gain.shape, lambda s: (0, 0))],
        out_specs=[pl.BlockSpec((tm, d), lambda s: (jnp.minimum(s, nrow - 1 + extra), 0)),
                   pl.BlockSpec((tm, WIDTH), lambda s: (prev(s), 0))],
        out_shape=[jax.ShapeDtypeStruct((t + extra_rows, d), BF16), jax.ShapeDtypeStruct((t, WIDTH), BF16)],
        scratch_shapes=[pltpu.VMEM((d, WIDTH), BF16), pltpu.VMEM((2, tm, d), BF16)],
        compiler_params=_cparams("arbitrary"),
        name="prenorm_q",
    )(x, gain, scale, shift, w_in, q_gain)


def _suffix_matrix(bk):
    j = lax.broadcasted_iota(jnp.int32, (2 * bk, 2 * bk), 0) % bk
    s = lax.broadcasted_iota(jnp.int32, (2 * bk, 2 * bk), 1)
    return jnp.where((j > s) | (s >= bk), -1.0, 0.0).astype(BF16)


def _sb_tiles(qs, ks, vs, carries, sfx, masks=None, valid=None):
    bk = sfx.shape[0] // 2
    n = range(len(qs))
    spans = [range(ks[i].shape[0] // bk) for i in n]
    lanes = lambda x, t: x[:, t * bk:(t + 1) * bk]
    mask_of = lambda i, t: None if masks is None or masks[i] is None else masks[i][t]
    zs = [lax.dot_general(qs[i], ks[i], (((1,), (1,)), ((), ())), preferred_element_type=F32) * HEAD_DIM ** -0.5
          for i in n]
    sps = [jnp.maximum(z, 0.0) + jnp.log(1.0 + jnp.exp(-jnp.abs(z))) for z in zs]
    r2s = []
    for i in n:
        r2 = []
        for t in spans[i]:
            m = mask_of(i, t)
            l1m = lanes(sps[i], t) if m is None else jnp.where(m, lanes(sps[i], t), 0.0)
            hi = l1m.astype(BF16)
            lo = (l1m - hi.astype(F32)).astype(BF16)
            r2.append(jnp.dot(jnp.concatenate([hi, lo], axis=1), sfx, preferred_element_type=F32))
        r2s.append(r2)
    new, wss = [], []
    for i in n:
        c = carries[i]
        ws = []
        for t in spans[i]:
            w = jnp.exp(lanes(zs[i], t) - lanes(sps[i], t) + r2s[i][t][:, :bk] + c)
            m = mask_of(i, t)
            if m is not None:
                w = jnp.where(m, w, 0.0)
            if valid is not None and valid[i][t] is not None:
                w = jnp.where(valid[i][t], w, 0.0)
            ws.append(w.astype(BF16))
            c = c + r2s[i][t][:, bk:]
        new.append(c)
        wss.append(ws[0] if len(ws) == 1 else jnp.concatenate(ws, axis=1))
    pvs = [jnp.dot(wss[i], vs[i], preferred_element_type=F32) for i in n]
    return new, pvs


def _sb_prompt_kernel(q_ref, k_ref, v_ref, z_ref, o_ref, c_scr, acc_scr, *, n_groups, group, ahead):
    blk = SB_BLOCK
    qt = pl.program_id(1)
    sfx = _suffix_matrix(blk)
    row = lax.broadcasted_iota(jnp.int32, (blk, blk), 0)
    col = lax.broadcasted_iota(jnp.int32, (blk, blk), 1)
    causal = col < row
    alive = lambda cs: (functools.reduce(jnp.maximum, [jnp.max(c) for c in cs]) >= SB_LOG_CUTOFF).astype(jnp.int32)

    def kv(kb):
        start = pl.multiple_of(kb * blk, blk)
        return k_ref[pl.ds(start, blk), :], v_ref[pl.ds(start, blk), :]

    def qgroup(ig, _):
        gq0 = (qt * n_groups + ig) * group
        rows = [pl.ds(pl.multiple_of((ig * group + g) * blk, blk), blk) for g in range(group)]
        qs = [q_ref[r, :] for r in rows]
        kbs = [[gq0 + g - s for s in range(1 + ahead)] for g in range(group)]
        kvs = [[kv(jnp.maximum(kb, 0)) for kb in kbs[g]] for g in range(group)]
        cat = lambda xs: jnp.concatenate(xs, axis=0)
        cs, pvs = _sb_tiles(qs, [cat([k for k, _ in kvs[g]]) for g in range(group)],
                            [cat([v for _, v in kvs[g]]) for g in range(group)],
                            [jnp.zeros((blk, blk), F32)] * group, sfx,
                            masks=[[causal] + [None] * ahead] * group,
                            valid=[[None] + [kb >= 0 for kb in kbs[g][1:]] for g in range(group)])
        for g in range(group):
            c_scr[g] = cs[g]
            acc_scr[g] = pvs[g]

        def cond(st):
            s, go = st
            return jnp.logical_and(s <= gq0 + group - 1, go > 0)

        def body(st):
            s, _ = st
            kbs = [gq0 + g - s for g in range(group)]
            kvs = [kv(jnp.maximum(kb, 0)) for kb in kbs]
            cs, pvs = _sb_tiles(qs, [k for k, _ in kvs], [v for _, v in kvs],
                                [c_scr[g] for g in range(group)], sfx, valid=[[kb >= 0] for kb in kbs])
            for g in range(group):
                c_scr[g] = cs[g]
                acc_scr[g] += pvs[g]
            return s + 1, alive(cs)

        lax.while_loop(cond, body, (1 + ahead, alive(cs)))
        for g in range(group):
            o_ref[rows[g], :] = (acc_scr[g] * _silu(z_ref[rows[g], :])).astype(BF16)
        return 0

    lax.fori_loop(0, n_groups, qgroup, 0)


def _sb_prompt_call(q, k, v, z, *, tq=2048, group=8, ahead=2):
    t = q.shape[0]
    tq = min(tq, t)
    assert t % tq == 0 and tq % (SB_BLOCK * group) == 0
    qspec = pl.BlockSpec((tq, HEAD_DIM), lambda h, i: (i, h))
    kvspec = pl.BlockSpec((t, HEAD_DIM), lambda h, i: (0, h))
    return pl.pallas_call(
        functools.partial(_sb_prompt_kernel, n_groups=tq // (SB_BLOCK * group), group=group, ahead=ahead),
        grid=(N_HEADS, t // tq),
        in_specs=[qspec, kvspec, kvspec, qspec],
        out_specs=qspec,
        out_shape=jax.ShapeDtypeStruct((t, WIDTH), BF16),
        scratch_shapes=[pltpu.VMEM((group, SB_BLOCK, SB_BLOCK), F32), pltpu.VMEM((group, SB_BLOCK, HEAD_DIM), F32)],
        compiler_params=_cparams("arbitrary", "arbitrary"),
        name="sb_prompt",
    )(q, k, v, z)


def _sb_sample_kernel(q_ref, kn_ref, vn_ref, z_ref, kc_hbm, vc_hbm, o_ref, kbuf, vbuf, sem, c_scr, acc_scr, *, past):
    blk = SB_BLOCK
    nh = N_HEADS
    b = pl.program_id(0)
    tq = q_ref.shape[0]
    half = blk - tq
    n_full = (past - half) // blk
    rem = (past - half) % blk
    sfx = _suffix_matrix(blk)
    row = lax.broadcasted_iota(jnp.int32, (tq, blk), 0)
    col = lax.broadcasted_iota(jnp.int32, (tq, blk), 1)
    heads = [slice(h * HEAD_DIM, (h + 1) * HEAD_DIM) for h in range(nh)]
    alive = lambda cs: (functools.reduce(jnp.maximum, [jnp.max(c) for c in cs]) >= SB_LOG_CUTOFF).astype(jnp.int32)

    def copies(key0, nkeys, slot):
        src = pl.ds(key0 * nh, nkeys * nh)
        dst = pl.ds(0, nkeys * nh)
        return (pltpu.make_async_copy(kc_hbm.at[b, src, :], kbuf.at[slot, dst, :], sem.at[0, slot]),
                pltpu.make_async_copy(vc_hbm.at[b, src, :], vbuf.at[slot, dst, :], sem.at[1, slot]))

    def start(cps):
        for cp in cps:
            cp.start()

    def wait(cps):
        for cp in cps:
            cp.wait()

    def tile_copies(j):
        return copies(past - half - (j + 1) * blk, blk, (j + 1) % 2)

    def cached(buf, slot, h, nkeys):
        return buf[slot, pl.ds(h, nkeys, stride=nh), :].astype(BF16)

    start(copies(past - half, half, 0))
    if n_full:
        start(tile_copies(0))
    wait(copies(past - half, half, 0))

    qs = [q_ref[:, heads[h]] for h in range(nh)]
    k0 = [jnp.concatenate([cached(kbuf, 0, h, half), kn_ref[:, heads[h]]], axis=0) for h in range(nh)]
    v0 = [jnp.concatenate([cached(vbuf, 0, h, half), vn_ref[:, heads[h]]], axis=0) for h in range(nh)]
    cs, pvs = _sb_tiles(qs, k0, v0, [jnp.zeros((tq, blk), F32)] * nh, sfx, masks=[[col < row + half]] * nh)
    for h in range(nh):
        c_scr[h] = cs[h]
        acc_scr[h] = pvs[h]

    def sweep(slot, mask):
        cs, pvs = _sb_tiles(qs, [cached(kbuf, slot, h, blk) for h in range(nh)],
                            [cached(vbuf, slot, h, blk) for h in range(nh)],
                            [c_scr[h] for h in range(nh)], sfx, masks=None if mask is None else [[mask]] * nh)
        for h in range(nh):
            c_scr[h] = cs[h]
            acc_scr[h] += pvs[h]
        return cs

    def cond(st):
        j, go = st
        return jnp.logical_and(j < n_full, go > 0)

    def body(st):
        j, _ = st
        slot = (j + 1) % 2
        wait(tile_copies(j))

        @pl.when(j + 1 < n_full)
        def _():
            start(tile_copies(j + 1))

        return j + 1, alive(sweep(slot, None))

    done, go = lax.while_loop(cond, body, (0, alive(cs)))

    @pl.when(done < n_full)
    def _():
        wait(tile_copies(done))

    if rem:
        @pl.when(go > 0)
        def _():
            cps = copies(0, blk, 0)
            start(cps)
            wait(cps)
            sweep(0, col < rem)

    for h in range(nh):
        o_ref[:, heads[h]] = (acc_scr[h] * _silu(z_ref[:, heads[h]])).astype(BF16)


def _sb_sample_call(q, kn, vn, z, kc, vc, *, z_row0=0):
    nb = kc.shape[0]
    past = kc.shape[1] // N_HEADS
    tq = q.shape[0] // nb
    assert tq % 16 == 0 and tq < SB_BLOCK and past >= SB_BLOCK and z_row0 % tq == 0
    new = pl.BlockSpec((tq, WIDTH), lambda b: (b, 0))
    zspec = pl.BlockSpec((tq, WIDTH), lambda b: (z_row0 // tq + b, 0))
    hbm = pl.BlockSpec(memory_space=pl.ANY)
    return pl.pallas_call(
        functools.partial(_sb_sample_kernel, past=past),
        grid=(nb,),
        in_specs=[new, new, new, zspec, hbm, hbm],
        out_specs=new,
        out_shape=jax.ShapeDtypeStruct(q.shape, BF16),
        scratch_shapes=[pltpu.VMEM((2, SB_BLOCK * N_HEADS, HEAD_DIM), F32),
                        pltpu.VMEM((2, SB_BLOCK * N_HEADS, HEAD_DIM), F32),
                        pltpu.SemaphoreType.DMA((2, 2)),
                        pltpu.VMEM((N_HEADS, tq, SB_BLOCK), F32), pltpu.VMEM((N_HEADS, tq, HEAD_DIM), F32)],
        compiler_params=_cparams("arbitrary"),
        name="sb_sample",
    )(q, kn, vn, z, kc, vc)


def _prefix_matrix(c):
    t = lax.broadcasted_iota(jnp.int32, (2 * c, c), 0)
    s = lax.broadcasted_iota(jnp.int32, (2 * c, c), 1)
    incl = (t < c) & (s <= t)
    sub = (t >= c) & (s < ((t - c) // HG_SUB) * HG_SUB)
    return jnp.where(incl | sub, 1.0, 0.0).astype(BF16)


def _hgrn_chunk(lf, qh, kh, v, sts, pfx, tril):
    c = lf.shape[0]
    n_sub = c // HG_SUB
    heads = [slice(h * HEAD_DIM, (h + 1) * HEAD_DIM) for h in range(len(sts))]
    p0 = lf.astype(BF16)
    r1 = lf - p0.astype(F32)
    p1 = r1.astype(BF16)
    p2 = (r1 - p1.astype(F32)).astype(BF16)
    br = (jnp.dot(pfx, p0, preferred_element_type=F32) + jnp.dot(pfx, p1, preferred_element_type=F32)
          + jnp.dot(pfx, p2, preferred_element_type=F32))
    b = br[:c]
    r = br[c:]
    b_last = b[c - 1:c, :]
    vb = v
    q_sub = (qh * jnp.exp(b - r)).astype(BF16)
    q_dec = (qh * jnp.exp(b)).astype(BF16)
    k_end = (kh * jnp.exp(b_last - b)).astype(BF16)
    dec = jnp.exp(b_last)
    att = [[] for _ in heads]
    for i in range(n_sub):
        lo, hi = i * HG_SUB, (i + 1) * HG_SUB
        k_i = (kh[:hi] * jnp.exp(r[lo:lo + 1, :] - b[:hi])).astype(BF16)
        if hi < c:
            k_i = jnp.concatenate([k_i, jnp.zeros((c - hi, k_i.shape[1]), BF16)], axis=0)
        for h, hs in enumerate(heads):
            att[h].append(lax.dot_general(q_sub[lo:hi, hs], k_i[:, hs], (((1,), (1,)), ((), ())),
                                          preferred_element_type=F32))
    att = [jnp.where(tril, jnp.concatenate(a, axis=0), 0.0).astype(BF16) for a in att]
    outs, new_sts = [], []
    for h, hs in enumerate(heads):
        o = jnp.dot(att[h], vb[:, hs], preferred_element_type=F32)
        o = o + lax.dot_general(q_dec[:, hs], sts[h].astype(BF16), (((1,), (1,)), ((), ())),
                                preferred_element_type=F32)
        outs.append(o)
    for h, hs in enumerate(heads):
        new_sts.append(sts[h] * dec[:, hs] + lax.dot_general(vb[:, hs], k_end[:, hs], (((0,), (0,)), ((), ())),
                                                             preferred_element_type=F32))
    return outs, new_sts


def _hgrn_kernel(lf_ref, qh_ref, kh_ref, v_ref, zh_ref, g_ref, s0_ref, o_ref, s_ref, st_scr, *, chunk, n_chunks):
    tt = pl.program_id(2)
    nh = st_scr.shape[0]
    heads = [slice(h * HEAD_DIM, (h + 1) * HEAD_DIM) for h in range(nh)]

    @pl.when(tt == 0)
    def _():
        for h in range(nh):
            st_scr[h] = s0_ref[0, h].T

    pfx = _prefix_matrix(chunk)
    ti = lax.broadcasted_iota(jnp.int32, (chunk, chunk), 0)
    si = lax.broadcasted_iota(jnp.int32, (chunk, chunk), 1)
    tril = si <= ti

    def step(ci, _):
        r0 = pl.multiple_of(ci * chunk, chunk)
        rs = pl.ds(r0, chunk)
        outs, sts = _hgrn_chunk(lf_ref[rs, :], qh_ref[rs, :], kh_ref[rs, :], v_ref[rs, :],
                                [st_scr[h] for h in range(nh)], pfx, tril)
        for h in range(nh):
            st_scr[h] = sts[h]
            o = outs[h]
            ms = jnp.mean(o * o, axis=-1, keepdims=True)
            o_ref[rs, heads[h]] = (o * lax.rsqrt(ms + EPS) * g_ref[h] * _silu(zh_ref[rs, heads[h]])).astype(BF16)
        return 0

    lax.fori_loop(0, n_chunks, step, 0)

    @pl.when(tt == pl.num_programs(2) - 1)
    def _():
        for h in range(nh):
            s_ref[0, h] = st_scr[h].T


def _hgrn_call(lf, qh, kh, v, zh, gain, s0, *, t, chunk, tile, heads, row0=0):
    nb = s0.shape[0]
    tile = min(tile, t)
    assert t % tile == 0 and tile % chunk == 0 and chunk % HG_SUB == 0 and N_HEADS % heads == 0 and row0 % tile == 0
    nt = t // tile
    tok_in = pl.BlockSpec((tile, heads * HEAD_DIM), lambda b, h, i: (row0 // tile + b * nt + i, h))
    tok = pl.BlockSpec((tile, heads * HEAD_DIM), lambda b, h, i: (b * nt + i, h))
    state = pl.BlockSpec((1, heads, HEAD_DIM, HEAD_DIM), lambda b, h, i: (b, h, 0, 0))
    return pl.pallas_call(
        functools.partial(_hgrn_kernel, chunk=chunk, n_chunks=tile // chunk),
        grid=(nb, N_HEADS // heads, nt),
        in_specs=[tok_in] * 5 + [pl.BlockSpec((heads, 1, HEAD_DIM), lambda b, h, i: (h, 0, 0)), state],
        out_specs=[tok, state],
        out_shape=[jax.ShapeDtypeStruct((nb * t, WIDTH), BF16), jax.ShapeDtypeStruct(s0.shape, F32)],
        scratch_shapes=[pltpu.VMEM((heads, HEAD_DIM, HEAD_DIM), F32)],
        compiler_params=_cparams("arbitrary", "arbitrary", "arbitrary"),
        name="hgrn2",
    )(lf, qh, kh, v, zh, gain, s0)


def _out_kernel(gs_ref, gh_ref, gsb_ref, ghg_ref, x_ref, gate_ref, wsb_ref, whg_ref, wo_ref, y_ref):
    nb, tr, d = x_ref.shape
    y_sb = jnp.dot(gs_ref[...], wsb_ref[...], preferred_element_type=F32)
    y_h = jnp.dot(gh_ref[...], whg_ref[...], preferred_element_type=F32)
    merged = jax.nn.sigmoid(gsb_ref[...]) * y_sb + jax.nn.sigmoid(ghg_ref[...]) * y_h
    upd = jnp.dot(merged.astype(BF16), wo_ref[...], preferred_element_type=F32)
    y_ref[...] = x_ref[...] + gate_ref[...] * upd.reshape(nb, tr, d)


def _out_call(gs, gh, gg, x, gate, wsb, whg, wo, nb, tr, *, gg_row0=0):
    n, t, d = x.shape
    tm = nb * tr
    nt = t // tr
    assert gg_row0 % tm == 0 and (nb == 1 or nt == 1)
    rowblk = lambda w, c, r0=0: pl.BlockSpec((tm, w), lambda i, j: (r0 // tm + i * nt + j, c))
    const = lambda a: pl.BlockSpec(a.shape, lambda i, j: (0, 0), pipeline_mode=pl.Buffered(1))
    return pl.pallas_call(
        _out_kernel,
        grid=(n // nb, nt),
        in_specs=[rowblk(WIDTH, 0), rowblk(WIDTH, 0), rowblk(d, 0, gg_row0), rowblk(d, 1, gg_row0),
                  pl.BlockSpec((nb, tr, d), lambda i, j: (i, j, 0)),
                  pl.BlockSpec((nb, 1, d), lambda i, j: (i, 0, 0)),
                  const(wsb), const(whg), const(wo)],
        out_specs=pl.BlockSpec((nb, tr, d), lambda i, j: (i, j, 0)),
        out_shape=jax.ShapeDtypeStruct(x.shape, F32),
        compiler_params=_cparams("arbitrary", "arbitrary"),
        name="merge_out",
    )(gs, gh, gg, gg, x, gate, wsb, whg, wo)


def _layer(x_p, x_s, mod_p, mod_s, p, s0_s, caches):
    n_p, t_p, d = x_p.shape
    n_s, t_s, _ = x_s.shape
    assert n_p == 1
    rows_p, rows_s = t_p, n_s * t_s
    tm = math.gcd(1024, rows_p, rows_s)
    w_in = p["w_in"]

    h, q_p = _prenorm_q_call(x_p, p["norm_gain"], mod_p[1], mod_p[0], w_in, p["q_gain"], tm=tm, extra_rows=rows_s)
    pn_tr = min(1024, t_s)
    pn_nb = max(1, min(n_s, 1024 // pn_tr))
    h = _prenorm_call(x_s, p["norm_gain"], mod_s[1], mod_s[0], pn_nb, pn_tr, h, rows_p)

    proj = functools.partial(_proj_call, h, w_in, tm=tm)
    prompt, sample = dict(row0=0, rows=rows_p), dict(row0=rows_p, rows=rows_s)
    (q_s,) = proj(0 * WIDTH, WIDTH, "norm_q", (p["q_gain"],), **sample)
    k_p, kb_p = proj(1 * WIDTH, WIDTH, "norm_k", (p["k_gain"],), **prompt)
    k_s, kb_s = proj(1 * WIDTH, WIDTH, "norm_k", (p["k_gain"],), **sample)
    v_p, vb_p = proj(2 * WIDTH, WIDTH, "copy2", **prompt)
    v_s, vb_s = proj(2 * WIDTH, WIDTH, "copy2", **sample)
    (z_sb,) = proj(3 * WIDTH, WIDTH, "plain")
    logf, k_h = proj(4 * WIDTH, WIDTH, "forget", (p["lb_raw"],))
    (i_h,) = proj(5 * WIDTH, WIDTH, "plain_bf16")
    (q_h,) = proj(6 * WIDTH, WIDTH, "silu")
    (z_h,) = proj(7 * WIDTH, WIDTH, "plain")
    (gg,) = proj(8 * WIDTH, 2 * d, "plain")

    gs_p = _sb_prompt_call(q_p, kb_p, vb_p, z_sb)
    gs_s = _sb_sample_call(q_s, kb_s, vb_s, z_sb, caches[0], caches[1], z_row0=rows_p)
    hgrn = functools.partial(_hgrn_call, logf, q_h, k_h, i_h, z_h, p["onorm_gain"])
    gh_p, s_p = hgrn(jnp.zeros((n_p, N_HEADS, HEAD_DIM, HEAD_DIM), F32), t=t_p, chunk=min(128, t_p), tile=512,
                     heads=N_HEADS)
    gh_s, s_s = hgrn(s0_s, t=t_s, chunk=t_s, tile=t_s, heads=N_HEADS, row0=rows_p)
    out = functools.partial(_out_call, wsb=p["w_br_sb"], whg=p["w_br_hg"], wo=p["w_out"])
    y_p = out(gs_p, gh_p, gg, x_p, mod_p[2], nb=1, tr=min(256, t_p))
    y_s = out(gs_s, gh_s, gg, x_s, mod_s[2], nb=max(1, min(n_s, 256 // t_s)), tr=t_s, gg_row0=rows_p)
    heads5 = lambda a, n, t: a.reshape(1, n, t, N_HEADS, HEAD_DIM)
    return (y_p, y_s, heads5(k_p, n_p, t_p), heads5(v_p, n_p, t_p), s_p[None],
            heads5(k_s, n_s, t_s), heads5(v_s, n_s, t_s), s_s[None])


def kernel(x_prompt, x_sample, cache_sb_k, cache_sb_v, state_hgrn, c_prompt, c_sample, norm_gain, w_ada, b_ada, w_in, q_norm_gain, k_norm_gain, hgrn_lb_raw, hgrn_onorm_gain, w_branch_sb, w_branch_hgrn, w_out):
    assert w_in.shape[0] == 1, "single-layer trunk"
    n_p, t_p, d = x_prompt.shape
    n_s, t_s, _ = x_sample.shape
    past = cache_sb_k.shape[2]

    c_all = jnp.concatenate([c_prompt, c_sample], axis=0)
    pad = (-c_all.shape[0]) % 8
    c_all = jnp.pad(c_all, ((0, pad), (0, 0)))
    mod = _ada_call(c_all, w_ada[0], b_ada[0].reshape(1, 3 * d))
    mods = lambda lo, hi: tuple(mod[lo:hi, i * d:(i + 1) * d].reshape(hi - lo, 1, d) for i in range(3))

    p = {
        "norm_gain": norm_gain[0].reshape(1, 1, d),
        "w_in": w_in[0],
        "q_gain": q_norm_gain[0].reshape(1, HEAD_DIM),
        "k_gain": k_norm_gain[0].reshape(1, HEAD_DIM),
        "lb_raw": hgrn_lb_raw,
        "onorm_gain": hgrn_onorm_gain[0].reshape(N_HEADS, 1, HEAD_DIM),
        "w_br_sb": w_branch_sb[0].astype(BF16),
        "w_br_hg": w_branch_hgrn[0].astype(BF16),
        "w_out": w_out[0].astype(BF16),
    }

    return _layer(x_prompt, x_sample, mods(0, n_p), mods(n_p, n_p + n_s), p, state_hgrn[0],
                  (cache_sb_k.reshape(n_s, past * N_HEADS, HEAD_DIM), cache_sb_v.reshape(n_s, past * N_HEADS, HEAD_DIM)))
```

```python
import functools
import math

import jax
import jax.numpy as jnp
from jax import lax
from jax.experimental import pallas as pl
from jax.experimental.pallas import tpu as pltpu

F32 = jnp.float32
BF16 = jnp.bfloat16

N_HEADS = 8
HEAD_DIM = 128
WIDTH = N_HEADS * HEAD_DIM
HG_SUB = 16
EPS = 1e-6
SB_BLOCK = 128
SB_LOG_CUTOFF = -88.0
VMEM_LIMIT = 56 * 1024 * 1024


def _cparams(*sem):
    return pltpu.CompilerParams(dimension_semantics=sem, vmem_limit_bytes=VMEM_LIMIT)


def _silu(x):
    return x * jax.nn.sigmoid(x)


def _ada_kernel(c_ref, w_ref, b_ref, o_ref):
    c = c_ref[...]
    a = _silu(c).astype(BF16)
    o_ref[...] = jnp.dot(a, w_ref[...].astype(BF16), preferred_element_type=F32) + b_ref[...]


def _ada_call(c, w, b):
    r, d = c.shape
    n = w.shape[1]
    tn = 1024
    return pl.pallas_call(
        _ada_kernel,
        grid=(n // tn,),
        in_specs=[pl.BlockSpec((r, d), lambda j: (0, 0)),
                  pl.BlockSpec((d, tn), lambda j: (0, j)),
                  pl.BlockSpec((1, tn), lambda j: (0, j))],
        out_specs=pl.BlockSpec((r, tn), lambda j: (0, j)),
        out_shape=jax.ShapeDtypeStruct((r, n), F32),
        compiler_params=_cparams("arbitrary"),
        name="ada_mod",
    )(c, w, b)


def _prenorm_kernel(x_ref, g_ref, sc_ref, sh_ref, hall_ref, h_ref):
    del hall_ref
    x = x_ref[...]
    ms = jnp.mean(x * x, axis=-1, keepdims=True)
    xn = x * lax.rsqrt(ms + EPS)
    h = xn * g_ref[...] * (1.0 + sc_ref[...]) + sh_ref[...]
    h_ref[...] = h.astype(BF16).reshape(h_ref.shape)


def _prenorm_call(x, gain, scale, shift, nb, tr, h_all, row0):
    n, t, d = x.shape
    tm = nb * tr
    assert row0 % tm == 0 and (nb == 1 or tr == t)
    vec = pl.BlockSpec((nb, 1, d), lambda i, j: (i, 0, 0))
    return pl.pallas_call(
        _prenorm_kernel,
        grid=(n // nb, t // tr),
        in_specs=[pl.BlockSpec((nb, tr, d), lambda i, j: (i, j, 0)),
                  pl.BlockSpec((1, 1, d), lambda i, j: (0, 0, 0)), vec, vec,
                  pl.BlockSpec(memory_space=pl.ANY)],
        out_specs=pl.BlockSpec((tm, d), lambda i, j: (row0 // tm + i * (t // tr) + j, 0)),
        out_shape=jax.ShapeDtypeStruct(h_all.shape, BF16),
        input_output_aliases={4: 0},
        compiler_params=_cparams("arbitrary", "arbitrary"),
        name="prenorm",
    )(x, gain, scale, shift, h_all)


def _head_rms(y, gain):
    outs = []
    for g in range(N_HEADS):
        yh = y[:, g * HEAD_DIM:(g + 1) * HEAD_DIM]
        ms = jnp.mean(yh * yh, axis=-1, keepdims=True)
        outs.append(yh * lax.rsqrt(ms + EPS) * gain)
    return outs


def _proj_kernel(*refs, kind):
    h_ref, w_ref = refs[0], refs[1]
    wb_ref = refs[-1]

    @pl.when(pl.program_id(1) == 0)
    def _():
        wb_ref[...] = w_ref[...].astype(BF16)

    y = jnp.dot(h_ref[...], wb_ref[...], preferred_element_type=F32)
    tm = y.shape[0]
    if kind == "plain":
        refs[2][...] = y
    elif kind == "plain_bf16":
        refs[2][...] = y.astype(BF16)
    elif kind == "silu":
        refs[2][...] = _silu(y)
    elif kind == "copy2":
        for g in range(N_HEADS):
            refs[2][pl.ds(g, tm, stride=N_HEADS), :] = y[:, g * HEAD_DIM:(g + 1) * HEAD_DIM]
        refs[3][...] = y.astype(BF16)
    elif kind == "norm_k":
        gain = refs[2][...]
        for g, o in enumerate(_head_rms(y, gain)):
            refs[3][pl.ds(g, tm, stride=N_HEADS), :] = o
            refs[4][:, g * HEAD_DIM:(g + 1) * HEAD_DIM] = o.astype(BF16)
    elif kind == "forget":
        raw = refs[2][...]
        e = jnp.exp(raw - jnp.max(raw, axis=0, keepdims=True))
        lb = e[0:1, :] / jnp.sum(e, axis=0, keepdims=True)
        f = lb + (1.0 - lb) * jax.nn.sigmoid(y)
        refs[3][...] = jnp.log(f)
        refs[4][...] = 1.0 - f
    else:
        raise ValueError(kind)


def _proj_call(h, w_in, col0, ncols, kind, extra=(), *, tm=512, tn=1024, row0=0, rows=None):
    d = h.shape[1]
    rows = h.shape[0] - row0 if rows is None else rows
    assert col0 % tn == 0 and ncols % tn == 0 and rows % tm == 0 and row0 % tm == 0
    jb, ib = col0 // tn, row0 // tm
    grid = (ncols // tn, rows // tm)
    tile = lambda: pl.BlockSpec((tm, tn), lambda j, i: (i, j))
    in_specs = [pl.BlockSpec((tm, d), lambda j, i: (ib + i, 0)),
                pl.BlockSpec((d, tn), lambda j, i: (0, jb + j))]
    for e in extra:
        in_specs.append(pl.BlockSpec(e.shape, lambda j, i: (0, 0)))
    if kind in ("plain", "silu"):
        out_dt = (F32,)
    elif kind == "plain_bf16":
        out_dt = (BF16,)
    elif kind in ("copy2", "norm_k"):
        out_dt = (F32, BF16)
    else:
        out_dt = (F32, F32)
    out_specs = [tile() for _ in out_dt]
    out_shape = [jax.ShapeDtypeStruct((rows, ncols), dt) for dt in out_dt]
    if kind in ("copy2", "norm_k"):
        assert ncols == WIDTH
        out_specs[0] = pl.BlockSpec((tm * N_HEADS, HEAD_DIM), lambda j, i: (i, 0))
        out_shape[0] = jax.ShapeDtypeStruct((rows * N_HEADS, HEAD_DIM), F32)
    outs = pl.pallas_call(
        functools.partial(_proj_kernel, kind=kind),
        grid=grid,
        in_specs=in_specs,
        out_specs=out_specs,
        out_shape=out_shape,
        scratch_shapes=[pltpu.VMEM((d, tn), BF16)],
        compiler_params=_cparams("arbitrary", "arbitrary"),
        name="proj_" + kind,
    )(h, w_in, *extra)
    return outs


def _proj_qkv_kernel(h_ref, w_ref, qg_ref, kg_ref, q_ref, k_ref, kb_ref, v_ref, vb_ref):
    j = pl.program_id(1)
    y = jnp.dot(h_ref[...], w_ref[...].astype(BF16), preferred_element_type=F32)
    tm = y.shape[0]
    heads = [slice(g * HEAD_DIM, (g + 1) * HEAD_DIM) for g in range(N_HEADS)]

    @pl.when(j == 0)
    def _():
        for g, o in enumerate(_head_rms(y, qg_ref[...])):
            q_ref[:, heads[g]] = o.astype(BF16)

    @pl.when(j == 1)
    def _():
        for g, o in enumerate(_head_rms(y, kg_ref[...])):
            k_ref[pl.ds(g, tm, stride=N_HEADS), :] = o
            kb_ref[:, heads[g]] = o.astype(BF16)

    @pl.when(j == 2)
    def _():
        for g in range(N_HEADS):
            v_ref[pl.ds(g, tm, stride=N_HEADS), :] = y[:, heads[g]]
        vb_ref[...] = y.astype(BF16)


def _proj_qkv_call(h, w_in, q_gain, k_gain, *, row0, rows, tm=512):
    d = h.shape[1]
    tm = min(tm, rows)
    assert rows % tm == 0 and row0 % tm == 0
    ib = row0 // tm
    wide = lambda: pl.BlockSpec((tm, WIDTH), lambda i, j: (i, 0))
    tall = lambda: pl.BlockSpec((tm * N_HEADS, HEAD_DIM), lambda i, j: (i, 0))
    gain = lambda g: pl.BlockSpec(g.shape, lambda i, j: (0, 0))
    return pl.pallas_call(
        _proj_qkv_kernel,
        grid=(rows // tm, 3),
        in_specs=[pl.BlockSpec((tm, d), lambda i, j: (ib + i, 0)),
                  pl.BlockSpec((d, WIDTH), lambda i, j: (0, j)), gain(q_gain), gain(k_gain)],
        out_specs=[wide(), tall(), wide(), tall(), wide()],
        out_shape=[jax.ShapeDtypeStruct((rows, WIDTH), BF16),
                   jax.ShapeDtypeStruct((rows * N_HEADS, HEAD_DIM), F32), jax.ShapeDtypeStruct((rows, WIDTH), BF16),
                   jax.ShapeDtypeStruct((rows * N_HEADS, HEAD_DIM), F32), jax.ShapeDtypeStruct((rows, WIDTH), BF16)],
        compiler_params=_cparams("arbitrary", "arbitrary"),
        name="proj_qkv",
    )(h, w_in, q_gain, k_gain)


def _prenorm_q_kernel(x_ref, g_ref, sc_ref, sh_ref, w_ref, qg_ref, h_ref, q_ref, wb_ref, h2_ref, *, nrow, extra):
    s = pl.program_id(0)

    @pl.when(s == 0)
    def _():
        wb_ref[...] = w_ref[...].astype(BF16)
        h2_ref[1] = jnp.zeros(h2_ref.shape[1:], BF16)

    y = jnp.dot(h2_ref[(s + 1) % 2], wb_ref[...], preferred_element_type=F32)
    for g, o in enumerate(_head_rms(y, qg_ref[...])):
        q_ref[:, g * HEAD_DIM:(g + 1) * HEAD_DIM] = o.astype(BF16)

    x = x_ref[0]
    ms = jnp.mean(x * x, axis=-1, keepdims=True)
    hn = (x * lax.rsqrt(ms + EPS) * g_ref[0] * (1.0 + sc_ref[0]) + sh_ref[0]).astype(BF16)
    h2_ref[s % 2] = hn
    h_ref[...] = jnp.where(s < nrow, hn, jnp.zeros_like(hn)) if extra else hn


def _prenorm_q_call(x, gain, scale, shift, w_in, q_gain, *, tm, extra_rows=0):
    n, t, d = x.shape
    assert n == 1 and t % tm == 0 and extra_rows % tm == 0
    nrow, extra = t // tm, extra_rows // tm
    steps = nrow + max(1, extra)
    this = lambda s: jnp.minimum(s, nrow - 1)
    prev = lambda s: jnp.minimum(jnp.maximum(s - 1, 0), nrow - 1)
    vec = pl.BlockSpec((1, 1, d), lambda s: (0, 0, 0))
    return pl.pallas_call(
        functools.partial(_prenorm_q_kernel, nrow=nrow, extra=extra),
        grid=(steps,),
        in_specs=[pl.BlockSpec((1, tm, d), lambda s: (0, this(s), 0)), vec, vec, vec,
                  pl.BlockSpec((d, WIDTH), lambda s: (0, 0), pipeline_mode=pl.Buffered(1)),
                  pl.BlockSpec(q_gain.shape, lambda s: (0, 0))],
        out_specs=[pl.BlockSpec((tm, d), lambda s: (jnp.minimum(s, nrow - 1 + extra), 0)),
                   pl.BlockSpec((tm, WIDTH), lambda s: (prev(s), 0))],
        out_shape=[jax.ShapeDtypeStruct((t + extra_rows, d), BF16), jax.ShapeDtypeStruct((t, WIDTH), BF16)],
        scratch_shapes=[pltpu.VMEM((d, WIDTH), BF16), pltpu.VMEM((2, tm, d), BF16)],
        compiler_params=_cparams("arbitrary"),
        name="prenorm_q",
    )(x, gain, scale, shift, w_in, q_gain)


def _suffix_matrix(bk):
    j = lax.broadcasted_iota(jnp.int32, (2 * bk, 2 * bk), 0) % bk
    s = lax.broadcasted_iota(jnp.int32, (2 * bk, 2 * bk), 1)
    return jnp.where((j > s) | (s >= bk), -1.0, 0.0).astype(BF16)


def _sb_tiles(qs, ks, vs, carries, sfx, masks=None, valid=None):
    bk = sfx.shape[0] // 2
    n = range(len(qs))
    spans = [range(ks[i].shape[0] // bk) for i in n]
    lanes = lambda x, t: x[:, t * bk:(t + 1) * bk]
    mask_of = lambda i, t: None if masks is None or masks[i] is None else masks[i][t]
    zs = [lax.dot_general(qs[i], ks[i], (((1,), (1,)), ((), ())), preferred_element_type=F32) * HEAD_DIM ** -0.5
          for i in n]
    sps = [jnp.maximum(z, 0.0) + jnp.log(1.0 + jnp.exp(-jnp.abs(z))) for z in zs]
    r2s = []
    for i in n:
        r2 = []
        for t in spans[i]:
            m = mask_of(i, t)
            l1m = lanes(sps[i], t) if m is None else jnp.where(m, lanes(sps[i], t), 0.0)
            hi = l1m.astype(BF16)
            lo = (l1m - hi.astype(F32)).astype(BF16)
            r2.append(jnp.dot(jnp.concatenate([hi, lo], axis=1), sfx, preferred_element_type=F32))
        r2s.append(r2)
    new, wss = [], []
    for i in n:
        c = carries[i]
        ws = []
        for t in spans[i]:
            w = jnp.exp(lanes(zs[i], t) - lanes(sps[i], t) + r2s[i][t][:, :bk] + c)
            m = mask_of(i, t)
            if m is not None:
                w = jnp.where(m, w, 0.0)
            if valid is not None and valid[i][t] is not None:
                w = jnp.where(valid[i][t], w, 0.0)
            ws.append(w.astype(BF16))
            c = c + r2s[i][t][:, bk:]
        new.append(c)
        wss.append(ws[0] if len(ws) == 1 else jnp.concatenate(ws, axis=1))
    pvs = [jnp.dot(wss[i], vs[i], preferred_element_type=F32) for i in n]
    return new, pvs


def _sb_prompt_kernel(q_ref, k_ref, v_ref, z_ref, o_ref, c_scr, acc_scr, *, n_groups, group, ahead):
    blk = SB_BLOCK
    qt = pl.program_id(1)
    sfx = _suffix_matrix(blk)
    row = lax.broadcasted_iota(jnp.int32, (blk, blk), 0)
    col = lax.broadcasted_iota(jnp.int32, (blk, blk), 1)
    causal = col < row
    alive = lambda cs: (functools.reduce(jnp.maximum, [jnp.max(c) for c in cs]) >= SB_LOG_CUTOFF).astype(jnp.int32)

    def kv(kb):
        start = pl.multiple_of(kb * blk, blk)
        return k_ref[pl.ds(start, blk), :], v_ref[pl.ds(start, blk), :]

    def qgroup(ig, _):
        gq0 = (qt * n_groups + ig) * group
        rows = [pl.ds(pl.multiple_of((ig * group + g) * blk, blk), blk) for g in range(group)]
        qs = [q_ref[r, :] for r in rows]
        kbs = [[gq0 + g - s for s in range(1 + ahead)] for g in range(group)]
        kvs = [[kv(jnp.maximum(kb, 0)) for kb in kbs[g]] for g in range(group)]
        cat = lambda xs: jnp.concatenate(xs, axis=0)
        cs, pvs = _sb_tiles(qs, [cat([k for k, _ in kvs[g]]) for g in range(group)],
                            [cat([v for _, v in kvs[g]]) for g in range(group)],
                            [jnp.zeros((blk, blk), F32)] * group, sfx,
                            masks=[[causal] + [None] * ahead] * group,
                            valid=[[None] + [kb >= 0 for kb in kbs[g][1:]] for g in range(group)])
        for g in range(group):
            c_scr[g] = cs[g]
            acc_scr[g] = pvs[g]

        def cond(st):
            s, go = st
            return jnp.logical_and(s <= gq0 + group - 1, go > 0)

        def body(st):
            s, _ = st
            kbs = [gq0 + g - s for g in range(group)]
            kvs = [kv(jnp.maximum(kb, 0)) for kb in kbs]
            cs, pvs = _sb_tiles(qs, [k for k, _ in kvs], [v for _, v in kvs],
                                [c_scr[g] for g in range(group)], sfx, valid=[[kb >= 0] for kb in kbs])
            for g in range(group):
                c_scr[g] = cs[g]
                acc_scr[g] += pvs[g]
            return s + 1, alive(cs)

        lax.while_loop(cond, body, (1 + ahead, alive(cs)))
        for g in range(group):
            o_ref[rows[g], :] = (acc_scr[g] * _silu(z_ref[rows[g], :])).astype(BF16)
        return 0

    lax.fori_loop(0, n_groups, qgroup, 0)


def _sb_prompt_call(q, k, v, z, *, tq=2048, group=8, ahead=2):
    t = q.shape[0]
    tq = min(tq, t)
    assert t % tq == 0 and tq % (SB_BLOCK * group) == 0
    qspec = pl.BlockSpec((tq, HEAD_DIM), lambda h, i: (i, h))
    kvspec = pl.BlockSpec((t, HEAD_DIM), lambda h, i: (0, h))
    return pl.pallas_call(
        functools.partial(_sb_prompt_kernel, n_groups=tq // (SB_BLOCK * group), group=group, ahead=ahead),
        grid=(N_HEADS, t // tq),
        in_specs=[qspec, kvspec, kvspec, qspec],
        out_specs=qspec,
        out_shape=jax.ShapeDtypeStruct((t, WIDTH), BF16),
        scratch_shapes=[pltpu.VMEM((group, SB_BLOCK, SB_BLOCK), F32), pltpu.VMEM((group, SB_BLOCK, HEAD_DIM), F32)],
        compiler_params=_cparams("arbitrary", "arbitrary"),
        name="sb_prompt",
    )(q, k, v, z)


def _sb_sample_kernel(q_ref, kn_ref, vn_ref, z_ref, kc_hbm, vc_hbm, o_ref, kbuf, vbuf, sem, c_scr, acc_scr, *, past):
    blk = SB_BLOCK
    nh = N_HEADS
    b = pl.program_id(0)
    tq = q_ref.shape[0]
    half = blk - tq
    n_full = (past - half) // blk
    rem = (past - half) % blk
    sfx = _suffix_matrix(blk)
    row = lax.broadcasted_iota(jnp.int32, (tq, blk), 0)
    col = lax.broadcasted_iota(jnp.int32, (tq, blk), 1)
    heads = [slice(h * HEAD_DIM, (h + 1) * HEAD_DIM) for h in range(nh)]
    alive = lambda cs: (functools.reduce(jnp.maximum, [jnp.max(c) for c in cs]) >= SB_LOG_CUTOFF).astype(jnp.int32)

    def copies(key0, nkeys, slot):
        src = pl.ds(key0 * nh, nkeys * nh)
        dst = pl.ds(0, nkeys * nh)
        return (pltpu.make_async_copy(kc_hbm.at[b, src, :], kbuf.at[slot, dst, :], sem.at[0, slot]),
                pltpu.make_async_copy(vc_hbm.at[b, src, :], vbuf.at[slot, dst, :], sem.at[1, slot]))

    def start(cps):
        for cp in cps:
            cp.start()

    def wait(cps):
        for cp in cps:
            cp.wait()

    def tile_copies(j):
        return copies(past - half - (j + 1) * blk, blk, (j + 1) % 2)

    def cached(buf, slot, h, nkeys):
        return buf[slot, pl.ds(h, nkeys, stride=nh), :].astype(BF16)

    first = min(n_full, 1)
    start(copies(past - half, half, 0))
    if first:
        start(tile_copies(0))
    wait(copies(past - half, half, 0))
    if first:
        wait(tile_copies(0))

    qs = [q_ref[:, heads[h]] for h in range(nh)]

    def span(buf, new_ref, h):
        tiles = [cached(buf, 0, h, half), new_ref[:, heads[h]]] + [cached(buf, 1, h, blk)] * first
        return jnp.concatenate(tiles, axis=0)

    cs, pvs = _sb_tiles(qs, [span(kbuf, kn_ref, h) for h in range(nh)], [span(vbuf, vn_ref, h) for h in range(nh)],
                        [jnp.zeros((tq, blk), F32)] * nh, sfx, masks=[[col < row + half] + [None] * first] * nh)
    for h in range(nh):
        c_scr[h] = cs[h]
        acc_scr[h] = pvs[h]

    def sweep(slot, mask):
        cs, pvs = _sb_tiles(qs, [cached(kbuf, slot, h, blk) for h in range(nh)],
                            [cached(vbuf, slot, h, blk) for h in range(nh)],
                            [c_scr[h] for h in range(nh)], sfx, masks=None if mask is None else [[mask]] * nh)
        for h in range(nh):
            c_scr[h] = cs[h]
            acc_scr[h] += pvs[h]
        return cs

    def cond(st):
        j, go = st
        return jnp.logical_and(j < n_full, go > 0)

    def body(st):
        j, _ = st
        cps = tile_copies(j)
        start(cps)
        wait(cps)
        return j + 1, alive(sweep((j + 1) % 2, None))

    _, go = lax.while_loop(cond, body, (first, alive(cs)))

    if rem:
        @pl.when(go > 0)
        def _():
            cps = copies(0, blk, 0)
            start(cps)
            wait(cps)
            sweep(0, col < rem)

    for h in range(nh):
        o_ref[:, heads[h]] = (acc_scr[h] * _silu(z_ref[:, heads[h]])).astype(BF16)


def _sb_sample_call(q, kn, vn, z, kc, vc, *, z_row0=0):
    nb = kc.shape[0]
    past = kc.shape[1] // N_HEADS
    tq = q.shape[0] // nb
    assert tq % 16 == 0 and tq < SB_BLOCK and past >= SB_BLOCK and z_row0 % tq == 0
    new = pl.BlockSpec((tq, WIDTH), lambda b: (b, 0))
    zspec = pl.BlockSpec((tq, WIDTH), lambda b: (z_row0 // tq + b, 0))
    hbm = pl.BlockSpec(memory_space=pl.ANY)
    return pl.pallas_call(
        functools.partial(_sb_sample_kernel, past=past),
        grid=(nb,),
        in_specs=[new, new, new, zspec, hbm, hbm],
        out_specs=new,
        out_shape=jax.ShapeDtypeStruct(q.shape, BF16),
        scratch_shapes=[pltpu.VMEM((2, SB_BLOCK * N_HEADS, HEAD_DIM), F32),
                        pltpu.VMEM((2, SB_BLOCK * N_HEADS, HEAD_DIM), F32),
                        pltpu.SemaphoreType.DMA((2, 2)),
                        pltpu.VMEM((N_HEADS, tq, SB_BLOCK), F32), pltpu.VMEM((N_HEADS, tq, HEAD_DIM), F32)],
        compiler_params=_cparams("arbitrary"),
        name="sb_sample",
    )(q, kn, vn, z, kc, vc)


def _prefix_matrix(c):
    t = lax.broadcasted_iota(jnp.int32, (2 * c, c), 0)
    s = lax.broadcasted_iota(jnp.int32, (2 * c, c), 1)
    incl = (t < c) & (s <= t)
    sub = (t >= c) & (s < ((t - c) // HG_SUB) * HG_SUB)
    return jnp.where(incl | sub, 1.0, 0.0).astype(BF16)


def _hgrn_front(lf, qh, kh, n_heads, pfx, tril):
    c = lf.shape[0]
    n_sub = c // HG_SUB
    heads = [slice(h * HEAD_DIM, (h + 1) * HEAD_DIM) for h in range(n_heads)]
    p0 = lf.astype(BF16)
    r1 = lf - p0.astype(F32)
    p1 = r1.astype(BF16)
    p2 = (r1 - p1.astype(F32)).astype(BF16)
    br = (jnp.dot(pfx, p0, preferred_element_type=F32) + jnp.dot(pfx, p1, preferred_element_type=F32)
          + jnp.dot(pfx, p2, preferred_element_type=F32))
    b = br[:c]
    r = br[c:]
    b_last = b[c - 1:c, :]
    q_sub = (qh * jnp.exp(b - r)).astype(BF16)
    q_dec = (qh * jnp.exp(b)).astype(BF16)
    k_end = (kh * jnp.exp(b_last - b)).astype(BF16)
    dec = jnp.exp(b_last)
    att = [[] for _ in heads]
    for i in range(n_sub):
        lo, hi = i * HG_SUB, (i + 1) * HG_SUB
        k_i = (kh[:hi] * jnp.exp(r[lo:lo + 1, :] - b[:hi])).astype(BF16)
        if hi < c:
            k_i = jnp.concatenate([k_i, jnp.zeros((c - hi, k_i.shape[1]), BF16)], axis=0)
        for h, hs in enumerate(heads):
            att[h].append(lax.dot_general(q_sub[lo:hi, hs], k_i[:, hs], (((1,), (1,)), ((), ())),
                                          preferred_element_type=F32))
    att = [jnp.where(tril, jnp.concatenate(a, axis=0), 0.0).astype(BF16) for a in att]
    return att, q_dec, k_end, dec


def _hgrn_back(front, v, sts):
    att, q_dec, k_end, dec = front
    heads = [slice(h * HEAD_DIM, (h + 1) * HEAD_DIM) for h in range(len(sts))]
    vb = v
    outs, new_sts = [], []
    for h, hs in enumerate(heads):
        o = jnp.dot(att[h], vb[:, hs], preferred_element_type=F32)
        o = o + lax.dot_general(q_dec[:, hs], sts[h].astype(BF16), (((1,), (1,)), ((), ())),
                                preferred_element_type=F32)
        outs.append(o)
    for h, hs in enumerate(heads):
        new_sts.append(sts[h] * dec[:, hs] + lax.dot_general(vb[:, hs], k_end[:, hs], (((0,), (0,)), ((), ())),
                                                             preferred_element_type=F32))
    return outs, new_sts


def _hgrn_kernel(lf_ref, qh_ref, kh_ref, v_ref, zh_ref, g_ref, s0_ref, o_ref, s_ref, st_scr, *, chunk, n_chunks):
    tt = pl.program_id(2)
    nh = st_scr.shape[0]
    heads = [slice(h * HEAD_DIM, (h + 1) * HEAD_DIM) for h in range(nh)]

    @pl.when(tt == 0)
    def _():
        for h in range(nh):
            st_scr[h] = s0_ref[0, h].T

    pfx = _prefix_matrix(chunk)
    ti = lax.broadcasted_iota(jnp.int32, (chunk, chunk), 0)
    si = lax.broadcasted_iota(jnp.int32, (chunk, chunk), 1)
    tril = si <= ti

    unroll = 2 if n_chunks % 2 == 0 else 1

    def step(ci, _):
        rss = [pl.ds(pl.multiple_of((ci * unroll + u) * chunk, chunk), chunk) for u in range(unroll)]
        fronts = [_hgrn_front(lf_ref[rs, :], qh_ref[rs, :], kh_ref[rs, :], nh, pfx, tril) for rs in rss]
        sts = [st_scr[h] for h in range(nh)]
        for rs, front in zip(rss, fronts):
            outs, sts = _hgrn_back(front, v_ref[rs, :], sts)
            for h in range(nh):
                o = outs[h]
                ms = jnp.mean(o * o, axis=-1, keepdims=True)
                o_ref[rs, heads[h]] = (o * lax.rsqrt(ms + EPS) * g_ref[h] * _silu(zh_ref[rs, heads[h]])).astype(BF16)
        for h in range(nh):
            st_scr[h] = sts[h]
        return 0

    lax.fori_loop(0, n_chunks // unroll, step, 0)

    @pl.when(tt == pl.num_programs(2) - 1)
    def _():
        for h in range(nh):
            s_ref[0, h] = st_scr[h].T


def _hgrn_call(lf, qh, kh, v, zh, gain, s0, *, t, chunk, tile, heads, row0=0):
    nb = s0.shape[0]
    tile = min(tile, t)
    assert t % tile == 0 and tile % chunk == 0 and chunk % HG_SUB == 0 and N_HEADS % heads == 0 and row0 % tile == 0
    nt = t // tile
    tok_in = pl.BlockSpec((tile, heads * HEAD_DIM), lambda b, h, i: (row0 // tile + b * nt + i, h))
    tok = pl.BlockSpec((tile, heads * HEAD_DIM), lambda b, h, i: (b * nt + i, h))
    state = pl.BlockSpec((1, heads, HEAD_DIM, HEAD_DIM), lambda b, h, i: (b, h, 0, 0))
    return pl.pallas_call(
        functools.partial(_hgrn_kernel, chunk=chunk, n_chunks=tile // chunk),
        grid=(nb, N_HEADS // heads, nt),
        in_specs=[tok_in] * 5 + [pl.BlockSpec((heads, 1, HEAD_DIM), lambda b, h, i: (h, 0, 0)), state],
        out_specs=[tok, state],
        out_shape=[jax.ShapeDtypeStruct((nb * t, WIDTH), BF16), jax.ShapeDtypeStruct(s0.shape, F32)],
        scratch_shapes=[pltpu.VMEM((heads, HEAD_DIM, HEAD_DIM), F32)],
        compiler_params=_cparams("arbitrary", "arbitrary", "arbitrary"),
        name="hgrn2",
    )(lf, qh, kh, v, zh, gain, s0)


def _out_kernel(gs_ref, gh_ref, gsb_ref, ghg_ref, x_ref, gate_ref, wsb_ref, whg_ref, wo_ref, y_ref):
    nb, tr, d = x_ref.shape
    y_sb = jnp.dot(gs_ref[...], wsb_ref[...], preferred_element_type=F32)
    y_h = jnp.dot(gh_ref[...], whg_ref[...], preferred_element_type=F32)
    merged = jax.nn.sigmoid(gsb_ref[...]) * y_sb + jax.nn.sigmoid(ghg_ref[...]) * y_h
    upd = jnp.dot(merged.astype(BF16), wo_ref[...], preferred_element_type=F32)
    y_ref[...] = x_ref[...] + gate_ref[...] * upd.reshape(nb, tr, d)


def _out_call(gs, gh, gg, x, gate, wsb, whg, wo, nb, tr, *, gg_row0=0):
    n, t, d = x.shape
    tm = nb * tr
    nt = t // tr
    assert gg_row0 % tm == 0 and (nb == 1 or nt == 1)
    rowblk = lambda w, c, r0=0: pl.BlockSpec((tm, w), lambda i, j: (r0 // tm + i * nt + j, c))
    const = lambda a: pl.BlockSpec(a.shape, lambda i, j: (0, 0), pipeline_mode=pl.Buffered(1))
    return pl.pallas_call(
        _out_kernel,
        grid=(n // nb, nt),
        in_specs=[rowblk(WIDTH, 0), rowblk(WIDTH, 0), rowblk(d, 0, gg_row0), rowblk(d, 1, gg_row0),
                  pl.BlockSpec((nb, tr, d), lambda i, j: (i, j, 0)),
                  pl.BlockSpec((nb, 1, d), lambda i, j: (i, 0, 0)),
                  const(wsb), const(whg), const(wo)],
        out_specs=pl.BlockSpec((nb, tr, d), lambda i, j: (i, j, 0)),
        out_shape=jax.ShapeDtypeStruct(x.shape, F32),
        compiler_params=_cparams("arbitrary", "arbitrary"),
        name="merge_out",
    )(gs, gh, gg, gg, x, gate, wsb, whg, wo)


def _layer(x_p, x_s, mod_p, mod_s, p, s0_s, caches):
    n_p, t_p, d = x_p.shape
    n_s, t_s, _ = x_s.shape
    assert n_p == 1
    rows_p, rows_s = t_p, n_s * t_s
    tm = math.gcd(1024, rows_p, rows_s)
    w_in = p["w_in"]

    h, q_p = _prenorm_q_call(x_p, p["norm_gain"], mod_p[1], mod_p[0], w_in, p["q_gain"], tm=tm, extra_rows=rows_s)
    pn_tr = min(1024, t_s)
    pn_nb = max(1, min(n_s, 1024 // pn_tr))
    h = _prenorm_call(x_s, p["norm_gain"], mod_s[1], mod_s[0], pn_nb, pn_tr, h, rows_p)

    proj = functools.partial(_proj_call, h, w_in, tm=tm)
    prompt, sample = dict(row0=0, rows=rows_p), dict(row0=rows_p, rows=rows_s)
    q_s, k_s, kb_s, v_s, vb_s = _proj_qkv_call(h, w_in, p["q_gain"], p["k_gain"], **sample)
    k_p, kb_p = proj(1 * WIDTH, WIDTH, "norm_k", (p["k_gain"],), **prompt)
    v_p, vb_p = proj(2 * WIDTH, WIDTH, "copy2", **prompt)
    (z_sb,) = proj(3 * WIDTH, WIDTH, "plain")
    logf, k_h = proj(4 * WIDTH, WIDTH, "forget", (p["lb_raw"],))
    (i_h,) = proj(5 * WIDTH, WIDTH, "plain_bf16")
    (q_h,) = proj(6 * WIDTH, WIDTH, "silu")
    (z_h,) = proj(7 * WIDTH, WIDTH, "plain")
    (gg,) = proj(8 * WIDTH, 2 * d, "plain")

    gs_p = _sb_prompt_call(q_p, kb_p, vb_p, z_sb)
    gs_s = _sb_sample_call(q_s, kb_s, vb_s, z_sb, caches[0], caches[1], z_row0=rows_p)
    hgrn = functools.partial(_hgrn_call, logf, q_h, k_h, i_h, z_h, p["onorm_gain"])
    gh_p, s_p = hgrn(jnp.zeros((n_p, N_HEADS, HEAD_DIM, HEAD_DIM), F32), t=t_p, chunk=min(128, t_p), tile=512,
                     heads=N_HEADS)
    gh_s, s_s = hgrn(s0_s, t=t_s, chunk=t_s, tile=t_s, heads=N_HEADS, row0=rows_p)
    out = functools.partial(_out_call, wsb=p["w_br_sb"], whg=p["w_br_hg"], wo=p["w_out"])
    y_p = out(gs_p, gh_p, gg, x_p, mod_p[2], nb=1, tr=min(256, t_p))
    y_s = out(gs_s, gh_s, gg, x_s, mod_s[2], nb=max(1, min(n_s, 256 // t_s)), tr=t_s, gg_row0=rows_p)
    heads5 = lambda a, n, t: a.reshape(1, n, t, N_HEADS, HEAD_DIM)
    return (y_p, y_s, heads5(k_p, n_p, t_p), heads5(v_p, n_p, t_p), s_p[None],
            heads5(k_s, n_s, t_s), heads5(v_s, n_s, t_s), s_s[None])


def kernel(x_prompt, x_sample, cache_sb_k, cache_sb_v, state_hgrn, c_prompt, c_sample, norm_gain, w_ada, b_ada, w_in, q_norm_gain, k_norm_gain, hgrn_lb_raw, hgrn_onorm_gain, w_branch_sb, w_branch_hgrn, w_out):
    assert w_in.shape[0] == 1, "single-layer trunk"
    n_p, t_p, d = x_prompt.shape
    n_s, t_s, _ = x_sample.shape
    past = cache_sb_k.shape[2]

    c_all = jnp.concatenate([c_prompt, c_sample], axis=0)
    pad = (-c_all.shape[0]) % 8
    c_all = jnp.pad(c_all, ((0, pad), (0, 0)))
    mod = _ada_call(c_all, w_ada[0], b_ada[0].reshape(1, 3 * d))
    mods = lambda lo, hi: tuple(mod[lo:hi, i * d:(i + 1) * d].reshape(hi - lo, 1, d) for i in range(3))

    p = {
        "norm_gain": norm_gain[0].reshape(1, 1, d),
        "w_in": w_in[0],
        "q_gain": q_norm_gain[0].reshape(1, HEAD_DIM),
        "k_gain": k_norm_gain[0].reshape(1, HEAD_DIM),
        "lb_raw": hgrn_lb_raw,
        "onorm_gain": hgrn_onorm_gain[0].reshape(N_HEADS, 1, HEAD_DIM),
        "w_br_sb": w_branch_sb[0].astype(BF16),
        "w_br_hg": w_branch_hgrn[0].astype(BF16),
        "w_out": w_out[0].astype(BF16),
    }

    return _layer(x_prompt, x_sample, mods(0, n_p), mods(n_p, n_p + n_s), p, state_hgrn[0],
                  (cache_sb_k.reshape(n_s, past * N_HEADS, HEAD_DIM), cache_sb_v.reshape(n_s, past * N_HEADS, HEAD_DIM)))
```

```python
import functools
import math

import jax
import jax.numpy as jnp
from jax import lax
from jax.experimental import pallas as pl
from jax.experimental.pallas import tpu as pltpu

F32 = jnp.float32
BF16 = jnp.bfloat16

N_HEADS = 8
HEAD_DIM = 128
WIDTH = N_HEADS * HEAD_DIM
HG_SUB = 16
EPS = 1e-6
SB_BLOCK = 128
SB_LOG_CUTOFF = -88.0
VMEM_LIMIT = 56 * 1024 * 1024


def _cparams(*sem):
    return pltpu.CompilerParams(dimension_semantics=sem, vmem_limit_bytes=VMEM_LIMIT)


def _silu(x):
    return x * jax.nn.sigmoid(x)


def _ada_kernel(c_ref, w_ref, b_ref, o_ref):
    c = c_ref[...]
    a = _silu(c).astype(BF16)
    o_ref[...] = jnp.dot(a, w_ref[...].astype(BF16), preferred_element_type=F32) + b_ref[...]


def _ada_call(c, w, b):
    r, d = c.shape
    n = w.shape[1]
    tn = 1024
    return pl.pallas_call(
        _ada_kernel,
        grid=(n // tn,),
        in_specs=[pl.BlockSpec((r, d), lambda j: (0, 0)),
                  pl.BlockSpec((d, tn), lambda j: (0, j)),
                  pl.BlockSpec((1, tn), lambda j: (0, j))],
        out_specs=pl.BlockSpec((r, tn), lambda j: (0, j)),
        out_shape=jax.ShapeDtypeStruct((r, n), F32),
        compiler_params=_cparams("arbitrary"),
        name="ada_mod",
    )(c, w, b)


def _prenorm_kernel(x_ref, g_ref, sc_ref, sh_ref, hall_ref, h_ref):
    del hall_ref
    x = x_ref[...]
    ms = jnp.mean(x * x, axis=-1, keepdims=True)
    xn = x * lax.rsqrt(ms + EPS)
    h = xn * g_ref[...] * (1.0 + sc_ref[...]) + sh_ref[...]
    h_ref[...] = h.astype(BF16).reshape(h_ref.shape)


def _prenorm_call(x, gain, scale, shift, nb, tr, h_all, row0):
    n, t, d = x.shape
    tm = nb * tr
    assert row0 % tm == 0 and (nb == 1 or tr == t)
    vec = pl.BlockSpec((nb, 1, d), lambda i, j: (i, 0, 0))
    return pl.pallas_call(
        _prenorm_kernel,
        grid=(n // nb, t // tr),
        in_specs=[pl.BlockSpec((nb, tr, d), lambda i, j: (i, j, 0)),
                  pl.BlockSpec((1, 1, d), lambda i, j: (0, 0, 0)), vec, vec,
                  pl.BlockSpec(memory_space=pl.ANY)],
        out_specs=pl.BlockSpec((tm, d), lambda i, j: (row0 // tm + i * (t // tr) + j, 0)),
        out_shape=jax.ShapeDtypeStruct(h_all.shape, BF16),
        input_output_aliases={4: 0},
        compiler_params=_cparams("arbitrary", "arbitrary"),
        name="prenorm",
    )(x, gain, scale, shift, h_all)


def _head_rms(y, gain):
    outs = []
    for g in range(N_HEADS):
        yh = y[:, g * HEAD_DIM:(g + 1) * HEAD_DIM]
        ms = jnp.mean(yh * yh, axis=-1, keepdims=True)
        outs.append(yh * lax.rsqrt(ms + EPS) * gain)
    return outs


def _proj_kernel(*refs, kind):
    h_ref, w_ref = refs[0], refs[1]
    wb_ref = refs[-1]

    @pl.when(pl.program_id(1) == 0)
    def _():
        wb_ref[...] = w_ref[...].astype(BF16)

    y = jnp.dot(h_ref[...], wb_ref[...], preferred_element_type=F32)
    tm = y.shape[0]
    if kind == "plain":
        refs[2][...] = y
    elif kind == "plain_bf16":
        refs[2][...] = y.astype(BF16)
    elif kind == "silu":
        refs[2][...] = _silu(y)
    elif kind == "copy2":
        for g in range(N_HEADS):
            refs[2][pl.ds(g, tm, stride=N_HEADS), :] = y[:, g * HEAD_DIM:(g + 1) * HEAD_DIM]
        refs[3][...] = y.astype(BF16)
    elif kind == "norm_k":
        gain = refs[2][...]
        for g, o in enumerate(_head_rms(y, gain)):
            refs[3][pl.ds(g, tm, stride=N_HEADS), :] = o
            refs[4][:, g * HEAD_DIM:(g + 1) * HEAD_DIM] = o.astype(BF16)
    elif kind == "forget":
        raw = refs[2][...]
        e = jnp.exp(raw - jnp.max(raw, axis=0, keepdims=True))
        lb = e[0:1, :] / jnp.sum(e, axis=0, keepdims=True)
        f = lb + (1.0 - lb) * jax.nn.sigmoid(y)
        refs[3][...] = jnp.log(f)
        refs[4][...] = 1.0 - f
    else:
        raise ValueError(kind)


def _proj_call(h, w_in, col0, ncols, kind, extra=(), *, tm=512, tn=1024, row0=0, rows=None):
    d = h.shape[1]
    rows = h.shape[0] - row0 if rows is None else rows
    assert col0 % tn == 0 and ncols % tn == 0 and rows % tm == 0 and row0 % tm == 0
    jb, ib = col0 // tn, row0 // tm
    grid = (ncols // tn, rows // tm)
    tile = lambda: pl.BlockSpec((tm, tn), lambda j, i: (i, j))
    in_specs = [pl.BlockSpec((tm, d), lambda j, i: (ib + i, 0)),
                pl.BlockSpec((d, tn), lambda j, i: (0, jb + j))]
    for e in extra:
        in_specs.append(pl.BlockSpec(e.shape, lambda j, i: (0, 0)))
    if kind in ("plain", "silu"):
        out_dt = (F32,)
    elif kind == "plain_bf16":
        out_dt = (BF16,)
    elif kind in ("copy2", "norm_k"):
        out_dt = (F32, BF16)
    else:
        out_dt = (F32, F32)
    out_specs = [tile() for _ in out_dt]
    out_shape = [jax.ShapeDtypeStruct((rows, ncols), dt) for dt in out_dt]
    if kind in ("copy2", "norm_k"):
        assert ncols == WIDTH
        out_specs[0] = pl.BlockSpec((tm * N_HEADS, HEAD_DIM), lambda j, i: (i, 0))
        out_shape[0] = jax.ShapeDtypeStruct((rows * N_HEADS, HEAD_DIM), F32)
    outs = pl.pallas_call(
        functools.partial(_proj_kernel, kind=kind),
        grid=grid,
        in_specs=in_specs,
        out_specs=out_specs,
        out_shape=out_shape,
        scratch_shapes=[pltpu.VMEM((d, tn), BF16)],
        compiler_params=_cparams("arbitrary", "arbitrary"),
        name="proj_" + kind,
    )(h, w_in, *extra)
    return outs


def _proj_qkv_kernel(h_ref, w_ref, qg_ref, kg_ref, q_ref, k_ref, kb_ref, v_ref, vb_ref):
    j = pl.program_id(1)
    y = jnp.dot(h_ref[...], w_ref[...].astype(BF16), preferred_element_type=F32)
    tm = y.shape[0]
    heads = [slice(g * HEAD_DIM, (g + 1) * HEAD_DIM) for g in range(N_HEADS)]

    @pl.when(j == 0)
    def _():
        for g, o in enumerate(_head_rms(y, qg_ref[...])):
            q_ref[:, heads[g]] = o.astype(BF16)

    @pl.when(j == 1)
    def _():
        for g, o in enumerate(_head_rms(y, kg_ref[...])):
            k_ref[pl.ds(g, tm, stride=N_HEADS), :] = o
            kb_ref[:, heads[g]] = o.astype(BF16)

    @pl.when(j == 2)
    def _():
        for g in range(N_HEADS):
            v_ref[pl.ds(g, tm, stride=N_HEADS), :] = y[:, heads[g]]
        vb_ref[...] = y.astype(BF16)


def _proj_qkv_call(h, w_in, q_gain, k_gain, *, row0, rows, tm=512):
    d = h.shape[1]
    tm = min(tm, rows)
    assert rows % tm == 0 and row0 % tm == 0
    ib = row0 // tm
    wide = lambda: pl.BlockSpec((tm, WIDTH), lambda i, j: (i, 0))
    tall = lambda: pl.BlockSpec((tm * N_HEADS, HEAD_DIM), lambda i, j: (i, 0))
    gain = lambda g: pl.BlockSpec(g.shape, lambda i, j: (0, 0))
    return pl.pallas_call(
        _proj_qkv_kernel,
        grid=(rows // tm, 3),
        in_specs=[pl.BlockSpec((tm, d), lambda i, j: (ib + i, 0)),
                  pl.BlockSpec((d, WIDTH), lambda i, j: (0, j)), gain(q_gain), gain(k_gain)],
        out_specs=[wide(), tall(), wide(), tall(), wide()],
        out_shape=[jax.ShapeDtypeStruct((rows, WIDTH), BF16),
                   jax.ShapeDtypeStruct((rows * N_HEADS, HEAD_DIM), F32), jax.ShapeDtypeStruct((rows, WIDTH), BF16),
                   jax.ShapeDtypeStruct((rows * N_HEADS, HEAD_DIM), F32), jax.ShapeDtypeStruct((rows, WIDTH), BF16)],
        compiler_params=_cparams("arbitrary", "arbitrary"),
        name="proj_qkv",
    )(h, w_in, q_gain, k_gain)


def _prenorm_q_kernel(x_ref, g_ref, sc_ref, sh_ref, w_ref, qg_ref, h_ref, q_ref, wb_ref, h2_ref, *, nrow, extra):
    s = pl.program_id(0)

    @pl.when(s == 0)
    def _():
        wb_ref[...] = w_ref[...].astype(BF16)
        h2_ref[1] = jnp.zeros(h2_ref.shape[1:], BF16)

    y = jnp.dot(h2_ref[(s + 1) % 2], wb_ref[...], preferred_element_type=F32)
    for g, o in enumerate(_head_rms(y, qg_ref[...])):
        q_ref[:, g * HEAD_DIM:(g + 1) * HEAD_DIM] = o.astype(BF16)

    x = x_ref[0]
    ms = jnp.mean(x * x, axis=-1, keepdims=True)
    hn = (x * lax.rsqrt(ms + EPS) * g_ref[0] * (1.0 + sc_ref[0]) + sh_ref[0]).astype(BF16)
    h2_ref[s % 2] = hn
    h_ref[...] = jnp.where(s < nrow, hn, jnp.zeros_like(hn)) if extra else hn


def _prenorm_q_call(x, gain, scale, shift, w_in, q_gain, *, tm, extra_rows=0):
    n, t, d = x.shape
    assert n == 1 and t % tm == 0 and extra_rows % tm == 0
    nrow, extra = t // tm, extra_rows // tm
    steps = nrow + max(1, extra)
    this = lambda s: jnp.minimum(s, nrow - 1)
    prev = lambda s: jnp.minimum(jnp.maximum(s - 1, 0), nrow - 1)
    vec = pl.BlockSpec((1, 1, d), lambda s: (0, 0, 0))
    return pl.pallas_call(
        functools.partial(_prenorm_q_kernel, nrow=nrow, extra=extra),
        grid=(steps,),
        in_specs=[pl.BlockSpec((1, tm, d), lambda s: (0, this(s), 0)), vec, vec, vec,
                  pl.BlockSpec((d, WIDTH), lambda s: (0, 0), pipeline_mode=pl.Buffered(1)),
                  pl.BlockSpec(q_gain.shape, lambda s: (0, 0))],
        out_specs=[pl.BlockSpec((tm, d), lambda s: (jnp.minimum(s, nrow - 1 + extra), 0)),
                   pl.BlockSpec((tm, WIDTH), lambda s: (prev(s), 0))],
        out_shape=[jax.ShapeDtypeStruct((t + extra_rows, d), BF16), jax.ShapeDtypeStruct((t, WIDTH), BF16)],
        scratch_shapes=[pltpu.VMEM((d, WIDTH), BF16), pltpu.VMEM((2, tm, d), BF16)],
        compiler_params=_cparams("arbitrary"),
        name="prenorm_q",
    )(x, gain, scale, shift, w_in, q_gain)


def _suffix_matrix(bk):
    j = lax.broadcasted_iota(jnp.int32, (2 * bk, 2 * bk), 0) % bk
    s = lax.broadcasted_iota(jnp.int32, (2 * bk, 2 * bk), 1)
    return jnp.where((j > s) | (s >= bk), -1.0, 0.0).astype(BF16)


def _sb_tiles(qs, ks, vs, carries, sfx, masks=None, valid=None):
    bk = sfx.shape[0] // 2
    n = range(len(qs))
    spans = [range(ks[i].shape[0] // bk) for i in n]
    lanes = lambda x, t: x[:, t * bk:(t + 1) * bk]
    mask_of = lambda i, t: None if masks is None or masks[i] is None else masks[i][t]
    zs = [lax.dot_general(qs[i], ks[i], (((1,), (1,)), ((), ())), preferred_element_type=F32) * HEAD_DIM ** -0.5
          for i in n]
    sps = [jnp.maximum(z, 0.0) + jnp.log(1.0 + jnp.exp(-jnp.abs(z))) for z in zs]
    r2s = []
    for i in n:
        r2 = []
        for t in spans[i]:
            m = mask_of(i, t)
            l1m = lanes(sps[i], t) if m is None else jnp.where(m, lanes(sps[i], t), 0.0)
            hi = l1m.astype(BF16)
            lo = (l1m - hi.astype(F32)).astype(BF16)
            r2.append(jnp.dot(jnp.concatenate([hi, lo], axis=1), sfx, preferred_element_type=F32))
        r2s.append(r2)
    new, wss = [], []
    for i in n:
        c = carries[i]
        ws = []
        for t in spans[i]:
            w = jnp.exp(lanes(zs[i], t) - lanes(sps[i], t) + r2s[i][t][:, :bk] + c)
            m = mask_of(i, t)
            if m is not None:
                w = jnp.where(m, w, 0.0)
            if valid is not None and valid[i][t] is not None:
                w = jnp.where(valid[i][t], w, 0.0)
            ws.append(w.astype(BF16))
            c = c + r2s[i][t][:, bk:]
        new.append(c)
        wss.append(ws[0] if len(ws) == 1 else jnp.concatenate(ws, axis=1))
    pvs = [jnp.dot(wss[i], vs[i], preferred_element_type=F32) for i in n]
    return new, pvs


def _sb_prompt_kernel(q_ref, k_ref, v_ref, z_ref, o_ref, c_scr, acc_scr, *, n_groups, group, ahead):
    blk = SB_BLOCK
    qt = pl.program_id(1)
    sfx = _suffix_matrix(blk)
    row = lax.broadcasted_iota(jnp.int32, (blk, blk), 0)
    col = lax.broadcasted_iota(jnp.int32, (blk, blk), 1)
    causal = col < row
    alive = lambda cs: (functools.reduce(jnp.maximum, [jnp.max(c) for c in cs]) >= SB_LOG_CUTOFF).astype(jnp.int32)

    def kv(kb):
        start = pl.multiple_of(kb * blk, blk)
        return k_ref[pl.ds(start, blk), :], v_ref[pl.ds(start, blk), :]

    def qgroup(ig, _):
        gq0 = (qt * n_groups + ig) * group
        rows = [pl.ds(pl.multiple_of((ig * group + g) * blk, blk), blk) for g in range(group)]
        qs = [q_ref[r, :] for r in rows]
        kbs = [[gq0 + g - s for s in range(1 + ahead)] for g in range(group)]
        kvs = [[kv(jnp.maximum(kb, 0)) for kb in kbs[g]] for g in range(group)]
        cat = lambda xs: jnp.concatenate(xs, axis=0)
        cs, pvs = _sb_tiles(qs, [cat([k for k, _ in kvs[g]]) for g in range(group)],
                            [cat([v for _, v in kvs[g]]) for g in range(group)],
                            [jnp.zeros((blk, blk), F32)] * group, sfx,
                            masks=[[causal] + [None] * ahead] * group,
                            valid=[[None] + [kb >= 0 for kb in kbs[g][1:]] for g in range(group)])
        for g in range(group):
            c_scr[g] = cs[g]
            acc_scr[g] = pvs[g]

        def cond(st):
            s, go = st
            return jnp.logical_and(s <= gq0 + group - 1, go > 0)

        def body(st):
            s, _ = st
            kbs = [gq0 + g - s for g in range(group)]
            kvs = [kv(jnp.maximum(kb, 0)) for kb in kbs]
            cs, pvs = _sb_tiles(qs, [k for k, _ in kvs], [v for _, v in kvs],
                                [c_scr[g] for g in range(group)], sfx, valid=[[kb >= 0] for kb in kbs])
            for g in range(group):
                c_scr[g] = cs[g]
                acc_scr[g] += pvs[g]
            return s + 1, alive(cs)

        lax.while_loop(cond, body, (1 + ahead, alive(cs)))
        for g in range(group):
            o_ref[rows[g], :] = (acc_scr[g] * _silu(z_ref[rows[g], :])).astype(BF16)
        return 0

    lax.fori_loop(0, n_groups, qgroup, 0)


def _sb_prompt_call(q, k, v, z, *, tq=4096, group=8, ahead=2):
    t = q.shape[0]
    tq = min(tq, t)
    assert t % tq == 0 and tq % (SB_BLOCK * group) == 0
    qspec = pl.BlockSpec((tq, HEAD_DIM), lambda h, i: (i, h))
    kvspec = pl.BlockSpec((t, HEAD_DIM), lambda h, i: (0, h))
    return pl.pallas_call(
        functools.partial(_sb_prompt_kernel, n_groups=tq // (SB_BLOCK * group), group=group, ahead=ahead),
        grid=(N_HEADS, t // tq),
        in_specs=[qspec, kvspec, kvspec, qspec],
        out_specs=qspec,
        out_shape=jax.ShapeDtypeStruct((t, WIDTH), BF16),
        scratch_shapes=[pltpu.VMEM((group, SB_BLOCK, SB_BLOCK), F32), pltpu.VMEM((group, SB_BLOCK, HEAD_DIM), F32)],
        compiler_params=_cparams("arbitrary", "arbitrary"),
        name="sb_prompt",
    )(q, k, v, z)


def _sb_sample_kernel(q_ref, kn_ref, vn_ref, z_ref, kc_hbm, vc_hbm, o_ref, kbuf, vbuf, sem, c_scr, acc_scr, *, past):
    blk = SB_BLOCK
    nh = N_HEADS
    b = pl.program_id(0)
    tq = q_ref.shape[0]
    half = blk - tq
    n_full = (past - half) // blk
    rem = (past - half) % blk
    sfx = _suffix_matrix(blk)
    row = lax.broadcasted_iota(jnp.int32, (tq, blk), 0)
    col = lax.broadcasted_iota(jnp.int32, (tq, blk), 1)
    heads = [slice(h * HEAD_DIM, (h + 1) * HEAD_DIM) for h in range(nh)]
    alive = lambda cs: (functools.reduce(jnp.maximum, [jnp.max(c) for c in cs]) >= SB_LOG_CUTOFF).astype(jnp.int32)

    def copies(key0, nkeys, slot):
        src = pl.ds(key0 * nh, nkeys * nh)
        dst = pl.ds(0, nkeys * nh)
        return (pltpu.make_async_copy(kc_hbm.at[b, src, :], kbuf.at[slot, dst, :], sem.at[0, slot]),
                pltpu.make_async_copy(vc_hbm.at[b, src, :], vbuf.at[slot, dst, :], sem.at[1, slot]))

    def start(cps):
        for cp in cps:
            cp.start()

    def wait(cps):
        for cp in cps:
            cp.wait()

    def tile_copies(j):
        return copies(past - half - (j + 1) * blk, blk, (j + 1) % 2)

    def cached(buf, slot, h, nkeys):
        return buf[slot, pl.ds(h, nkeys, stride=nh), :].astype(BF16)

    first = min(n_full, 1)
    start(copies(past - half, half, 0))
    if first:
        start(tile_copies(0))
    wait(copies(past - half, half, 0))
    if first:
        wait(tile_copies(0))

    qs = [q_ref[:, heads[h]] for h in range(nh)]

    def span(buf, new_ref, h):
        tiles = [cached(buf, 0, h, half), new_ref[:, heads[h]]] + [cached(buf, 1, h, blk)] * first
        return jnp.concatenate(tiles, axis=0)

    cs, pvs = _sb_tiles(qs, [span(kbuf, kn_ref, h) for h in range(nh)], [span(vbuf, vn_ref, h) for h in range(nh)],
                        [jnp.zeros((tq, blk), F32)] * nh, sfx, masks=[[col < row + half] + [None] * first] * nh)
    for h in range(nh):
        c_scr[h] = cs[h]
        acc_scr[h] = pvs[h]

    def sweep(slot, mask):
        cs, pvs = _sb_tiles(qs, [cached(kbuf, slot, h, blk) for h in range(nh)],
                            [cached(vbuf, slot, h, blk) for h in range(nh)],
                            [c_scr[h] for h in range(nh)], sfx, masks=None if mask is None else [[mask]] * nh)
        for h in range(nh):
            c_scr[h] = cs[h]
            acc_scr[h] += pvs[h]
        return cs

    def cond(st):
        j, go = st
        return jnp.logical_and(j < n_full, go > 0)

    def body(st):
        j, _ = st
        cps = tile_copies(j)
        start(cps)
        wait(cps)
        return j + 1, alive(sweep((j + 1) % 2, None))

    _, go = lax.while_loop(cond, body, (first, alive(cs)))

    if rem:
        @pl.when(go > 0)
        def _():
            cps = copies(0, blk, 0)
            start(cps)
            wait(cps)
            sweep(0, col < rem)

    for h in range(nh):
        o_ref[:, heads[h]] = (acc_scr[h] * _silu(z_ref[:, heads[h]])).astype(BF16)


def _sb_sample_call(q, kn, vn, z, kc, vc, *, z_row0=0):
    nb = kc.shape[0]
    past = kc.shape[1] // N_HEADS
    tq = q.shape[0] // nb
    assert tq % 16 == 0 and tq < SB_BLOCK and past >= SB_BLOCK and z_row0 % tq == 0
    new = pl.BlockSpec((tq, WIDTH), lambda b: (b, 0))
    zspec = pl.BlockSpec((tq, WIDTH), lambda b: (z_row0 // tq + b, 0))
    hbm = pl.BlockSpec(memory_space=pl.ANY)
    return pl.pallas_call(
        functools.partial(_sb_sample_kernel, past=past),
        grid=(nb,),
        in_specs=[new, new, new, zspec, hbm, hbm],
        out_specs=new,
        out_shape=jax.ShapeDtypeStruct(q.shape, BF16),
        scratch_shapes=[pltpu.VMEM((2, SB_BLOCK * N_HEADS, HEAD_DIM), F32),
                        pltpu.VMEM((2, SB_BLOCK * N_HEADS, HEAD_DIM), F32),
                        pltpu.SemaphoreType.DMA((2, 2)),
                        pltpu.VMEM((N_HEADS, tq, SB_BLOCK), F32), pltpu.VMEM((N_HEADS, tq, HEAD_DIM), F32)],
        compiler_params=_cparams("arbitrary"),
        name="sb_sample",
    )(q, kn, vn, z, kc, vc)


def _prefix_matrix(c):
    t = lax.broadcasted_iota(jnp.int32, (2 * c, c), 0)
    s = lax.broadcasted_iota(jnp.int32, (2 * c, c), 1)
    incl = (t < c) & (s <= t)
    sub = (t >= c) & (s < ((t - c) // HG_SUB) * HG_SUB)
    return jnp.where(incl | sub, 1.0, 0.0).astype(BF16)


def _hgrn_front(lf, qh, kh, n_heads, pfx, tril):
    c = lf.shape[0]
    n_sub = c // HG_SUB
    heads = [slice(h * HEAD_DIM, (h + 1) * HEAD_DIM) for h in range(n_heads)]
    p0 = lf.astype(BF16)
    r1 = lf - p0.astype(F32)
    p1 = r1.astype(BF16)
    p2 = (r1 - p1.astype(F32)).astype(BF16)
    br = (jnp.dot(pfx, p0, preferred_element_type=F32) + jnp.dot(pfx, p1, preferred_element_type=F32)
          + jnp.dot(pfx, p2, preferred_element_type=F32))
    b = br[:c]
    r = br[c:]
    b_last = b[c - 1:c, :]
    q_sub = (qh * jnp.exp(b - r)).astype(BF16)
    q_dec = (qh * jnp.exp(b)).astype(BF16)
    k_end = (kh * jnp.exp(b_last - b)).astype(BF16)
    dec = jnp.exp(b_last)
    att = [[] for _ in heads]
    for i in range(n_sub):
        lo, hi = i * HG_SUB, (i + 1) * HG_SUB
        k_i = (kh[:hi] * jnp.exp(r[lo:lo + 1, :] - b[:hi])).astype(BF16)
        if hi < c:
            k_i = jnp.concatenate([k_i, jnp.zeros((c - hi, k_i.shape[1]), BF16)], axis=0)
        for h, hs in enumerate(heads):
            att[h].append(lax.dot_general(q_sub[lo:hi, hs], k_i[:, hs], (((1,), (1,)), ((), ())),
                                          preferred_element_type=F32))
    att = [jnp.where(tril, jnp.concatenate(a, axis=0), 0.0).astype(BF16) for a in att]
    return att, q_dec, k_end, dec


def _hgrn_back(front, v, sts):
    att, q_dec, k_end, dec = front
    heads = [slice(h * HEAD_DIM, (h + 1) * HEAD_DIM) for h in range(len(sts))]
    vb = v
    outs, new_sts = [], []
    for h, hs in enumerate(heads):
        o = jnp.dot(att[h], vb[:, hs], preferred_element_type=F32)
        o = o + lax.dot_general(q_dec[:, hs], sts[h].astype(BF16), (((1,), (1,)), ((), ())),
                                preferred_element_type=F32)
        outs.append(o)
    for h, hs in enumerate(heads):
        new_sts.append(sts[h] * dec[:, hs] + lax.dot_general(vb[:, hs], k_end[:, hs], (((0,), (0,)), ((), ())),
                                                             preferred_element_type=F32))
    return outs, new_sts


def _hgrn_kernel(lf_ref, qh_ref, kh_ref, v_ref, zh_ref, g_ref, s0_ref, o_ref, s_ref, st_scr, *, chunk, n_chunks):
    tt = pl.program_id(2)
    nh = st_scr.shape[0]
    heads = [slice(h * HEAD_DIM, (h + 1) * HEAD_DIM) for h in range(nh)]

    @pl.when(tt == 0)
    def _():
        for h in range(nh):
            st_scr[h] = s0_ref[0, h].T

    pfx = _prefix_matrix(chunk)
    ti = lax.broadcasted_iota(jnp.int32, (chunk, chunk), 0)
    si = lax.broadcasted_iota(jnp.int32, (chunk, chunk), 1)
    tril = si <= ti

    unroll = 2 if n_chunks % 2 == 0 else 1

    def step(ci, _):
        rss = [pl.ds(pl.multiple_of((ci * unroll + u) * chunk, chunk), chunk) for u in range(unroll)]
        fronts = [_hgrn_front(lf_ref[rs, :], qh_ref[rs, :], kh_ref[rs, :], nh, pfx, tril) for rs in rss]
        sts = [st_scr[h] for h in range(nh)]
        for rs, front in zip(rss, fronts):
            outs, sts = _hgrn_back(front, v_ref[rs, :], sts)
            for h in range(nh):
                o = outs[h]
                ms = jnp.mean(o * o, axis=-1, keepdims=True)
                o_ref[rs, heads[h]] = (o * lax.rsqrt(ms + EPS) * g_ref[h] * _silu(zh_ref[rs, heads[h]])).astype(BF16)
        for h in range(nh):
            st_scr[h] = sts[h]
        return 0

    lax.fori_loop(0, n_chunks // unroll, step, 0)

    @pl.when(tt == pl.num_programs(2) - 1)
    def _():
        for h in range(nh):
            s_ref[0, h] = st_scr[h].T


def _hgrn_call(lf, qh, kh, v, zh, gain, s0, *, t, chunk, tile, heads, row0=0):
    nb = s0.shape[0]
    tile = min(tile, t)
    assert t % tile == 0 and tile % chunk == 0 and chunk % HG_SUB == 0 and N_HEADS % heads == 0 and row0 % tile == 0
    nt = t // tile
    tok_in = pl.BlockSpec((tile, heads * HEAD_DIM), lambda b, h, i: (row0 // tile + b * nt + i, h))
    tok = pl.BlockSpec((tile, heads * HEAD_DIM), lambda b, h, i: (b * nt + i, h))
    state = pl.BlockSpec((1, heads, HEAD_DIM, HEAD_DIM), lambda b, h, i: (b, h, 0, 0))
    return pl.pallas_call(
        functools.partial(_hgrn_kernel, chunk=chunk, n_chunks=tile // chunk),
        grid=(nb, N_HEADS // heads, nt),
        in_specs=[tok_in] * 5 + [pl.BlockSpec((heads, 1, HEAD_DIM), lambda b, h, i: (h, 0, 0)), state],
        out_specs=[tok, state],
        out_shape=[jax.ShapeDtypeStruct((nb * t, WIDTH), BF16), jax.ShapeDtypeStruct(s0.shape, F32)],
        scratch_shapes=[pltpu.VMEM((heads, HEAD_DIM, HEAD_DIM), F32)],
        compiler_params=_cparams("arbitrary", "arbitrary", "arbitrary"),
        name="hgrn2",
    )(lf, qh, kh, v, zh, gain, s0)


def _out_kernel(gs_ref, gh_ref, gsb_ref, ghg_ref, x_ref, gate_ref, wsb_ref, whg_ref, wo_ref, y_ref):
    nb, tr, d = x_ref.shape
    y_sb = jnp.dot(gs_ref[...], wsb_ref[...], preferred_element_type=F32)
    y_h = jnp.dot(gh_ref[...], whg_ref[...], preferred_element_type=F32)
    merged = jax.nn.sigmoid(gsb_ref[...]) * y_sb + jax.nn.sigmoid(ghg_ref[...]) * y_h
    upd = jnp.dot(merged.astype(BF16), wo_ref[...], preferred_element_type=F32)
    y_ref[...] = x_ref[...] + gate_ref[...] * upd.reshape(nb, tr, d)


def _out_call(gs, gh, gg, x, gate, wsb, whg, wo, nb, tr, *, gg_row0=0):
    n, t, d = x.shape
    tm = nb * tr
    nt = t // tr
    assert gg_row0 % tm == 0 and (nb == 1 or nt == 1)
    rowblk = lambda w, c, r0=0: pl.BlockSpec((tm, w), lambda i, j: (r0 // tm + i * nt + j, c))
    const = lambda a: pl.BlockSpec(a.shape, lambda i, j: (0, 0), pipeline_mode=pl.Buffered(1))
    return pl.pallas_call(
        _out_kernel,
        grid=(n // nb, nt),
        in_specs=[rowblk(WIDTH, 0), rowblk(WIDTH, 0), rowblk(d, 0, gg_row0), rowblk(d, 1, gg_row0),
                  pl.BlockSpec((nb, tr, d), lambda i, j: (i, j, 0)),
                  pl.BlockSpec((nb, 1, d), lambda i, j: (i, 0, 0)),
                  const(wsb), const(whg), const(wo)],
        out_specs=pl.BlockSpec((nb, tr, d), lambda i, j: (i, j, 0)),
        out_shape=jax.ShapeDtypeStruct(x.shape, F32),
        compiler_params=_cparams("arbitrary", "arbitrary"),
        name="merge_out",
    )(gs, gh, gg, gg, x, gate, wsb, whg, wo)


def _layer(x_p, x_s, mod_p, mod_s, p, s0_s, caches):
    n_p, t_p, d = x_p.shape
    n_s, t_s, _ = x_s.shape
    assert n_p == 1
    rows_p, rows_s = t_p, n_s * t_s
    tm = math.gcd(1024, rows_p, rows_s)
    w_in = p["w_in"]

    h, q_p = _prenorm_q_call(x_p, p["norm_gain"], mod_p[1], mod_p[0], w_in, p["q_gain"], tm=tm, extra_rows=rows_s)
    pn_tr = min(1024, t_s)
    pn_nb = max(1, min(n_s, 1024 // pn_tr))
    h = _prenorm_call(x_s, p["norm_gain"], mod_s[1], mod_s[0], pn_nb, pn_tr, h, rows_p)

    proj = functools.partial(_proj_call, h, w_in, tm=tm)
    prompt, sample = dict(row0=0, rows=rows_p), dict(row0=rows_p, rows=rows_s)
    q_s, k_s, kb_s, v_s, vb_s = _proj_qkv_call(h, w_in, p["q_gain"], p["k_gain"], **sample)
    k_p, kb_p = proj(1 * WIDTH, WIDTH, "norm_k", (p["k_gain"],), **prompt)
    v_p, vb_p = proj(2 * WIDTH, WIDTH, "copy2", **prompt)
    (z_sb,) = proj(3 * WIDTH, WIDTH, "plain")
    logf, k_h = proj(4 * WIDTH, WIDTH, "forget", (p["lb_raw"],))
    (i_h,) = proj(5 * WIDTH, WIDTH, "plain_bf16")
    (q_h,) = proj(6 * WIDTH, WIDTH, "silu")
    (z_h,) = proj(7 * WIDTH, WIDTH, "plain")
    (gg,) = proj(8 * WIDTH, 2 * d, "plain")

    gs_p = _sb_prompt_call(q_p, kb_p, vb_p, z_sb)
    gs_s = _sb_sample_call(q_s, kb_s, vb_s, z_sb, caches[0], caches[1], z_row0=rows_p)
    hgrn = functools.partial(_hgrn_call, logf, q_h, k_h, i_h, z_h, p["onorm_gain"])
    gh_p, s_p = hgrn(jnp.zeros((n_p, N_HEADS, HEAD_DIM, HEAD_DIM), F32), t=t_p, chunk=min(128, t_p), tile=1024,
                     heads=N_HEADS)
    gh_s, s_s = hgrn(s0_s, t=t_s, chunk=t_s, tile=t_s, heads=N_HEADS, row0=rows_p)
    out = functools.partial(_out_call, wsb=p["w_br_sb"], whg=p["w_br_hg"], wo=p["w_out"])
    y_p = out(gs_p, gh_p, gg, x_p, mod_p[2], nb=1, tr=min(256, t_p))
    y_s = out(gs_s, gh_s, gg, x_s, mod_s[2], nb=max(1, min(n_s, 256 // t_s)), tr=t_s, gg_row0=rows_p)
    heads5 = lambda a, n, t: a.reshape(1, n, t, N_HEADS, HEAD_DIM)
    return (y_p, y_s, heads5(k_p, n_p, t_p), heads5(v_p, n_p, t_p), s_p[None],
            heads5(k_s, n_s, t_s), heads5(v_s, n_s, t_s), s_s[None])


def kernel(x_prompt, x_sample, cache_sb_k, cache_sb_v, state_hgrn, c_prompt, c_sample, norm_gain, w_ada, b_ada, w_in, q_norm_gain, k_norm_gain, hgrn_lb_raw, hgrn_onorm_gain, w_branch_sb, w_branch_hgrn, w_out):
    assert w_in.shape[0] == 1, "single-layer trunk"
    n_p, t_p, d = x_prompt.shape
    n_s, t_s, _ = x_sample.shape
    past = cache_sb_k.shape[2]

    c_all = jnp.concatenate([c_prompt, c_sample], axis=0)
    pad = (-c_all.shape[0]) % 8
    c_all = jnp.pad(c_all, ((0, pad), (0, 0)))
    mod = _ada_call(c_all, w_ada[0], b_ada[0].reshape(1, 3 * d))
    mods = lambda lo, hi: tuple(mod[lo:hi, i * d:(i + 1) * d].reshape(hi - lo, 1, d) for i in range(3))

    p = {
        "norm_gain": norm_gain[0].reshape(1, 1, d),
        "w_in": w_in[0],
        "q_gain": q_norm_gain[0].reshape(1, HEAD_DIM),
        "k_gain": k_norm_gain[0].reshape(1, HEAD_DIM),
        "lb_raw": hgrn_lb_raw,
        "onorm_gain": hgrn_onorm_gain[0].reshape(N_HEADS, 1, HEAD_DIM),
        "w_br_sb": w_branch_sb[0].astype(BF16),
        "w_br_hg": w_branch_hgrn[0].astype(BF16),
        "w_out": w_out[0].astype(BF16),
    }

    return _layer(x_prompt, x_sample, mods(0, n_p), mods(n_p, n_p + n_s), p, state_hgrn[0],
                  (cache_sb_k.reshape(n_s, past * N_HEADS, HEAD_DIM), cache_sb_v.reshape(n_s, past * N_HEADS, HEAD_DIM)))
```

```python
import functools
import math

import jax
import jax.numpy as jnp
from jax import lax
from jax.experimental import pallas as pl
from jax.experimental.pallas import tpu as pltpu

F32 = jnp.float32
BF16 = jnp.bfloat16

N_HEADS = 8
HEAD_DIM = 128
WIDTH = N_HEADS * HEAD_DIM
HG_SUB = 16
EPS = 1e-6
SB_BLOCK = 128
SB_LOG_CUTOFF = -88.0
VMEM_LIMIT = 56 * 1024 * 1024


def _cparams(*sem):
    return pltpu.CompilerParams(dimension_semantics=sem, vmem_limit_bytes=VMEM_LIMIT)


def _silu(x):
    return x * jax.nn.sigmoid(x)


def _ada_kernel(c_ref, w_ref, b_ref, o_ref):
    c = c_ref[...]
    a = _silu(c).astype(BF16)
    o_ref[...] = jnp.dot(a, w_ref[...].astype(BF16), preferred_element_type=F32) + b_ref[...]


def _ada_call(c, w, b):
    r, d = c.shape
    n = w.shape[1]
    tn = 1024
    return pl.pallas_call(
        _ada_kernel,
        grid=(n // tn,),
        in_specs=[pl.BlockSpec((r, d), lambda j: (0, 0)),
                  pl.BlockSpec((d, tn), lambda j: (0, j)),
                  pl.BlockSpec((1, tn), lambda j: (0, j))],
        out_specs=pl.BlockSpec((r, tn), lambda j: (0, j)),
        out_shape=jax.ShapeDtypeStruct((r, n), F32),
        compiler_params=_cparams("arbitrary"),
        name="ada_mod",
    )(c, w, b)


def _prenorm_kernel(x_ref, g_ref, sc_ref, sh_ref, hall_ref, h_ref):
    del hall_ref
    x = x_ref[...]
    ms = jnp.mean(x * x, axis=-1, keepdims=True)
    xn = x * lax.rsqrt(ms + EPS)
    h = xn * g_ref[...] * (1.0 + sc_ref[...]) + sh_ref[...]
    h_ref[...] = h.astype(BF16).reshape(h_ref.shape)


def _prenorm_call(x, gain, scale, shift, nb, tr, h_all, row0):
    n, t, d = x.shape
    tm = nb * tr
    assert row0 % tm == 0 and (nb == 1 or tr == t)
    vec = pl.BlockSpec((nb, 1, d), lambda i, j: (i, 0, 0))
    return pl.pallas_call(
        _prenorm_kernel,
        grid=(n // nb, t // tr),
        in_specs=[pl.BlockSpec((nb, tr, d), lambda i, j: (i, j, 0)),
                  pl.BlockSpec((1, 1, d), lambda i, j: (0, 0, 0)), vec, vec,
                  pl.BlockSpec(memory_space=pl.ANY)],
        out_specs=pl.BlockSpec((tm, d), lambda i, j: (row0 // tm + i * (t // tr) + j, 0)),
        out_shape=jax.ShapeDtypeStruct(h_all.shape, BF16),
        input_output_aliases={4: 0},
        compiler_params=_cparams("arbitrary", "arbitrary"),
        name="prenorm",
    )(x, gain, scale, shift, h_all)


def _head_rms(y, gain):
    outs = []
    for g in range(N_HEADS):
        yh = y[:, g * HEAD_DIM:(g + 1) * HEAD_DIM]
        ms = jnp.mean(yh * yh, axis=-1, keepdims=True)
        outs.append(yh * lax.rsqrt(ms + EPS) * gain)
    return outs


def _proj_kernel(*refs, kind):
    h_ref, w_ref = refs[0], refs[1]
    wb_ref = refs[-1]

    @pl.when(pl.program_id(1) == 0)
    def _():
        wb_ref[...] = w_ref[...].astype(BF16)

    y = jnp.dot(h_ref[...], wb_ref[...], preferred_element_type=F32)
    tm = y.shape[0]
    if kind == "plain":
        refs[2][...] = y
    elif kind == "plain_bf16":
        refs[2][...] = y.astype(BF16)
    elif kind == "silu":
        refs[2][...] = _silu(y)
    elif kind == "copy2":
        for g in range(N_HEADS):
            refs[2][pl.ds(g, tm, stride=N_HEADS), :] = y[:, g * HEAD_DIM:(g + 1) * HEAD_DIM]
        refs[3][...] = y.astype(BF16)
    elif kind == "norm_k":
        gain = refs[2][...]
        for g, o in enumerate(_head_rms(y, gain)):
            refs[3][pl.ds(g, tm, stride=N_HEADS), :] = o
            refs[4][:, g * HEAD_DIM:(g + 1) * HEAD_DIM] = o.astype(BF16)
    elif kind == "forget":
        raw = refs[2][...]
        e = jnp.exp(raw - jnp.max(raw, axis=0, keepdims=True))
        lb = e[0:1, :] / jnp.sum(e, axis=0, keepdims=True)
        f = lb + (1.0 - lb) * jax.nn.sigmoid(y)
        refs[3][...] = jnp.log(f)
        refs[4][...] = 1.0 - f
    else:
        raise ValueError(kind)


def _proj_call(h, w_in, col0, ncols, kind, extra=(), *, tm=512, tn=1024, row0=0, rows=None):
    d = h.shape[1]
    rows = h.shape[0] - row0 if rows is None else rows
    assert col0 % tn == 0 and ncols % tn == 0 and rows % tm == 0 and row0 % tm == 0
    jb, ib = col0 // tn, row0 // tm
    grid = (ncols // tn, rows // tm)
    tile = lambda: pl.BlockSpec((tm, tn), lambda j, i: (i, j))
    in_specs = [pl.BlockSpec((tm, d), lambda j, i: (ib + i, 0)),
                pl.BlockSpec((d, tn), lambda j, i: (0, jb + j))]
    for e in extra:
        in_specs.append(pl.BlockSpec(e.shape, lambda j, i: (0, 0)))
    if kind in ("plain", "silu"):
        out_dt = (F32,)
    elif kind == "plain_bf16":
        out_dt = (BF16,)
    elif kind in ("copy2", "norm_k"):
        out_dt = (F32, BF16)
    else:
        out_dt = (F32, F32)
    out_specs = [tile() for _ in out_dt]
    out_shape = [jax.ShapeDtypeStruct((rows, ncols), dt) for dt in out_dt]
    if kind in ("copy2", "norm_k"):
        assert ncols == WIDTH
        out_specs[0] = pl.BlockSpec((tm * N_HEADS, HEAD_DIM), lambda j, i: (i, 0))
        out_shape[0] = jax.ShapeDtypeStruct((rows * N_HEADS, HEAD_DIM), F32)
    outs = pl.pallas_call(
        functools.partial(_proj_kernel, kind=kind),
        grid=grid,
        in_specs=in_specs,
        out_specs=out_specs,
        out_shape=out_shape,
        scratch_shapes=[pltpu.VMEM((d, tn), BF16)],
        compiler_params=_cparams("arbitrary", "arbitrary"),
        name="proj_" + kind,
    )(h, w_in, *extra)
    return outs


def _proj_qkv_kernel(h_ref, w_ref, qg_ref, kg_ref, q_ref, k_ref, kb_ref, v_ref, vb_ref):
    j = pl.program_id(1)
    y = jnp.dot(h_ref[...], w_ref[...].astype(BF16), preferred_element_type=F32)
    tm = y.shape[0]
    heads = [slice(g * HEAD_DIM, (g + 1) * HEAD_DIM) for g in range(N_HEADS)]

    @pl.when(j == 0)
    def _():
        for g, o in enumerate(_head_rms(y, qg_ref[...])):
            q_ref[:, heads[g]] = o.astype(BF16)

    @pl.when(j == 1)
    def _():
        for g, o in enumerate(_head_rms(y, kg_ref[...])):
            k_ref[pl.ds(g, tm, stride=N_HEADS), :] = o
            kb_ref[:, heads[g]] = o.astype(BF16)

    @pl.when(j == 2)
    def _():
        for g in range(N_HEADS):
            v_ref[pl.ds(g, tm, stride=N_HEADS), :] = y[:, heads[g]]
        vb_ref[...] = y.astype(BF16)


def _proj_qkv_call(h, w_in, q_gain, k_gain, *, row0, rows, tm=512):
    d = h.shape[1]
    tm = min(tm, rows)
    assert rows % tm == 0 and row0 % tm == 0
    ib = row0 // tm
    wide = lambda: pl.BlockSpec((tm, WIDTH), lambda i, j: (i, 0))
    tall = lambda: pl.BlockSpec((tm * N_HEADS, HEAD_DIM), lambda i, j: (i, 0))
    gain = lambda g: pl.BlockSpec(g.shape, lambda i, j: (0, 0))
    return pl.pallas_call(
        _proj_qkv_kernel,
        grid=(rows // tm, 3),
        in_specs=[pl.BlockSpec((tm, d), lambda i, j: (ib + i, 0)),
                  pl.BlockSpec((d, WIDTH), lambda i, j: (0, j)), gain(q_gain), gain(k_gain)],
        out_specs=[wide(), tall(), wide(), tall(), wide()],
        out_shape=[jax.ShapeDtypeStruct((rows, WIDTH), BF16),
                   jax.ShapeDtypeStruct((rows * N_HEADS, HEAD_DIM), F32), jax.ShapeDtypeStruct((rows, WIDTH), BF16),
                   jax.ShapeDtypeStruct((rows * N_HEADS, HEAD_DIM), F32), jax.ShapeDtypeStruct((rows, WIDTH), BF16)],
        compiler_params=_cparams("arbitrary", "arbitrary"),
        name="proj_qkv",
    )(h, w_in, q_gain, k_gain)


def _prenorm_q_kernel(x_ref, g_ref, sc_ref, sh_ref, w_ref, qg_ref, h_ref, q_ref, wb_ref, h2_ref, *, nrow, extra):
    s = pl.program_id(0)

    @pl.when(s == 0)
    def _():
        wb_ref[...] = w_ref[...].astype(BF16)
        h2_ref[1] = jnp.zeros(h2_ref.shape[1:], BF16)

    y = jnp.dot(h2_ref[(s + 1) % 2], wb_ref[...], preferred_element_type=F32)
    for g, o in enumerate(_head_rms(y, qg_ref[...])):
        q_ref[:, g * HEAD_DIM:(g + 1) * HEAD_DIM] = o.astype(BF16)

    x = x_ref[0]
    ms = jnp.mean(x * x, axis=-1, keepdims=True)
    hn = (x * lax.rsqrt(ms + EPS) * g_ref[0] * (1.0 + sc_ref[0]) + sh_ref[0]).astype(BF16)
    h2_ref[s % 2] = hn
    h_ref[...] = jnp.where(s < nrow, hn, jnp.zeros_like(hn)) if extra else hn


def _prenorm_q_call(x, gain, scale, shift, w_in, q_gain, *, tm, extra_rows=0):
    n, t, d = x.shape
    assert n == 1 and t % tm == 0 and extra_rows % tm == 0
    nrow, extra = t // tm, extra_rows // tm
    steps = nrow + max(1, extra)
    this = lambda s: jnp.minimum(s, nrow - 1)
    prev = lambda s: jnp.minimum(jnp.maximum(s - 1, 0), nrow - 1)
    vec = pl.BlockSpec((1, 1, d), lambda s: (0, 0, 0))
    return pl.pallas_call(
        functools.partial(_prenorm_q_kernel, nrow=nrow, extra=extra),
        grid=(steps,),
        in_specs=[pl.BlockSpec((1, tm, d), lambda s: (0, this(s), 0)), vec, vec, vec,
                  pl.BlockSpec((d, WIDTH), lambda s: (0, 0), pipeline_mode=pl.Buffered(1)),
                  pl.BlockSpec(q_gain.shape, lambda s: (0, 0))],
        out_specs=[pl.BlockSpec((tm, d), lambda s: (jnp.minimum(s, nrow - 1 + extra), 0)),
                   pl.BlockSpec((tm, WIDTH), lambda s: (prev(s), 0))],
        out_shape=[jax.ShapeDtypeStruct((t + extra_rows, d), BF16), jax.ShapeDtypeStruct((t, WIDTH), BF16)],
        scratch_shapes=[pltpu.VMEM((d, WIDTH), BF16), pltpu.VMEM((2, tm, d), BF16)],
        compiler_params=_cparams("arbitrary"),
        name="prenorm_q",
    )(x, gain, scale, shift, w_in, q_gain)


def _suffix_matrix(bk):
    j = lax.broadcasted_iota(jnp.int32, (2 * bk, 2 * bk), 0) % bk
    s = lax.broadcasted_iota(jnp.int32, (2 * bk, 2 * bk), 1)
    return jnp.where((j > s) | (s >= bk), -1.0, 0.0).astype(BF16)


def _sb_tiles(qs, ks, vs, carries, sfx, masks=None, valid=None):
    bk = sfx.shape[0] // 2
    n = range(len(qs))
    spans = [range(ks[i].shape[0] // bk) for i in n]
    lanes = lambda x, t: x[:, t * bk:(t + 1) * bk]
    mask_of = lambda i, t: None if masks is None or masks[i] is None else masks[i][t]
    zs = [lax.dot_general(qs[i], ks[i], (((1,), (1,)), ((), ())), preferred_element_type=F32) * HEAD_DIM ** -0.5
          for i in n]
    sps = [jnp.maximum(z, 0.0) + jnp.log(1.0 + jnp.exp(-jnp.abs(z))) for z in zs]
    r2s = []
    for i in n:
        r2 = []
        for t in spans[i]:
            m = mask_of(i, t)
            l1m = lanes(sps[i], t) if m is None else jnp.where(m, lanes(sps[i], t), 0.0)
            hi = l1m.astype(BF16)
            lo = (l1m - hi.astype(F32)).astype(BF16)
            r2.append(jnp.dot(jnp.concatenate([hi, lo], axis=1), sfx, preferred_element_type=F32))
        r2s.append(r2)
    new, wss = [], []
    for i in n:
        c = carries[i]
        ws = []
        for t in spans[i]:
            w = jnp.exp(lanes(zs[i], t) - lanes(sps[i], t) + r2s[i][t][:, :bk] + c)
            m = mask_of(i, t)
            if m is not None:
                w = jnp.where(m, w, 0.0)
            if valid is not None and valid[i][t] is not None:
                w = jnp.where(valid[i][t], w, 0.0)
            ws.append(w.astype(BF16))
            c = c + r2s[i][t][:, bk:]
        new.append(c)
        wss.append(ws[0] if len(ws) == 1 else jnp.concatenate(ws, axis=1))
    pvs = [jnp.dot(wss[i], vs[i], preferred_element_type=F32) for i in n]
    return new, pvs


def _sb_prompt_kernel(q_ref, k_ref, v_ref, z_ref, o_ref, c_scr, acc_scr, *, n_groups, group, ahead):
    blk = SB_BLOCK
    qt = pl.program_id(1)
    sfx = _suffix_matrix(blk)
    row = lax.broadcasted_iota(jnp.int32, (blk, blk), 0)
    col = lax.broadcasted_iota(jnp.int32, (blk, blk), 1)
    causal = col < row
    alive = lambda cs: (functools.reduce(jnp.maximum, [jnp.max(c) for c in cs]) >= SB_LOG_CUTOFF).astype(jnp.int32)

    def kv(kb):
        start = pl.multiple_of(kb * blk, blk)
        return k_ref[pl.ds(start, blk), :], v_ref[pl.ds(start, blk), :]

    def qgroup(ig, _):
        gq0 = (qt * n_groups + ig) * group
        rows = [pl.ds(pl.multiple_of((ig * group + g) * blk, blk), blk) for g in range(group)]
        qs = [q_ref[r, :] for r in rows]
        kbs = [[gq0 + g - s for s in range(1 + ahead)] for g in range(group)]
        kvs = [[kv(jnp.maximum(kb, 0)) for kb in kbs[g]] for g in range(group)]
        cat = lambda xs: jnp.concatenate(xs, axis=0)
        cs, pvs = _sb_tiles(qs, [cat([k for k, _ in kvs[g]]) for g in range(group)],
                            [cat([v for _, v in kvs[g]]) for g in range(group)],
                            [jnp.zeros((blk, blk), F32)] * group, sfx,
                            masks=[[causal] + [None] * ahead] * group,
                            valid=[[None] + [kb >= 0 for kb in kbs[g][1:]] for g in range(group)])
        for g in range(group):
            c_scr[g] = cs[g]
            acc_scr[g] = pvs[g]

        def cond(st):
            s, go = st
            return jnp.logical_and(s <= gq0 + group - 1, go > 0)

        def body(st):
            s, _ = st
            kbs = [gq0 + g - s for g in range(group)]
            kvs = [kv(jnp.maximum(kb, 0)) for kb in kbs]
            cs, pvs = _sb_tiles(qs, [k for k, _ in kvs], [v for _, v in kvs],
                                [c_scr[g] for g in range(group)], sfx, valid=[[kb >= 0] for kb in kbs])
            for g in range(group):
                c_scr[g] = cs[g]
                acc_scr[g] += pvs[g]
            return s + 1, alive(cs)

        lax.while_loop(cond, body, (1 + ahead, alive(cs)))
        for g in range(group):
            o_ref[rows[g], :] = (acc_scr[g] * _silu(z_ref[rows[g], :])).astype(BF16)
        return 0

    lax.fori_loop(0, n_groups, qgroup, 0)


def _sb_prompt_call(q, k, v, z, *, tq=4096, group=8, ahead=2):
    t = q.shape[0]
    tq = min(tq, t)
    assert t % tq == 0 and tq % (SB_BLOCK * group) == 0
    qspec = pl.BlockSpec((tq, HEAD_DIM), lambda h, i: (i, h))
    kvspec = pl.BlockSpec((t, HEAD_DIM), lambda h, i: (0, h))
    return pl.pallas_call(
        functools.partial(_sb_prompt_kernel, n_groups=tq // (SB_BLOCK * group), group=group, ahead=ahead),
        grid=(N_HEADS, t // tq),
        in_specs=[qspec, kvspec, kvspec, qspec],
        out_specs=qspec,
        out_shape=jax.ShapeDtypeStruct((t, WIDTH), BF16),
        scratch_shapes=[pltpu.VMEM((group, SB_BLOCK, SB_BLOCK), F32), pltpu.VMEM((group, SB_BLOCK, HEAD_DIM), F32)],
        compiler_params=_cparams("arbitrary", "arbitrary"),
        name="sb_prompt",
    )(q, k, v, z)


def _sb_sample_kernel(q_ref, kn_ref, vn_ref, z_ref, kc_hbm, vc_hbm, o_ref, kbuf, vbuf, sem, c_scr, acc_scr, *, past):
    blk = SB_BLOCK
    nh = N_HEADS
    b = pl.program_id(0)
    tq = q_ref.shape[0]
    half = blk - tq
    n_full = (past - half) // blk
    rem = (past - half) % blk
    sfx = _suffix_matrix(blk)
    row = lax.broadcasted_iota(jnp.int32, (tq, blk), 0)
    col = lax.broadcasted_iota(jnp.int32, (tq, blk), 1)
    heads = [slice(h * HEAD_DIM, (h + 1) * HEAD_DIM) for h in range(nh)]
    alive = lambda cs: (functools.reduce(jnp.maximum, [jnp.max(c) for c in cs]) >= SB_LOG_CUTOFF).astype(jnp.int32)

    def copies(stream, key0, nkeys, slot):
        src = pl.ds(key0 * nh, nkeys * nh)
        dst = pl.ds(0, nkeys * nh)
        return (pltpu.make_async_copy(kc_hbm.at[stream, src, :], kbuf.at[slot, dst, :], sem.at[0, slot]),
                pltpu.make_async_copy(vc_hbm.at[stream, src, :], vbuf.at[slot, dst, :], sem.at[1, slot]))

    def start(cps):
        for cp in cps:
            cp.start()

    def wait(cps):
        for cp in cps:
            cp.wait()

    first = min(n_full, 1)
    base = 2 * (b % 2)

    def tile_copies(j):
        return copies(b, past - half - (j + 1) * blk, blk, base + (j + 1) % 2)

    def first_copies(stream):
        slot0 = 2 * (stream % 2)
        cps = copies(stream, past - half, half, slot0)
        return cps + copies(stream, past - half - blk, blk, slot0 + 1) if first else cps

    def cached(buf, slot, h, nkeys):
        return buf[slot, pl.ds(h, nkeys, stride=nh), :].astype(BF16)

    @pl.when(b == 0)
    def _():
        start(first_copies(b))

    @pl.when(b + 1 < pl.num_programs(0))
    def _():
        start(first_copies(b + 1))

    wait(first_copies(b))

    qs = [q_ref[:, heads[h]] for h in range(nh)]

    def span(buf, new_ref, h):
        tiles = [cached(buf, base, h, half), new_ref[:, heads[h]]] + [cached(buf, base + 1, h, blk)] * first
        return jnp.concatenate(tiles, axis=0)

    cs, pvs = _sb_tiles(qs, [span(kbuf, kn_ref, h) for h in range(nh)], [span(vbuf, vn_ref, h) for h in range(nh)],
                        [jnp.zeros((tq, blk), F32)] * nh, sfx, masks=[[col < row + half] + [None] * first] * nh)
    for h in range(nh):
        c_scr[h] = cs[h]
        acc_scr[h] = pvs[h]

    def sweep(slot, mask):
        cs, pvs = _sb_tiles(qs, [cached(kbuf, slot, h, blk) for h in range(nh)],
                            [cached(vbuf, slot, h, blk) for h in range(nh)],
                            [c_scr[h] for h in range(nh)], sfx, masks=None if mask is None else [[mask]] * nh)
        for h in range(nh):
            c_scr[h] = cs[h]
            acc_scr[h] += pvs[h]
        return cs

    def cond(st):
        j, go = st
        return jnp.logical_and(j < n_full, go > 0)

    def body(st):
        j, _ = st
        cps = tile_copies(j)
        start(cps)
        wait(cps)
        return j + 1, alive(sweep(base + (j + 1) % 2, None))

    _, go = lax.while_loop(cond, body, (first, alive(cs)))

    if rem:
        @pl.when(go > 0)
        def _():
            cps = copies(b, 0, blk, base)
            start(cps)
            wait(cps)
            sweep(base, col < rem)

    for h in range(nh):
        o_ref[:, heads[h]] = (acc_scr[h] * _silu(z_ref[:, heads[h]])).astype(BF16)


def _sb_sample_call(q, kn, vn, z, kc, vc, *, z_row0=0):
    nb = kc.shape[0]
    past = kc.shape[1] // N_HEADS
    tq = q.shape[0] // nb
    assert tq % 16 == 0 and tq < SB_BLOCK and past >= SB_BLOCK and z_row0 % tq == 0
    new = pl.BlockSpec((tq, WIDTH), lambda b: (b, 0))
    zspec = pl.BlockSpec((tq, WIDTH), lambda b: (z_row0 // tq + b, 0))
    hbm = pl.BlockSpec(memory_space=pl.ANY)
    return pl.pallas_call(
        functools.partial(_sb_sample_kernel, past=past),
        grid=(nb,),
        in_specs=[new, new, new, zspec, hbm, hbm],
        out_specs=new,
        out_shape=jax.ShapeDtypeStruct(q.shape, BF16),
        scratch_shapes=[pltpu.VMEM((4, SB_BLOCK * N_HEADS, HEAD_DIM), F32),
                        pltpu.VMEM((4, SB_BLOCK * N_HEADS, HEAD_DIM), F32),
                        pltpu.SemaphoreType.DMA((2, 4)),
                        pltpu.VMEM((N_HEADS, tq, SB_BLOCK), F32), pltpu.VMEM((N_HEADS, tq, HEAD_DIM), F32)],
        compiler_params=_cparams("arbitrary"),
        name="sb_sample",
    )(q, kn, vn, z, kc, vc)


def _prefix_matrix(c):
    t = lax.broadcasted_iota(jnp.int32, (2 * c, c), 0)
    s = lax.broadcasted_iota(jnp.int32, (2 * c, c), 1)
    incl = (t < c) & (s <= t)
    sub = (t >= c) & (s < ((t - c) // HG_SUB) * HG_SUB)
    return jnp.where(incl | sub, 1.0, 0.0).astype(BF16)


def _hgrn_front(lf, qh, kh, n_heads, pfx, tril):
    c = lf.shape[0]
    n_sub = c // HG_SUB
    heads = [slice(h * HEAD_DIM, (h + 1) * HEAD_DIM) for h in range(n_heads)]
    p0 = lf.astype(BF16)
    r1 = lf - p0.astype(F32)
    p1 = r1.astype(BF16)
    p2 = (r1 - p1.astype(F32)).astype(BF16)
    br = (jnp.dot(pfx, p0, preferred_element_type=F32) + jnp.dot(pfx, p1, preferred_element_type=F32)
          + jnp.dot(pfx, p2, preferred_element_type=F32))
    b = br[:c]
    r = br[c:]
    b_last = b[c - 1:c, :]
    q_sub = (qh * jnp.exp(b - r)).astype(BF16)
    q_dec = (qh * jnp.exp(b)).astype(BF16)
    k_end = (kh * jnp.exp(b_last - b)).astype(BF16)
    dec = jnp.exp(b_last)
    att = [[] for _ in heads]
    for i in range(n_sub):
        lo, hi = i * HG_SUB, (i + 1) * HG_SUB
        k_i = (kh[:hi] * jnp.exp(r[lo:lo + 1, :] - b[:hi])).astype(BF16)
        if hi < c:
            k_i = jnp.concatenate([k_i, jnp.zeros((c - hi, k_i.shape[1]), BF16)], axis=0)
        for h, hs in enumerate(heads):
            att[h].append(lax.dot_general(q_sub[lo:hi, hs], k_i[:, hs], (((1,), (1,)), ((), ())),
                                          preferred_element_type=F32))
    att = [jnp.where(tril, jnp.concatenate(a, axis=0), 0.0).astype(BF16) for a in att]
    return att, q_dec, k_end, dec


def _hgrn_back(front, v, sts):
    att, q_dec, k_end, dec = front
    heads = [slice(h * HEAD_DIM, (h + 1) * HEAD_DIM) for h in range(len(sts))]
    vb = v
    outs, new_sts = [], []
    for h, hs in enumerate(heads):
        o = jnp.dot(att[h], vb[:, hs], preferred_element_type=F32)
        o = o + lax.dot_general(q_dec[:, hs], sts[h].astype(BF16), (((1,), (1,)), ((), ())),
                                preferred_element_type=F32)
        outs.append(o)
    for h, hs in enumerate(heads):
        new_sts.append(sts[h] * dec[:, hs] + lax.dot_general(vb[:, hs], k_end[:, hs], (((0,), (0,)), ((), ())),
                                                             preferred_element_type=F32))
    return outs, new_sts


def _hgrn_kernel(lf_ref, qh_ref, kh_ref, v_ref, zh_ref, g_ref, s0_ref, o_ref, s_ref, st_scr, *, chunk, n_chunks):
    tt = pl.program_id(2)
    nh = st_scr.shape[0]
    heads = [slice(h * HEAD_DIM, (h + 1) * HEAD_DIM) for h in range(nh)]

    @pl.when(tt == 0)
    def _():
        for h in range(nh):
            st_scr[h] = s0_ref[0, h].T

    pfx = _prefix_matrix(chunk)
    ti = lax.broadcasted_iota(jnp.int32, (chunk, chunk), 0)
    si = lax.broadcasted_iota(jnp.int32, (chunk, chunk), 1)
    tril = si <= ti

    unroll = 2 if n_chunks % 2 == 0 else 1

    def step(ci, _):
        rss = [pl.ds(pl.multiple_of((ci * unroll + u) * chunk, chunk), chunk) for u in range(unroll)]
        fronts = [_hgrn_front(lf_ref[rs, :], qh_ref[rs, :], kh_ref[rs, :], nh, pfx, tril) for rs in rss]
        sts = [st_scr[h] for h in range(nh)]
        for rs, front in zip(rss, fronts):
            outs, sts = _hgrn_back(front, v_ref[rs, :], sts)
            for h in range(nh):
                o = outs[h]
                ms = jnp.mean(o * o, axis=-1, keepdims=True)
                o_ref[rs, heads[h]] = (o * lax.rsqrt(ms + EPS) * g_ref[h] * _silu(zh_ref[rs, heads[h]])).astype(BF16)
        for h in range(nh):
            st_scr[h] = sts[h]
        return 0

    lax.fori_loop(0, n_chunks // unroll, step, 0)

    @pl.when(tt == pl.num_programs(2) - 1)
    def _():
        for h in range(nh):
            s_ref[0, h] = st_scr[h].T


def _hgrn_call(lf, qh, kh, v, zh, gain, s0, *, t, chunk, tile, heads, row0=0):
    nb = s0.shape[0]
    tile = min(tile, t)
    assert t % tile == 0 and tile % chunk == 0 and chunk % HG_SUB == 0 and N_HEADS % heads == 0 and row0 % tile == 0
    nt = t // tile
    tok_in = pl.BlockSpec((tile, heads * HEAD_DIM), lambda b, h, i: (row0 // tile + b * nt + i, h))
    tok = pl.BlockSpec((tile, heads * HEAD_DIM), lambda b, h, i: (b * nt + i, h))
    state = pl.BlockSpec((1, heads, HEAD_DIM, HEAD_DIM), lambda b, h, i: (b, h, 0, 0))
    return pl.pallas_call(
        functools.partial(_hgrn_kernel, chunk=chunk, n_chunks=tile // chunk),
        grid=(nb, N_HEADS // heads, nt),
        in_specs=[tok_in] * 5 + [pl.BlockSpec((heads, 1, HEAD_DIM), lambda b, h, i: (h, 0, 0)), state],
        out_specs=[tok, state],
        out_shape=[jax.ShapeDtypeStruct((nb * t, WIDTH), BF16), jax.ShapeDtypeStruct(s0.shape, F32)],
        scratch_shapes=[pltpu.VMEM((heads, HEAD_DIM, HEAD_DIM), F32)],
        compiler_params=_cparams("arbitrary", "arbitrary", "arbitrary"),
        name="hgrn2",
    )(lf, qh, kh, v, zh, gain, s0)


def _out_kernel(gs_ref, gh_ref, gsb_ref, ghg_ref, x_ref, gate_ref, wsb_ref, whg_ref, wo_ref, y_ref):
    nb, tr, d = x_ref.shape
    y_sb = jnp.dot(gs_ref[...], wsb_ref[...], preferred_element_type=F32)
    y_h = jnp.dot(gh_ref[...], whg_ref[...], preferred_element_type=F32)
    merged = jax.nn.sigmoid(gsb_ref[...]) * y_sb + jax.nn.sigmoid(ghg_ref[...]) * y_h
    upd = jnp.dot(merged.astype(BF16), wo_ref[...], preferred_element_type=F32)
    y_ref[...] = x_ref[...] + gate_ref[...] * upd.reshape(nb, tr, d)


def _out_call(gs, gh, gg, x, gate, wsb, whg, wo, nb, tr, *, gg_row0=0):
    n, t, d = x.shape
    tm = nb * tr
    nt = t // tr
    assert gg_row0 % tm == 0 and (nb == 1 or nt == 1)
    rowblk = lambda w, c, r0=0: pl.BlockSpec((tm, w), lambda i, j: (r0 // tm + i * nt + j, c))
    const = lambda a: pl.BlockSpec(a.shape, lambda i, j: (0, 0), pipeline_mode=pl.Buffered(1))
    return pl.pallas_call(
        _out_kernel,
        grid=(n // nb, nt),
        in_specs=[rowblk(WIDTH, 0), rowblk(WIDTH, 0), rowblk(d, 0, gg_row0), rowblk(d, 1, gg_row0),
                  pl.BlockSpec((nb, tr, d), lambda i, j: (i, j, 0)),
                  pl.BlockSpec((nb, 1, d), lambda i, j: (i, 0, 0)),
                  const(wsb), const(whg), const(wo)],
        out_specs=pl.BlockSpec((nb, tr, d), lambda i, j: (i, j, 0)),
        out_shape=jax.ShapeDtypeStruct(x.shape, F32),
        compiler_params=_cparams("arbitrary", "arbitrary"),
        name="merge_out",
    )(gs, gh, gg, gg, x, gate, wsb, whg, wo)


def _layer(x_p, x_s, mod_p, mod_s, p, s0_s, caches):
    n_p, t_p, d = x_p.shape
    n_s, t_s, _ = x_s.shape
    assert n_p == 1
    rows_p, rows_s = t_p, n_s * t_s
    tm = math.gcd(1024, rows_p, rows_s)
    w_in = p["w_in"]

    h, q_p = _prenorm_q_call(x_p, p["norm_gain"], mod_p[1], mod_p[0], w_in, p["q_gain"], tm=tm, extra_rows=rows_s)
    pn_tr = min(1024, t_s)
    pn_nb = max(1, min(n_s, 1024 // pn_tr))
    h = _prenorm_call(x_s, p["norm_gain"], mod_s[1], mod_s[0], pn_nb, pn_tr, h, rows_p)

    proj = functools.partial(_proj_call, h, w_in, tm=tm)
    prompt, sample = dict(row0=0, rows=rows_p), dict(row0=rows_p, rows=rows_s)
    q_s, k_s, kb_s, v_s, vb_s = _proj_qkv_call(h, w_in, p["q_gain"], p["k_gain"], **sample)
    k_p, kb_p = proj(1 * WIDTH, WIDTH, "norm_k", (p["k_gain"],), **prompt)
    v_p, vb_p = proj(2 * WIDTH, WIDTH, "copy2", **prompt)
    (z_sb,) = proj(3 * WIDTH, WIDTH, "plain")
    logf, k_h = proj(4 * WIDTH, WIDTH, "forget", (p["lb_raw"],))
    (i_h,) = proj(5 * WIDTH, WIDTH, "plain_bf16")
    (q_h,) = proj(6 * WIDTH, WIDTH, "silu")
    (z_h,) = proj(7 * WIDTH, WIDTH, "plain")
    (gg,) = proj(8 * WIDTH, 2 * d, "plain")

    gs_p = _sb_prompt_call(q_p, kb_p, vb_p, z_sb)
    gs_s = _sb_sample_call(q_s, kb_s, vb_s, z_sb, caches[0], caches[1], z_row0=rows_p)
    hgrn = functools.partial(_hgrn_call, logf, q_h, k_h, i_h, z_h, p["onorm_gain"])
    gh_p, s_p = hgrn(jnp.zeros((n_p, N_HEADS, HEAD_DIM, HEAD_DIM), F32), t=t_p, chunk=min(128, t_p), tile=1024,
                     heads=N_HEADS)
    gh_s, s_s = hgrn(s0_s, t=t_s, chunk=t_s, tile=t_s, heads=N_HEADS, row0=rows_p)
    out = functools.partial(_out_call, wsb=p["w_br_sb"], whg=p["w_br_hg"], wo=p["w_out"])
    y_p = out(gs_p, gh_p, gg, x_p, mod_p[2], nb=1, tr=min(256, t_p))
    y_s = out(gs_s, gh_s, gg, x_s, mod_s[2], nb=max(1, min(n_s, 256 // t_s)), tr=t_s, gg_row0=rows_p)
    heads5 = lambda a, n, t: a.reshape(1, n, t, N_HEADS, HEAD_DIM)
    return (y_p, y_s, heads5(k_p, n_p, t_p), heads5(v_p, n_p, t_p), s_p[None],
            heads5(k_s, n_s, t_s), heads5(v_s, n_s, t_s), s_s[None])


def kernel(x_prompt, x_sample, cache_sb_k, cache_sb_v, state_hgrn, c_prompt, c_sample, norm_gain, w_ada, b_ada, w_in, q_norm_gain, k_norm_gain, hgrn_lb_raw, hgrn_onorm_gain, w_branch_sb, w_branch_hgrn, w_out):
    assert w_in.shape[0] == 1, "single-layer trunk"
    n_p, t_p, d = x_prompt.shape
    n_s, t_s, _ = x_sample.shape
    past = cache_sb_k.shape[2]

    c_all = jnp.concatenate([c_prompt, c_sample], axis=0)
    pad = (-c_all.shape[0]) % 8
    c_all = jnp.pad(c_all, ((0, pad), (0, 0)))
    mod = _ada_call(c_all, w_ada[0], b_ada[0].reshape(1, 3 * d))
    mods = lambda lo, hi: tuple(mod[lo:hi, i * d:(i + 1) * d].reshape(hi - lo, 1, d) for i in range(3))

    p = {
        "norm_gain": norm_gain[0].reshape(1, 1, d),
        "w_in": w_in[0],
        "q_gain": q_norm_gain[0].reshape(1, HEAD_DIM),
        "k_gain": k_norm_gain[0].reshape(1, HEAD_DIM),
        "lb_raw": hgrn_lb_raw,
        "onorm_gain": hgrn_onorm_gain[0].reshape(N_HEADS, 1, HEAD_DIM),
        "w_br_sb": w_branch_sb[0].astype(BF16),
        "w_br_hg": w_branch_hgrn[0].astype(BF16),
        "w_out": w_out[0].astype(BF16),
    }

    return _layer(x_prompt, x_sample, mods(0, n_p), mods(n_p, n_p + n_s), p, state_hgrn[0],
                  (cache_sb_k.reshape(n_s, past * N_HEADS, HEAD_DIM), cache_sb_v.reshape(n_s, past * N_HEADS, HEAD_DIM)))
```

```python
import functools
import math

import jax
import jax.numpy as jnp
from jax import lax
from jax.experimental import pallas as pl
from jax.experimental.pallas import tpu as pltpu

F32 = jnp.float32
BF16 = jnp.bfloat16

N_HEADS = 8
HEAD_DIM = 128
WIDTH = N_HEADS * HEAD_DIM
HG_SUB = 16
EPS = 1e-6
SB_BLOCK = 128
SB_LOG_CUTOFF = -88.0
VMEM_LIMIT = 56 * 1024 * 1024


def _cparams(*sem):
    return pltpu.CompilerParams(dimension_semantics=sem, vmem_limit_bytes=VMEM_LIMIT)


def _silu(x):
    return x * jax.nn.sigmoid(x)


def _ada_kernel(c_ref, w_ref, b_ref, o_ref):
    c = c_ref[...]
    a = _silu(c).astype(BF16)
    o_ref[...] = jnp.dot(a, w_ref[...].astype(BF16), preferred_element_type=F32) + b_ref[...]


def _ada_call(c, w, b):
    r, d = c.shape
    n = w.shape[1]
    tn = 1024
    return pl.pallas_call(
        _ada_kernel,
        grid=(n // tn,),
        in_specs=[pl.BlockSpec((r, d), lambda j: (0, 0)),
                  pl.BlockSpec((d, tn), lambda j: (0, j)),
                  pl.BlockSpec((1, tn), lambda j: (0, j))],
        out_specs=pl.BlockSpec((r, tn), lambda j: (0, j)),
        out_shape=jax.ShapeDtypeStruct((r, n), F32),
        compiler_params=_cparams("arbitrary"),
        name="ada_mod",
    )(c, w, b)


def _prenorm_kernel(x_ref, g_ref, sc_ref, sh_ref, hall_ref, h_ref):
    del hall_ref
    x = x_ref[...]
    ms = jnp.mean(x * x, axis=-1, keepdims=True)
    xn = x * lax.rsqrt(ms + EPS)
    h = xn * g_ref[...] * (1.0 + sc_ref[...]) + sh_ref[...]
    h_ref[...] = h.astype(BF16).reshape(h_ref.shape)


def _prenorm_call(x, gain, scale, shift, nb, tr, h_all, row0):
    n, t, d = x.shape
    tm = nb * tr
    assert row0 % tm == 0 and (nb == 1 or tr == t)
    vec = pl.BlockSpec((nb, 1, d), lambda i, j: (i, 0, 0))
    return pl.pallas_call(
        _prenorm_kernel,
        grid=(n // nb, t // tr),
        in_specs=[pl.BlockSpec((nb, tr, d), lambda i, j: (i, j, 0)),
                  pl.BlockSpec((1, 1, d), lambda i, j: (0, 0, 0)), vec, vec,
                  pl.BlockSpec(memory_space=pl.ANY)],
        out_specs=pl.BlockSpec((tm, d), lambda i, j: (row0 // tm + i * (t // tr) + j, 0)),
        out_shape=jax.ShapeDtypeStruct(h_all.shape, BF16),
        input_output_aliases={4: 0},
        compiler_params=_cparams("arbitrary", "arbitrary"),
        name="prenorm",
    )(x, gain, scale, shift, h_all)


def _head_rms(y, gain):
    outs = []
    for g in range(N_HEADS):
        yh = y[:, g * HEAD_DIM:(g + 1) * HEAD_DIM]
        ms = jnp.mean(yh * yh, axis=-1, keepdims=True)
        outs.append(yh * lax.rsqrt(ms + EPS) * gain)
    return outs


def _proj_kernel(*refs, kind):
    h_ref, w_ref = refs[0], refs[1]
    wb_ref = refs[-1]

    @pl.when(pl.program_id(1) == 0)
    def _():
        wb_ref[...] = w_ref[...].astype(BF16)

    y = jnp.dot(h_ref[...], wb_ref[...], preferred_element_type=F32)
    tm = y.shape[0]
    if kind == "plain":
        refs[2][...] = y
    elif kind == "plain_bf16":
        refs[2][...] = y.astype(BF16)
    elif kind == "silu":
        refs[2][...] = _silu(y)
    elif kind == "copy2":
        for g in range(N_HEADS):
            refs[2][pl.ds(g, tm, stride=N_HEADS), :] = y[:, g * HEAD_DIM:(g + 1) * HEAD_DIM]
        refs[3][...] = y.astype(BF16)
    elif kind == "norm_k":
        gain = refs[2][...]
        for g, o in enumerate(_head_rms(y, gain)):
            refs[3][pl.ds(g, tm, stride=N_HEADS), :] = o
            refs[4][:, g * HEAD_DIM:(g + 1) * HEAD_DIM] = o.astype(BF16)
    elif kind == "forget":
        raw = refs[2][...]
        e = jnp.exp(raw - jnp.max(raw, axis=0, keepdims=True))
        lb = e[0:1, :] / jnp.sum(e, axis=0, keepdims=True)
        f = lb + (1.0 - lb) * jax.nn.sigmoid(y)
        refs[3][...] = jnp.log(f)
        refs[4][...] = 1.0 - f
    else:
        raise ValueError(kind)


def _proj_call(h, w_in, col0, ncols, kind, extra=(), *, tm=512, tn=1024, row0=0, rows=None):
    d = h.shape[1]
    rows = h.shape[0] - row0 if rows is None else rows
    assert col0 % tn == 0 and ncols % tn == 0 and rows % tm == 0 and row0 % tm == 0
    jb, ib = col0 // tn, row0 // tm
    grid = (ncols // tn, rows // tm)
    tile = lambda: pl.BlockSpec((tm, tn), lambda j, i: (i, j))
    in_specs = [pl.BlockSpec((tm, d), lambda j, i: (ib + i, 0)),
                pl.BlockSpec((d, tn), lambda j, i: (0, jb + j))]
    for e in extra:
        in_specs.append(pl.BlockSpec(e.shape, lambda j, i: (0, 0)))
    if kind in ("plain", "silu"):
        out_dt = (F32,)
    elif kind == "plain_bf16":
        out_dt = (BF16,)
    elif kind in ("copy2", "norm_k"):
        out_dt = (F32, BF16)
    else:
        out_dt = (F32, F32)
    out_specs = [tile() for _ in out_dt]
    out_shape = [jax.ShapeDtypeStruct((rows, ncols), dt) for dt in out_dt]
    if kind in ("copy2", "norm_k"):
        assert ncols == WIDTH
        out_specs[0] = pl.BlockSpec((tm * N_HEADS, HEAD_DIM), lambda j, i: (i, 0))
        out_shape[0] = jax.ShapeDtypeStruct((rows * N_HEADS, HEAD_DIM), F32)
    outs = pl.pallas_call(
        functools.partial(_proj_kernel, kind=kind),
        grid=grid,
        in_specs=in_specs,
        out_specs=out_specs,
        out_shape=out_shape,
        scratch_shapes=[pltpu.VMEM((d, tn), BF16)],
        compiler_params=_cparams("arbitrary", "arbitrary"),
        name="proj_" + kind,
    )(h, w_in, *extra)
    return outs


def _proj_qkv_kernel(h_ref, w_ref, qg_ref, kg_ref, q_ref, k_ref, kb_ref, v_ref, vb_ref):
    j = pl.program_id(1)
    y = jnp.dot(h_ref[...], w_ref[...].astype(BF16), preferred_element_type=F32)
    tm = y.shape[0]
    heads = [slice(g * HEAD_DIM, (g + 1) * HEAD_DIM) for g in range(N_HEADS)]

    @pl.when(j == 0)
    def _():
        for g, o in enumerate(_head_rms(y, qg_ref[...])):
            q_ref[:, heads[g]] = o.astype(BF16)

    @pl.when(j == 1)
    def _():
        for g, o in enumerate(_head_rms(y, kg_ref[...])):
            k_ref[pl.ds(g, tm, stride=N_HEADS), :] = o
            kb_ref[:, heads[g]] = o.astype(BF16)

    @pl.when(j == 2)
    def _():
        for g in range(N_HEADS):
            v_ref[pl.ds(g, tm, stride=N_HEADS), :] = y[:, heads[g]]
        vb_ref[...] = y.astype(BF16)


def _proj_qkv_call(h, w_in, q_gain, k_gain, *, row0, rows, tm=512):
    d = h.shape[1]
    tm = min(tm, rows)
    assert rows % tm == 0 and row0 % tm == 0
    ib = row0 // tm
    wide = lambda: pl.BlockSpec((tm, WIDTH), lambda i, j: (i, 0))
    tall = lambda: pl.BlockSpec((tm * N_HEADS, HEAD_DIM), lambda i, j: (i, 0))
    gain = lambda g: pl.BlockSpec(g.shape, lambda i, j: (0, 0))
    return pl.pallas_call(
        _proj_qkv_kernel,
        grid=(rows // tm, 3),
        in_specs=[pl.BlockSpec((tm, d), lambda i, j: (ib + i, 0)),
                  pl.BlockSpec((d, WIDTH), lambda i, j: (0, j)), gain(q_gain), gain(k_gain)],
        out_specs=[wide(), tall(), wide(), tall(), wide()],
        out_shape=[jax.ShapeDtypeStruct((rows, WIDTH), BF16),
                   jax.ShapeDtypeStruct((rows * N_HEADS, HEAD_DIM), F32), jax.ShapeDtypeStruct((rows, WIDTH), BF16),
                   jax.ShapeDtypeStruct((rows * N_HEADS, HEAD_DIM), F32), jax.ShapeDtypeStruct((rows, WIDTH), BF16)],
        compiler_params=_cparams("arbitrary", "arbitrary"),
        name="proj_qkv",
    )(h, w_in, q_gain, k_gain)


def _prenorm_q_kernel(x_ref, g_ref, sc_ref, sh_ref, w_ref, qg_ref, h_ref, q_ref, wb_ref, h2_ref, *, nrow, extra):
    s = pl.program_id(0)

    @pl.when(s == 0)
    def _():
        wb_ref[...] = w_ref[...].astype(BF16)
        h2_ref[1] = jnp.zeros(h2_ref.shape[1:], BF16)

    y = jnp.dot(h2_ref[(s + 1) % 2], wb_ref[...], preferred_element_type=F32)
    for g, o in enumerate(_head_rms(y, qg_ref[...])):
        q_ref[:, g * HEAD_DIM:(g + 1) * HEAD_DIM] = o.astype(BF16)

    x = x_ref[0]
    ms = jnp.mean(x * x, axis=-1, keepdims=True)
    hn = (x * lax.rsqrt(ms + EPS) * g_ref[0] * (1.0 + sc_ref[0]) + sh_ref[0]).astype(BF16)
    h2_ref[s % 2] = hn
    h_ref[...] = jnp.where(s < nrow, hn, jnp.zeros_like(hn)) if extra else hn


def _prenorm_q_call(x, gain, scale, shift, w_in, q_gain, *, tm, extra_rows=0):
    n, t, d = x.shape
    assert n == 1 and t % tm == 0 and extra_rows % tm == 0
    nrow, extra = t // tm, extra_rows // tm
    steps = nrow + max(1, extra)
    this = lambda s: jnp.minimum(s, nrow - 1)
    prev = lambda s: jnp.minimum(jnp.maximum(s - 1, 0), nrow - 1)
    vec = pl.BlockSpec((1, 1, d), lambda s: (0, 0, 0))
    return pl.pallas_call(
        functools.partial(_prenorm_q_kernel, nrow=nrow, extra=extra),
        grid=(steps,),
        in_specs=[pl.BlockSpec((1, tm, d), lambda s: (0, this(s), 0)), vec, vec, vec,
                  pl.BlockSpec((d, WIDTH), lambda s: (0, 0), pipeline_mode=pl.Buffered(1)),
                  pl.BlockSpec(q_gain.shape, lambda s: (0, 0))],
        out_specs=[pl.BlockSpec((tm, d), lambda s: (jnp.minimum(s, nrow - 1 + extra), 0)),
                   pl.BlockSpec((tm, WIDTH), lambda s: (prev(s), 0))],
        out_shape=[jax.ShapeDtypeStruct((t + extra_rows, d), BF16), jax.ShapeDtypeStruct((t, WIDTH), BF16)],
        scratch_shapes=[pltpu.VMEM((d, WIDTH), BF16), pltpu.VMEM((2, tm, d), BF16)],
        compiler_params=_cparams("arbitrary"),
        name="prenorm_q",
    )(x, gain, scale, shift, w_in, q_gain)


def _suffix_matrix(bk):
    j = lax.broadcasted_iota(jnp.int32, (2 * bk, 2 * bk), 0) % bk
    s = lax.broadcasted_iota(jnp.int32, (2 * bk, 2 * bk), 1)
    return jnp.where((j > s) | (s >= bk), -1.0, 0.0).astype(BF16)


def _sb_tiles(qs, ks, vs, carries, sfx, masks=None, valid=None):
    bk = sfx.shape[0] // 2
    n = range(len(qs))
    spans = [range(ks[i].shape[0] // bk) for i in n]
    lanes = lambda x, t: x[:, t * bk:(t + 1) * bk]
    mask_of = lambda i, t: None if masks is None or masks[i] is None else masks[i][t]
    zs = [lax.dot_general(qs[i], ks[i], (((1,), (1,)), ((), ())), preferred_element_type=F32) * HEAD_DIM ** -0.5
          for i in n]
    sps = [jnp.maximum(z, 0.0) + jnp.log(1.0 + jnp.exp(-jnp.abs(z))) for z in zs]
    r2s = []
    for i in n:
        r2 = []
        for t in spans[i]:
            m = mask_of(i, t)
            l1m = lanes(sps[i], t) if m is None else jnp.where(m, lanes(sps[i], t), 0.0)
            hi = l1m.astype(BF16)
            lo = (l1m - hi.astype(F32)).astype(BF16)
            r2.append(jnp.dot(jnp.concatenate([hi, lo], axis=1), sfx, preferred_element_type=F32))
        r2s.append(r2)
    new, wss = [], []
    for i in n:
        c = carries[i]
        ws = []
        for t in spans[i]:
            w = jnp.exp(lanes(zs[i], t) - lanes(sps[i], t) + r2s[i][t][:, :bk] + c)
            m = mask_of(i, t)
            if m is not None:
                w = jnp.where(m, w, 0.0)
            if valid is not None and valid[i][t] is not None:
                w = jnp.where(valid[i][t], w, 0.0)
            ws.append(w.astype(BF16))
            c = c + r2s[i][t][:, bk:]
        new.append(c)
        wss.append(ws[0] if len(ws) == 1 else jnp.concatenate(ws, axis=1))
    pvs = [jnp.dot(wss[i], vs[i], preferred_element_type=F32) for i in n]
    return new, pvs


def _sb_prompt_kernel(q_ref, k_ref, v_ref, z_ref, o_ref, c_scr, acc_scr, *, n_groups, group, ahead):
    blk = SB_BLOCK
    qt = pl.program_id(1)
    sfx = _suffix_matrix(blk)
    row = lax.broadcasted_iota(jnp.int32, (blk, blk), 0)
    col = lax.broadcasted_iota(jnp.int32, (blk, blk), 1)
    causal = col < row
    alive = lambda cs: (functools.reduce(jnp.maximum, [jnp.max(c) for c in cs]) >= SB_LOG_CUTOFF).astype(jnp.int32)

    def kv(kb):
        start = pl.multiple_of(kb * blk, blk)
        return k_ref[pl.ds(start, blk), :], v_ref[pl.ds(start, blk), :]

    def qgroup(ig, _):
        gq0 = (qt * n_groups + ig) * group
        rows = [pl.ds(pl.multiple_of((ig * group + g) * blk, blk), blk) for g in range(group)]
        qs = [q_ref[r, :] for r in rows]
        kbs = [[gq0 + g - s for s in range(1 + ahead)] for g in range(group)]
        kvs = [[kv(jnp.maximum(kb, 0)) for kb in kbs[g]] for g in range(group)]
        cat = lambda xs: jnp.concatenate(xs, axis=0)
        cs, pvs = _sb_tiles(qs, [cat([k for k, _ in kvs[g]]) for g in range(group)],
                            [cat([v for _, v in kvs[g]]) for g in range(group)],
                            [jnp.zeros((blk, blk), F32)] * group, sfx,
                            masks=[[causal] + [None] * ahead] * group,
                            valid=[[None] + [kb >= 0 for kb in kbs[g][1:]] for g in range(group)])
        for g in range(group):
            c_scr[g] = cs[g]
            acc_scr[g] = pvs[g]

        def cond(st):
            s, go = st
            return jnp.logical_and(s <= gq0 + group - 1, go > 0)

        def body(st):
            s, _ = st
            kbs = [gq0 + g - s for g in range(group)]
            kvs = [kv(jnp.maximum(kb, 0)) for kb in kbs]
            cs, pvs = _sb_tiles(qs, [k for k, _ in kvs], [v for _, v in kvs],
                                [c_scr[g] for g in range(group)], sfx, valid=[[kb >= 0] for kb in kbs])
            for g in range(group):
                c_scr[g] = cs[g]
                acc_scr[g] += pvs[g]
            return s + 1, alive(cs)

        lax.while_loop(cond, body, (1 + ahead, alive(cs)))
        for g in range(group):
            o_ref[rows[g], :] = (acc_scr[g] * _silu(z_ref[rows[g], :])).astype(BF16)
        return 0

    lax.fori_loop(0, n_groups, qgroup, 0)


def _sb_prompt_call(q, k, v, z, *, tq=4096, group=8, ahead=2):
    t = q.shape[0]
    tq = min(tq, t)
    assert t % tq == 0 and tq % (SB_BLOCK * group) == 0
    qspec = pl.BlockSpec((tq, HEAD_DIM), lambda h, i: (i, h))
    kvspec = pl.BlockSpec((t, HEAD_DIM), lambda h, i: (0, h))
    return pl.pallas_call(
        functools.partial(_sb_prompt_kernel, n_groups=tq // (SB_BLOCK * group), group=group, ahead=ahead),
        grid=(N_HEADS, t // tq),
        in_specs=[qspec, kvspec, kvspec, qspec],
        out_specs=qspec,
        out_shape=jax.ShapeDtypeStruct((t, WIDTH), BF16),
        scratch_shapes=[pltpu.VMEM((group, SB_BLOCK, SB_BLOCK), F32), pltpu.VMEM((group, SB_BLOCK, HEAD_DIM), F32)],
        compiler_params=_cparams("arbitrary", "arbitrary"),
        name="sb_prompt",
    )(q, k, v, z)


def _sb_sample_kernel(q_ref, kn_ref, vn_ref, z_ref, kc_hbm, vc_hbm, o_ref, kbuf, vbuf, sem, c_scr, acc_scr, *, past):
    blk = SB_BLOCK
    nh = N_HEADS
    b = pl.program_id(0)
    tq = q_ref.shape[0]
    half = blk - tq
    n_full = (past - half) // blk
    rem = (past - half) % blk
    sfx = _suffix_matrix(blk)
    row = lax.broadcasted_iota(jnp.int32, (tq, blk), 0)
    col = lax.broadcasted_iota(jnp.int32, (tq, blk), 1)
    heads = [slice(h * HEAD_DIM, (h + 1) * HEAD_DIM) for h in range(nh)]
    alive = lambda cs: (functools.reduce(jnp.maximum, [jnp.max(c) for c in cs]) >= SB_LOG_CUTOFF).astype(jnp.int32)

    def copies(stream, key0, nkeys, slot):
        src = pl.ds(key0 * nh, nkeys * nh)
        dst = pl.ds(0, nkeys * nh)
        return (pltpu.make_async_copy(kc_hbm.at[stream, src, :], kbuf.at[slot, dst, :], sem.at[0, slot]),
                pltpu.make_async_copy(vc_hbm.at[stream, src, :], vbuf.at[slot, dst, :], sem.at[1, slot]))

    def start(cps):
        for cp in cps:
            cp.start()

    def wait(cps):
        for cp in cps:
            cp.wait()

    first = min(n_full, 1)
    base = 2 * (b % 2)

    def tile_copies(j):
        return copies(b, past - half - (j + 1) * blk, blk, base + (j + 1) % 2)

    def first_copies(stream):
        slot0 = 2 * (stream % 2)
        cps = copies(stream, past - half, half, slot0)
        return cps + copies(stream, past - half - blk, blk, slot0 + 1) if first else cps

    def cached(buf, slot, h, nkeys):
        return buf[slot, pl.ds(h, nkeys, stride=nh), :].astype(BF16)

    @pl.when(b == 0)
    def _():
        start(first_copies(b))

    @pl.when(b + 1 < pl.num_programs(0))
    def _():
        start(first_copies(b + 1))

    wait(first_copies(b))

    qs = [q_ref[:, heads[h]] for h in range(nh)]

    def span(buf, new_ref, h):
        tiles = [cached(buf, base, h, half), new_ref[:, heads[h]]] + [cached(buf, base + 1, h, blk)] * first
        return jnp.concatenate(tiles, axis=0)

    cs, pvs = _sb_tiles(qs, [span(kbuf, kn_ref, h) for h in range(nh)], [span(vbuf, vn_ref, h) for h in range(nh)],
                        [jnp.zeros((tq, blk), F32)] * nh, sfx, masks=[[col < row + half] + [None] * first] * nh)
    for h in range(nh):
        c_scr[h] = cs[h]
        acc_scr[h] = pvs[h]

    def sweep(slot, mask):
        cs, pvs = _sb_tiles(qs, [cached(kbuf, slot, h, blk) for h in range(nh)],
                            [cached(vbuf, slot, h, blk) for h in range(nh)],
                            [c_scr[h] for h in range(nh)], sfx, masks=None if mask is None else [[mask]] * nh)
        for h in range(nh):
            c_scr[h] = cs[h]
            acc_scr[h] += pvs[h]
        return cs

    def cond(st):
        j, go = st
        return jnp.logical_and(j < n_full, go > 0)

    def body(st):
        j, _ = st
        cps = tile_copies(j)
        start(cps)
        wait(cps)
        return j + 1, alive(sweep(base + (j + 1) % 2, None))

    _, go = lax.while_loop(cond, body, (first, alive(cs)))

    if rem:
        @pl.when(go > 0)
        def _():
            cps = copies(b, 0, blk, base)
            start(cps)
            wait(cps)
            sweep(base, col < rem)

    for h in range(nh):
        o_ref[:, heads[h]] = (acc_scr[h] * _silu(z_ref[:, heads[h]])).astype(BF16)


def _sb_sample_call(q, kn, vn, z, kc, vc, *, z_row0=0):
    nb = kc.shape[0]
    past = kc.shape[1] // N_HEADS
    tq = q.shape[0] // nb
    assert tq % 16 == 0 and tq < SB_BLOCK and past >= SB_BLOCK and z_row0 % tq == 0
    new = pl.BlockSpec((tq, WIDTH), lambda b: (b, 0))
    zspec = pl.BlockSpec((tq, WIDTH), lambda b: (z_row0 // tq + b, 0))
    hbm = pl.BlockSpec(memory_space=pl.ANY)
    return pl.pallas_call(
        functools.partial(_sb_sample_kernel, past=past),
        grid=(nb,),
        in_specs=[new, new, new, zspec, hbm, hbm],
        out_specs=new,
        out_shape=jax.ShapeDtypeStruct(q.shape, BF16),
        scratch_shapes=[pltpu.VMEM((4, SB_BLOCK * N_HEADS, HEAD_DIM), F32),
                        pltpu.VMEM((4, SB_BLOCK * N_HEADS, HEAD_DIM), F32),
                        pltpu.SemaphoreType.DMA((2, 4)),
                        pltpu.VMEM((N_HEADS, tq, SB_BLOCK), F32), pltpu.VMEM((N_HEADS, tq, HEAD_DIM), F32)],
        compiler_params=_cparams("arbitrary"),
        name="sb_sample",
    )(q, kn, vn, z, kc, vc)


def _prefix_matrix(c):
    t = lax.broadcasted_iota(jnp.int32, (2 * c, c), 0)
    s = lax.broadcasted_iota(jnp.int32, (2 * c, c), 1)
    incl = (t < c) & (s <= t)
    sub = (t >= c) & (s < ((t - c) // HG_SUB) * HG_SUB)
    return jnp.where(incl | sub, 1.0, 0.0).astype(BF16)


def _hgrn_front(lf, qh, kh, n_heads, pfx, tril):
    c = lf.shape[0]
    n_sub = c // HG_SUB
    heads = [slice(h * HEAD_DIM, (h + 1) * HEAD_DIM) for h in range(n_heads)]
    p0 = lf.astype(BF16)
    r1 = lf - p0.astype(F32)
    p1 = r1.astype(BF16)
    p2 = (r1 - p1.astype(F32)).astype(BF16)
    br = (jnp.dot(pfx, p0, preferred_element_type=F32) + jnp.dot(pfx, p1, preferred_element_type=F32)
          + jnp.dot(pfx, p2, preferred_element_type=F32))
    b = br[:c]
    r = br[c:]
    b_last = b[c - 1:c, :]
    q_sub = (qh * jnp.exp(b - r)).astype(BF16)
    q_dec = (qh * jnp.exp(b)).astype(BF16)
    k_end = (kh * jnp.exp(b_last - b)).astype(BF16)
    dec = jnp.exp(b_last)
    att = [[] for _ in heads]
    for i in range(n_sub):
        lo, hi = i * HG_SUB, (i + 1) * HG_SUB
        k_i = (kh[:hi] * jnp.exp(r[lo:lo + 1, :] - b[:hi])).astype(BF16)
        if hi < c:
            k_i = jnp.concatenate([k_i, jnp.zeros((c - hi, k_i.shape[1]), BF16)], axis=0)
        for h, hs in enumerate(heads):
            att[h].append(lax.dot_general(q_sub[lo:hi, hs], k_i[:, hs], (((1,), (1,)), ((), ())),
                                          preferred_element_type=F32))
    att = [jnp.where(tril, jnp.concatenate(a, axis=0), 0.0).astype(BF16) for a in att]
    return att, q_dec, k_end, dec


def _hgrn_back(front, v, sts):
    att, q_dec, k_end, dec = front
    heads = [slice(h * HEAD_DIM, (h + 1) * HEAD_DIM) for h in range(len(sts))]
    vb = v
    outs, new_sts = [], []
    for h, hs in enumerate(heads):
        o = jnp.dot(att[h], vb[:, hs], preferred_element_type=F32)
        o = o + lax.dot_general(q_dec[:, hs], sts[h].astype(BF16), (((1,), (1,)), ((), ())),
                                preferred_element_type=F32)
        outs.append(o)
    for h, hs in enumerate(heads):
        new_sts.append(sts[h] * dec[:, hs] + lax.dot_general(vb[:, hs], k_end[:, hs], (((0,), (0,)), ((), ())),
                                                             preferred_element_type=F32))
    return outs, new_sts


def _hgrn_kernel(lf_ref, qh_ref, kh_ref, v_ref, zh_ref, g_ref, s0_ref, o_ref, s_ref, st_scr, *, chunk, n_chunks, streams):
    tt = pl.program_id(2)
    nh = st_scr.shape[0] // streams
    tile = lf_ref.shape[0] // streams
    heads = [slice(h * HEAD_DIM, (h + 1) * HEAD_DIM) for h in range(nh)]

    @pl.when(tt == 0)
    def _():
        for s in range(streams):
            for h in range(nh):
                st_scr[s * nh + h] = s0_ref[s, h].T

    pfx = _prefix_matrix(chunk)
    ti = lax.broadcasted_iota(jnp.int32, (chunk, chunk), 0)
    si = lax.broadcasted_iota(jnp.int32, (chunk, chunk), 1)
    tril = si <= ti

    unroll = 2 if n_chunks % 2 == 0 else 1

    def step(ci, _):
        rss = [[pl.ds(pl.multiple_of(s * tile + (ci * unroll + u) * chunk, chunk), chunk) for u in range(unroll)]
               for s in range(streams)]
        fronts = [[_hgrn_front(lf_ref[rs, :], qh_ref[rs, :], kh_ref[rs, :], nh, pfx, tril) for rs in rss[s]]
                  for s in range(streams)]
        for s in range(streams):
            sts = [st_scr[s * nh + h] for h in range(nh)]
            for rs, front in zip(rss[s], fronts[s]):
                outs, sts = _hgrn_back(front, v_ref[rs, :], sts)
                for h in range(nh):
                    o = outs[h]
                    ms = jnp.mean(o * o, axis=-1, keepdims=True)
                    o_ref[rs, heads[h]] = (o * lax.rsqrt(ms + EPS) * g_ref[h]
                                           * _silu(zh_ref[rs, heads[h]])).astype(BF16)
            for h in range(nh):
                st_scr[s * nh + h] = sts[h]
        return 0

    lax.fori_loop(0, n_chunks // unroll, step, 0)

    @pl.when(tt == pl.num_programs(2) - 1)
    def _():
        for s in range(streams):
            for h in range(nh):
                s_ref[s, h] = st_scr[s * nh + h].T


def _hgrn_call(lf, qh, kh, v, zh, gain, s0, *, t, chunk, tile, heads, row0=0, streams=1):
    nb = s0.shape[0]
    tile = min(tile, t)
    rows = streams * tile
    assert t % tile == 0 and tile % chunk == 0 and chunk % HG_SUB == 0 and N_HEADS % heads == 0 and row0 % rows == 0
    assert nb % streams == 0 and (streams == 1 or tile == t)
    nt = t // tile
    tok_in = pl.BlockSpec((rows, heads * HEAD_DIM), lambda b, h, i: (row0 // rows + b * nt + i, h))
    tok = pl.BlockSpec((rows, heads * HEAD_DIM), lambda b, h, i: (b * nt + i, h))
    state = pl.BlockSpec((streams, heads, HEAD_DIM, HEAD_DIM), lambda b, h, i: (b, h, 0, 0))
    return pl.pallas_call(
        functools.partial(_hgrn_kernel, chunk=chunk, n_chunks=tile // chunk, streams=streams),
        grid=(nb // streams, N_HEADS // heads, nt),
        in_specs=[tok_in] * 5 + [pl.BlockSpec((heads, 1, HEAD_DIM), lambda b, h, i: (h, 0, 0)), state],
        out_specs=[tok, state],
        out_shape=[jax.ShapeDtypeStruct((nb * t, WIDTH), BF16), jax.ShapeDtypeStruct(s0.shape, F32)],
        scratch_shapes=[pltpu.VMEM((streams * heads, HEAD_DIM, HEAD_DIM), F32)],
        compiler_params=_cparams("arbitrary", "arbitrary", "arbitrary"),
        name="hgrn2",
    )(lf, qh, kh, v, zh, gain, s0)


def _out_kernel(gs_ref, gh_ref, gsb_ref, ghg_ref, x_ref, gate_ref, wsb_ref, whg_ref, wo_ref, y_ref):
    nb, tr, d = x_ref.shape
    y_sb = jnp.dot(gs_ref[...], wsb_ref[...], preferred_element_type=F32)
    y_h = jnp.dot(gh_ref[...], whg_ref[...], preferred_element_type=F32)
    merged = jax.nn.sigmoid(gsb_ref[...]) * y_sb + jax.nn.sigmoid(ghg_ref[...]) * y_h
    upd = jnp.dot(merged.astype(BF16), wo_ref[...], preferred_element_type=F32)
    y_ref[...] = x_ref[...] + gate_ref[...] * upd.reshape(nb, tr, d)


def _out_call(gs, gh, gg, x, gate, wsb, whg, wo, nb, tr, *, gg_row0=0):
    n, t, d = x.shape
    tm = nb * tr
    nt = t // tr
    assert gg_row0 % tm == 0 and (nb == 1 or nt == 1)
    rowblk = lambda w, c, r0=0: pl.BlockSpec((tm, w), lambda i, j: (r0 // tm + i * nt + j, c))
    const = lambda a: pl.BlockSpec(a.shape, lambda i, j: (0, 0), pipeline_mode=pl.Buffered(1))
    return pl.pallas_call(
        _out_kernel,
        grid=(n // nb, nt),
        in_specs=[rowblk(WIDTH, 0), rowblk(WIDTH, 0), rowblk(d, 0, gg_row0), rowblk(d, 1, gg_row0),
                  pl.BlockSpec((nb, tr, d), lambda i, j: (i, j, 0)),
                  pl.BlockSpec((nb, 1, d), lambda i, j: (i, 0, 0)),
                  const(wsb), const(whg), const(wo)],
        out_specs=pl.BlockSpec((nb, tr, d), lambda i, j: (i, j, 0)),
        out_shape=jax.ShapeDtypeStruct(x.shape, F32),
        compiler_params=_cparams("arbitrary", "arbitrary"),
        name="merge_out",
    )(gs, gh, gg, gg, x, gate, wsb, whg, wo)


def _layer(x_p, x_s, mod_p, mod_s, p, s0_s, caches):
    n_p, t_p, d = x_p.shape
    n_s, t_s, _ = x_s.shape
    assert n_p == 1
    rows_p, rows_s = t_p, n_s * t_s
    tm = math.gcd(1024, rows_p, rows_s)
    w_in = p["w_in"]

    h, q_p = _prenorm_q_call(x_p, p["norm_gain"], mod_p[1], mod_p[0], w_in, p["q_gain"], tm=tm, extra_rows=rows_s)
    pn_tr = min(1024, t_s)
    pn_nb = max(1, min(n_s, 1024 // pn_tr))
    h = _prenorm_call(x_s, p["norm_gain"], mod_s[1], mod_s[0], pn_nb, pn_tr, h, rows_p)

    proj = functools.partial(_proj_call, h, w_in, tm=tm)
    prompt, sample = dict(row0=0, rows=rows_p), dict(row0=rows_p, rows=rows_s)
    q_s, k_s, kb_s, v_s, vb_s = _proj_qkv_call(h, w_in, p["q_gain"], p["k_gain"], **sample)
    k_p, kb_p = proj(1 * WIDTH, WIDTH, "norm_k", (p["k_gain"],), **prompt)
    v_p, vb_p = proj(2 * WIDTH, WIDTH, "copy2", **prompt)
    (z_sb,) = proj(3 * WIDTH, WIDTH, "plain")
    logf, k_h = proj(4 * WIDTH, WIDTH, "forget", (p["lb_raw"],))
    (i_h,) = proj(5 * WIDTH, WIDTH, "plain_bf16")
    (q_h,) = proj(6 * WIDTH, WIDTH, "silu")
    (z_h,) = proj(7 * WIDTH, WIDTH, "plain")
    (gg,) = proj(8 * WIDTH, 2 * d, "plain")

    gs_p = _sb_prompt_call(q_p, kb_p, vb_p, z_sb)
    gs_s = _sb_sample_call(q_s, kb_s, vb_s, z_sb, caches[0], caches[1], z_row0=rows_p)
    hgrn = functools.partial(_hgrn_call, logf, q_h, k_h, i_h, z_h, p["onorm_gain"])
    gh_p, s_p = hgrn(jnp.zeros((n_p, N_HEADS, HEAD_DIM, HEAD_DIM), F32), t=t_p, chunk=min(128, t_p), tile=1024,
                     heads=N_HEADS)
    gh_s, s_s = hgrn(s0_s, t=t_s, chunk=t_s, tile=t_s, heads=N_HEADS, row0=rows_p, streams=math.gcd(n_s, 4))
    out = functools.partial(_out_call, wsb=p["w_br_sb"], whg=p["w_br_hg"], wo=p["w_out"])
    y_p = out(gs_p, gh_p, gg, x_p, mod_p[2], nb=1, tr=min(256, t_p))
    y_s = out(gs_s, gh_s, gg, x_s, mod_s[2], nb=max(1, min(n_s, 256 // t_s)), tr=t_s, gg_row0=rows_p)
    heads5 = lambda a, n, t: a.reshape(1, n, t, N_HEADS, HEAD_DIM)
    return (y_p, y_s, heads5(k_p, n_p, t_p), heads5(v_p, n_p, t_p), s_p[None],
            heads5(k_s, n_s, t_s), heads5(v_s, n_s, t_s), s_s[None])


def kernel(x_prompt, x_sample, cache_sb_k, cache_sb_v, state_hgrn, c_prompt, c_sample, norm_gain, w_ada, b_ada, w_in, q_norm_gain, k_norm_gain, hgrn_lb_raw, hgrn_onorm_gain, w_branch_sb, w_branch_hgrn, w_out):
    assert w_in.shape[0] == 1, "single-layer trunk"
    n_p, t_p, d = x_prompt.shape
    n_s, t_s, _ = x_sample.shape
    past = cache_sb_k.shape[2]

    c_all = jnp.concatenate([c_prompt, c_sample], axis=0)
    pad = (-c_all.shape[0]) % 8
    c_all = jnp.pad(c_all, ((0, pad), (0, 0)))
    mod = _ada_call(c_all, w_ada[0], b_ada[0].reshape(1, 3 * d))
    mods = lambda lo, hi: tuple(mod[lo:hi, i * d:(i + 1) * d].reshape(hi - lo, 1, d) for i in range(3))

    p = {
        "norm_gain": norm_gain[0].reshape(1, 1, d),
        "w_in": w_in[0],
        "q_gain": q_norm_gain[0].reshape(1, HEAD_DIM),
        "k_gain": k_norm_gain[0].reshape(1, HEAD_DIM),
        "lb_raw": hgrn_lb_raw,
        "onorm_gain": hgrn_onorm_gain[0].reshape(N_HEADS, 1, HEAD_DIM),
        "w_br_sb": w_branch_sb[0].astype(BF16),
        "w_br_hg": w_branch_hgrn[0].astype(BF16),
        "w_out": w_out[0].astype(BF16),
    }

    return _layer(x_prompt, x_sample, mods(0, n_p), mods(n_p, n_p + n_s), p, state_hgrn[0],
                  (cache_sb_k.reshape(n_s, past * N_HEADS, HEAD_DIM), cache_sb_v.reshape(n_s, past * N_HEADS, HEAD_DIM)))
```

```python
import functools
import math

import jax
import jax.numpy as jnp
from jax import lax
from jax.experimental import pallas as pl
from jax.experimental.pallas import tpu as pltpu

F32 = jnp.float32
BF16 = jnp.bfloat16

N_HEADS = 8
HEAD_DIM = 128
WIDTH = N_HEADS * HEAD_DIM
HG_SUB = 16
EPS = 1e-6
SB_BLOCK = 128
SB_LOG_CUTOFF = -88.0
VMEM_LIMIT = 56 * 1024 * 1024


def _cparams(*sem):
    return pltpu.CompilerParams(dimension_semantics=sem, vmem_limit_bytes=VMEM_LIMIT)


def _silu(x):
    return x * jax.nn.sigmoid(x)


def _ada_kernel(c_ref, w_ref, b_ref, o_ref):
    c = c_ref[...]
    a = _silu(c).astype(BF16)
    o_ref[...] = jnp.dot(a, w_ref[...].astype(BF16), preferred_element_type=F32) + b_ref[...]


def _ada_call(c, w, b):
    r, d = c.shape
    n = w.shape[1]
    tn = 1024
    return pl.pallas_call(
        _ada_kernel,
        grid=(n // tn,),
        in_specs=[pl.BlockSpec((r, d), lambda j: (0, 0)),
                  pl.BlockSpec((d, tn), lambda j: (0, j)),
                  pl.BlockSpec((1, tn), lambda j: (0, j))],
        out_specs=pl.BlockSpec((r, tn), lambda j: (0, j)),
        out_shape=jax.ShapeDtypeStruct((r, n), F32),
        compiler_params=_cparams("arbitrary"),
        name="ada_mod",
    )(c, w, b)


def _prenorm_kernel(x_ref, g_ref, sc_ref, sh_ref, hall_ref, h_ref):
    del hall_ref
    x = x_ref[...]
    ms = jnp.mean(x * x, axis=-1, keepdims=True)
    xn = x * lax.rsqrt(ms + EPS)
    h = xn * g_ref[...] * (1.0 + sc_ref[...]) + sh_ref[...]
    h_ref[...] = h.astype(BF16).reshape(h_ref.shape)


def _prenorm_call(x, gain, scale, shift, nb, tr, h_all, row0):
    n, t, d = x.shape
    tm = nb * tr
    assert row0 % tm == 0 and (nb == 1 or tr == t)
    vec = pl.BlockSpec((nb, 1, d), lambda i, j: (i, 0, 0))
    return pl.pallas_call(
        _prenorm_kernel,
        grid=(n // nb, t // tr),
        in_specs=[pl.BlockSpec((nb, tr, d), lambda i, j: (i, j, 0)),
                  pl.BlockSpec((1, 1, d), lambda i, j: (0, 0, 0)), vec, vec,
                  pl.BlockSpec(memory_space=pl.ANY)],
        out_specs=pl.BlockSpec((tm, d), lambda i, j: (row0 // tm + i * (t // tr) + j, 0)),
        out_shape=jax.ShapeDtypeStruct(h_all.shape, BF16),
        input_output_aliases={4: 0},
        compiler_params=_cparams("arbitrary", "arbitrary"),
        name="prenorm",
    )(x, gain, scale, shift, h_all)


def _head_rms(y, gain):
    outs = []
    for g in range(N_HEADS):
        yh = y[:, g * HEAD_DIM:(g + 1) * HEAD_DIM]
        ms = jnp.mean(yh * yh, axis=-1, keepdims=True)
        outs.append(yh * lax.rsqrt(ms + EPS) * gain)
    return outs


def _proj_kernel(*refs, kind):
    h_ref, w_ref = refs[0], refs[1]
    wb_ref = refs[-1]

    @pl.when(pl.program_id(1) == 0)
    def _():
        wb_ref[...] = w_ref[...].astype(BF16)

    y = jnp.dot(h_ref[...], wb_ref[...], preferred_element_type=F32)
    tm = y.shape[0]
    if kind == "plain":
        refs[2][...] = y
    elif kind == "plain_bf16":
        refs[2][...] = y.astype(BF16)
    elif kind == "silu":
        refs[2][...] = _silu(y)
    elif kind == "copy2":
        for g in range(N_HEADS):
            refs[2][pl.ds(g, tm, stride=N_HEADS), :] = y[:, g * HEAD_DIM:(g + 1) * HEAD_DIM]
        refs[3][...] = y.astype(BF16)
    elif kind == "norm_k":
        gain = refs[2][...]
        for g, o in enumerate(_head_rms(y, gain)):
            refs[3][pl.ds(g, tm, stride=N_HEADS), :] = o
            refs[4][:, g * HEAD_DIM:(g + 1) * HEAD_DIM] = o.astype(BF16)
    elif kind == "forget":
        raw = refs[2][...]
        e = jnp.exp(raw - jnp.max(raw, axis=0, keepdims=True))
        lb = e[0:1, :] / jnp.sum(e, axis=0, keepdims=True)
        f = lb + (1.0 - lb) * jax.nn.sigmoid(y)
        refs[3][...] = jnp.log(f)
        refs[4][...] = 1.0 - f
    else:
        raise ValueError(kind)


def _proj_call(h, w_in, col0, ncols, kind, extra=(), *, tm=512, tn=1024, row0=0, rows=None):
    d = h.shape[1]
    rows = h.shape[0] - row0 if rows is None else rows
    assert col0 % tn == 0 and ncols % tn == 0 and rows % tm == 0 and row0 % tm == 0
    jb, ib = col0 // tn, row0 // tm
    grid = (ncols // tn, rows // tm)
    tile = lambda: pl.BlockSpec((tm, tn), lambda j, i: (i, j))
    in_specs = [pl.BlockSpec((tm, d), lambda j, i: (ib + i, 0)),
                pl.BlockSpec((d, tn), lambda j, i: (0, jb + j))]
    for e in extra:
        in_specs.append(pl.BlockSpec(e.shape, lambda j, i: (0, 0)))
    if kind in ("plain", "silu"):
        out_dt = (F32,)
    elif kind == "plain_bf16":
        out_dt = (BF16,)
    elif kind in ("copy2", "norm_k"):
        out_dt = (F32, BF16)
    else:
        out_dt = (F32, F32)
    out_specs = [tile() for _ in out_dt]
    out_shape = [jax.ShapeDtypeStruct((rows, ncols), dt) for dt in out_dt]
    if kind in ("copy2", "norm_k"):
        assert ncols == WIDTH
        out_specs[0] = pl.BlockSpec((tm * N_HEADS, HEAD_DIM), lambda j, i: (i, 0))
        out_shape[0] = jax.ShapeDtypeStruct((rows * N_HEADS, HEAD_DIM), F32)
    outs = pl.pallas_call(
        functools.partial(_proj_kernel, kind=kind),
        grid=grid,
        in_specs=in_specs,
        out_specs=out_specs,
        out_shape=out_shape,
        scratch_shapes=[pltpu.VMEM((d, tn), BF16)],
        compiler_params=_cparams("arbitrary", "arbitrary"),
        name="proj_" + kind,
    )(h, w_in, *extra)
    return outs


def _proj_qkv_kernel(h_ref, w_ref, qg_ref, kg_ref, q_ref, k_ref, kb_ref, v_ref, vb_ref):
    j = pl.program_id(1)
    y = jnp.dot(h_ref[...], w_ref[...].astype(BF16), preferred_element_type=F32)
    tm = y.shape[0]
    heads = [slice(g * HEAD_DIM, (g + 1) * HEAD_DIM) for g in range(N_HEADS)]

    @pl.when(j == 0)
    def _():
        for g, o in enumerate(_head_rms(y, qg_ref[...])):
            q_ref[:, heads[g]] = o.astype(BF16)

    @pl.when(j == 1)
    def _():
        for g, o in enumerate(_head_rms(y, kg_ref[...])):
            k_ref[pl.ds(g, tm, stride=N_HEADS), :] = o
            kb_ref[:, heads[g]] = o.astype(BF16)

    @pl.when(j == 2)
    def _():
        for g in range(N_HEADS):
            v_ref[pl.ds(g, tm, stride=N_HEADS), :] = y[:, heads[g]]
        vb_ref[...] = y.astype(BF16)


def _proj_qkv_call(h, w_in, q_gain, k_gain, *, row0, rows, tm=512):
    d = h.shape[1]
    tm = min(tm, rows)
    assert rows % tm == 0 and row0 % tm == 0
    ib = row0 // tm
    wide = lambda: pl.BlockSpec((tm, WIDTH), lambda i, j: (i, 0))
    tall = lambda: pl.BlockSpec((tm * N_HEADS, HEAD_DIM), lambda i, j: (i, 0))
    gain = lambda g: pl.BlockSpec(g.shape, lambda i, j: (0, 0))
    return pl.pallas_call(
        _proj_qkv_kernel,
        grid=(rows // tm, 3),
        in_specs=[pl.BlockSpec((tm, d), lambda i, j: (ib + i, 0)),
                  pl.BlockSpec((d, WIDTH), lambda i, j: (0, j)), gain(q_gain), gain(k_gain)],
        out_specs=[wide(), tall(), wide(), tall(), wide()],
        out_shape=[jax.ShapeDtypeStruct((rows, WIDTH), BF16),
                   jax.ShapeDtypeStruct((rows * N_HEADS, HEAD_DIM), F32), jax.ShapeDtypeStruct((rows, WIDTH), BF16),
                   jax.ShapeDtypeStruct((rows * N_HEADS, HEAD_DIM), F32), jax.ShapeDtypeStruct((rows, WIDTH), BF16)],
        compiler_params=_cparams("arbitrary", "arbitrary"),
        name="proj_qkv",
    )(h, w_in, q_gain, k_gain)


def _prenorm_q_kernel(x_ref, g_ref, sc_ref, sh_ref, w_ref, qg_ref, h_ref, q_ref, wb_ref, h2_ref, *, nrow, extra):
    s = pl.program_id(0)

    @pl.when(s == 0)
    def _():
        wb_ref[...] = w_ref[...].astype(BF16)
        h2_ref[1] = jnp.zeros(h2_ref.shape[1:], BF16)

    y = jnp.dot(h2_ref[(s + 1) % 2], wb_ref[...], preferred_element_type=F32)
    for g, o in enumerate(_head_rms(y, qg_ref[...])):
        q_ref[:, g * HEAD_DIM:(g + 1) * HEAD_DIM] = o.astype(BF16)

    x = x_ref[0]
    ms = jnp.mean(x * x, axis=-1, keepdims=True)
    hn = (x * lax.rsqrt(ms + EPS) * g_ref[0] * (1.0 + sc_ref[0]) + sh_ref[0]).astype(BF16)
    h2_ref[s % 2] = hn
    h_ref[...] = jnp.where(s < nrow, hn, jnp.zeros_like(hn)) if extra else hn


def _prenorm_q_call(x, gain, scale, shift, w_in, q_gain, *, tm, extra_rows=0):
    n, t, d = x.shape
    assert n == 1 and t % tm == 0 and extra_rows % tm == 0
    nrow, extra = t // tm, extra_rows // tm
    steps = nrow + max(1, extra)
    this = lambda s: jnp.minimum(s, nrow - 1)
    prev = lambda s: jnp.minimum(jnp.maximum(s - 1, 0), nrow - 1)
    vec = pl.BlockSpec((1, 1, d), lambda s: (0, 0, 0))
    return pl.pallas_call(
        functools.partial(_prenorm_q_kernel, nrow=nrow, extra=extra),
        grid=(steps,),
        in_specs=[pl.BlockSpec((1, tm, d), lambda s: (0, this(s), 0)), vec, vec, vec,
                  pl.BlockSpec((d, WIDTH), lambda s: (0, 0), pipeline_mode=pl.Buffered(1)),
                  pl.BlockSpec(q_gain.shape, lambda s: (0, 0))],
        out_specs=[pl.BlockSpec((tm, d), lambda s: (jnp.minimum(s, nrow - 1 + extra), 0)),
                   pl.BlockSpec((tm, WIDTH), lambda s: (prev(s), 0))],
        out_shape=[jax.ShapeDtypeStruct((t + extra_rows, d), BF16), jax.ShapeDtypeStruct((t, WIDTH), BF16)],
        scratch_shapes=[pltpu.VMEM((d, WIDTH), BF16), pltpu.VMEM((2, tm, d), BF16)],
        compiler_params=_cparams("arbitrary"),
        name="prenorm_q",
    )(x, gain, scale, shift, w_in, q_gain)


def _suffix_matrix(bk):
    j = lax.broadcasted_iota(jnp.int32, (2 * bk, 2 * bk), 0) % bk
    s = lax.broadcasted_iota(jnp.int32, (2 * bk, 2 * bk), 1)
    return jnp.where((j > s) | (s >= bk), -1.0, 0.0).astype(BF16)


def _sb_tiles(qs, ks, vs, carries, sfx, masks=None, valid=None):
    bk = sfx.shape[0] // 2
    n = range(len(qs))
    spans = [range(ks[i].shape[0] // bk) for i in n]
    lanes = lambda x, t: x[:, t * bk:(t + 1) * bk]
    mask_of = lambda i, t: None if masks is None or masks[i] is None else masks[i][t]
    zs = [lax.dot_general(qs[i], ks[i], (((1,), (1,)), ((), ())), preferred_element_type=F32) * HEAD_DIM ** -0.5
          for i in n]
    sps = [jnp.maximum(z, 0.0) + jnp.log(1.0 + jnp.exp(-jnp.abs(z))) for z in zs]
    r2s = []
    for i in n:
        r2 = []
        for t in spans[i]:
            m = mask_of(i, t)
            l1m = lanes(sps[i], t) if m is None else jnp.where(m, lanes(sps[i], t), 0.0)
            hi = l1m.astype(BF16)
            lo = (l1m - hi.astype(F32)).astype(BF16)
            r2.append(jnp.dot(jnp.concatenate([hi, lo], axis=1), sfx, preferred_element_type=F32))
        r2s.append(r2)
    new, wss = [], []
    for i in n:
        c = carries[i]
        ws = []
        for t in spans[i]:
            w = jnp.exp(lanes(zs[i], t) - lanes(sps[i], t) + r2s[i][t][:, :bk] + c)
            m = mask_of(i, t)
            if m is not None:
                w = jnp.where(m, w, 0.0)
            if valid is not None and valid[i][t] is not None:
                w = jnp.where(valid[i][t], w, 0.0)
            ws.append(w.astype(BF16))
            c = c + r2s[i][t][:, bk:]
        new.append(c)
        wss.append(ws[0] if len(ws) == 1 else jnp.concatenate(ws, axis=1))
    pvs = [jnp.dot(wss[i], vs[i], preferred_element_type=F32) for i in n]
    return new, pvs


def _sb_prompt_kernel(q_ref, k_ref, v_ref, z_ref, o_ref, c_scr, acc_scr, *, n_groups, group, ahead):
    blk = SB_BLOCK
    qt = pl.program_id(1)
    sfx = _suffix_matrix(blk)
    row = lax.broadcasted_iota(jnp.int32, (blk, blk), 0)
    col = lax.broadcasted_iota(jnp.int32, (blk, blk), 1)
    causal = col < row
    alive = lambda cs: (functools.reduce(jnp.maximum, [jnp.max(c) for c in cs]) >= SB_LOG_CUTOFF).astype(jnp.int32)

    def kv(kb):
        start = pl.multiple_of(kb * blk, blk)
        return k_ref[pl.ds(start, blk), :], v_ref[pl.ds(start, blk), :]

    def qgroup(ig, _):
        gq0 = (qt * n_groups + ig) * group
        rows = [pl.ds(pl.multiple_of((ig * group + g) * blk, blk), blk) for g in range(group)]
        qs = [q_ref[r, :] for r in rows]
        kbs = [[gq0 + g - s for s in range(1 + ahead)] for g in range(group)]
        kvs = [[kv(jnp.maximum(kb, 0)) for kb in kbs[g]] for g in range(group)]
        cat = lambda xs: jnp.concatenate(xs, axis=0)
        cs, pvs = _sb_tiles(qs, [cat([k for k, _ in kvs[g]]) for g in range(group)],
                            [cat([v for _, v in kvs[g]]) for g in range(group)],
                            [jnp.zeros((blk, blk), F32)] * group, sfx,
                            masks=[[causal] + [None] * ahead] * group,
                            valid=[[None] + [kb >= 0 for kb in kbs[g][1:]] for g in range(group)])
        for g in range(group):
            c_scr[g] = cs[g]
            acc_scr[g] = pvs[g]

        def cond(st):
            s, go = st
            return jnp.logical_and(s <= gq0 + group - 1, go > 0)

        def body(st):
            s, _ = st
            kbs = [gq0 + g - s for g in range(group)]
            kvs = [kv(jnp.maximum(kb, 0)) for kb in kbs]
            cs, pvs = _sb_tiles(qs, [k for k, _ in kvs], [v for _, v in kvs],
                                [c_scr[g] for g in range(group)], sfx, valid=[[kb >= 0] for kb in kbs])
            for g in range(group):
                c_scr[g] = cs[g]
                acc_scr[g] += pvs[g]
            return s + 1, alive(cs)

        lax.while_loop(cond, body, (1 + ahead, alive(cs)))
        for g in range(group):
            o_ref[rows[g], :] = (acc_scr[g] * _silu(z_ref[rows[g], :])).astype(BF16)
        return 0

    lax.fori_loop(0, n_groups, qgroup, 0)


def _sb_prompt_call(q, k, v, z, *, tq=4096, group=16, ahead=2):
    t = q.shape[0]
    tq = min(tq, t)
    group = math.gcd(group, tq // SB_BLOCK)
    assert t % tq == 0 and tq % (SB_BLOCK * group) == 0
    qspec = pl.BlockSpec((tq, HEAD_DIM), lambda h, i: (i, h))
    kvspec = pl.BlockSpec((t, HEAD_DIM), lambda h, i: (0, h))
    return pl.pallas_call(
        functools.partial(_sb_prompt_kernel, n_groups=tq // (SB_BLOCK * group), group=group, ahead=ahead),
        grid=(N_HEADS, t // tq),
        in_specs=[qspec, kvspec, kvspec, qspec],
        out_specs=qspec,
        out_shape=jax.ShapeDtypeStruct((t, WIDTH), BF16),
        scratch_shapes=[pltpu.VMEM((group, SB_BLOCK, SB_BLOCK), F32), pltpu.VMEM((group, SB_BLOCK, HEAD_DIM), F32)],
        compiler_params=_cparams("arbitrary", "arbitrary"),
        name="sb_prompt",
    )(q, k, v, z)


def _sb_sample_kernel(q_ref, kn_ref, vn_ref, z_ref, kc_hbm, vc_hbm, o_ref, kbuf, vbuf, sem, c_scr, acc_scr, *, past):
    blk = SB_BLOCK
    nh = N_HEADS
    b = pl.program_id(0)
    tq = q_ref.shape[0]
    half = blk - tq
    n_full = (past - half) // blk
    rem = (past - half) % blk
    sfx = _suffix_matrix(blk)
    row = lax.broadcasted_iota(jnp.int32, (tq, blk), 0)
    col = lax.broadcasted_iota(jnp.int32, (tq, blk), 1)
    heads = [slice(h * HEAD_DIM, (h + 1) * HEAD_DIM) for h in range(nh)]
    alive = lambda cs: (functools.reduce(jnp.maximum, [jnp.max(c) for c in cs]) >= SB_LOG_CUTOFF).astype(jnp.int32)

    def copies(stream, key0, nkeys, slot):
        src = pl.ds(key0 * nh, nkeys * nh)
        dst = pl.ds(0, nkeys * nh)
        return (pltpu.make_async_copy(kc_hbm.at[stream, src, :], kbuf.at[slot, dst, :], sem.at[0, slot]),
                pltpu.make_async_copy(vc_hbm.at[stream, src, :], vbuf.at[slot, dst, :], sem.at[1, slot]))

    def start(cps):
        for cp in cps:
            cp.start()

    def wait(cps):
        for cp in cps:
            cp.wait()

    first = min(n_full, 1)
    base = 2 * (b % 2)

    def tile_copies(j):
        return copies(b, past - half - (j + 1) * blk, blk, base + (j + 1) % 2)

    def first_copies(stream):
        slot0 = 2 * (stream % 2)
        cps = copies(stream, past - half, half, slot0)
        return cps + copies(stream, past - half - blk, blk, slot0 + 1) if first else cps

    def cached(buf, slot, h, nkeys):
        return buf[slot, pl.ds(h, nkeys, stride=nh), :].astype(BF16)

    @pl.when(b == 0)
    def _():
        start(first_copies(b))

    @pl.when(b + 1 < pl.num_programs(0))
    def _():
        start(first_copies(b + 1))

    wait(first_copies(b))

    qs = [q_ref[:, heads[h]] for h in range(nh)]

    def span(buf, new_ref, h):
        tiles = [cached(buf, base, h, half), new_ref[:, heads[h]]] + [cached(buf, base + 1, h, blk)] * first
        return jnp.concatenate(tiles, axis=0)

    cs, pvs = _sb_tiles(qs, [span(kbuf, kn_ref, h) for h in range(nh)], [span(vbuf, vn_ref, h) for h in range(nh)],
                        [jnp.zeros((tq, blk), F32)] * nh, sfx, masks=[[col < row + half] + [None] * first] * nh)
    for h in range(nh):
        c_scr[h] = cs[h]
        acc_scr[h] = pvs[h]

    def sweep(slot, mask):
        cs, pvs = _sb_tiles(qs, [cached(kbuf, slot, h, blk) for h in range(nh)],
                            [cached(vbuf, slot, h, blk) for h in range(nh)],
                            [c_scr[h] for h in range(nh)], sfx, masks=None if mask is None else [[mask]] * nh)
        for h in range(nh):
            c_scr[h] = cs[h]
            acc_scr[h] += pvs[h]
        return cs

    def cond(st):
        j, go = st
        return jnp.logical_and(j < n_full, go > 0)

    def body(st):
        j, _ = st
        cps = tile_copies(j)
        start(cps)
        wait(cps)
        return j + 1, alive(sweep(base + (j + 1) % 2, None))

    _, go = lax.while_loop(cond, body, (first, alive(cs)))

    if rem:
        @pl.when(go > 0)
        def _():
            cps = copies(b, 0, blk, base)
            start(cps)
            wait(cps)
            sweep(base, col < rem)

    for h in range(nh):
        o_ref[:, heads[h]] = (acc_scr[h] * _silu(z_ref[:, heads[h]])).astype(BF16)


def _sb_sample_call(q, kn, vn, z, kc, vc, *, z_row0=0):
    nb = kc.shape[0]
    past = kc.shape[1] // N_HEADS
    tq = q.shape[0] // nb
    assert tq % 16 == 0 and tq < SB_BLOCK and past >= SB_BLOCK and z_row0 % tq == 0
    new = pl.BlockSpec((tq, WIDTH), lambda b: (b, 0))
    zspec = pl.BlockSpec((tq, WIDTH), lambda b: (z_row0 // tq + b, 0))
    hbm = pl.BlockSpec(memory_space=pl.ANY)
    return pl.pallas_call(
        functools.partial(_sb_sample_kernel, past=past),
        grid=(nb,),
        in_specs=[new, new, new, zspec, hbm, hbm],
        out_specs=new,
        out_shape=jax.ShapeDtypeStruct(q.shape, BF16),
        scratch_shapes=[pltpu.VMEM((4, SB_BLOCK * N_HEADS, HEAD_DIM), F32),
                        pltpu.VMEM((4, SB_BLOCK * N_HEADS, HEAD_DIM), F32),
                        pltpu.SemaphoreType.DMA((2, 4)),
                        pltpu.VMEM((N_HEADS, tq, SB_BLOCK), F32), pltpu.VMEM((N_HEADS, tq, HEAD_DIM), F32)],
        compiler_params=_cparams("arbitrary"),
        name="sb_sample",
    )(q, kn, vn, z, kc, vc)


def _prefix_matrix(c):
    t = lax.broadcasted_iota(jnp.int32, (2 * c, c), 0)
    s = lax.broadcasted_iota(jnp.int32, (2 * c, c), 1)
    incl = (t < c) & (s <= t)
    sub = (t >= c) & (s < ((t - c) // HG_SUB) * HG_SUB)
    return jnp.where(incl | sub, 1.0, 0.0).astype(BF16)


def _hgrn_front(lf, qh, kh, n_heads, pfx, tril):
    c = lf.shape[0]
    n_sub = c // HG_SUB
    heads = [slice(h * HEAD_DIM, (h + 1) * HEAD_DIM) for h in range(n_heads)]
    p0 = lf.astype(BF16)
    r1 = lf - p0.astype(F32)
    p1 = r1.astype(BF16)
    p2 = (r1 - p1.astype(F32)).astype(BF16)
    br = (jnp.dot(pfx, p0, preferred_element_type=F32) + jnp.dot(pfx, p1, preferred_element_type=F32)
          + jnp.dot(pfx, p2, preferred_element_type=F32))
    b = br[:c]
    r = br[c:]
    b_last = b[c - 1:c, :]
    q_sub = (qh * jnp.exp(b - r)).astype(BF16)
    q_dec = (qh * jnp.exp(b)).astype(BF16)
    k_end = (kh * jnp.exp(b_last - b)).astype(BF16)
    dec = jnp.exp(b_last)
    att = [[] for _ in heads]
    for i in range(n_sub):
        lo, hi = i * HG_SUB, (i + 1) * HG_SUB
        k_i = (kh[:hi] * jnp.exp(r[lo:lo + 1, :] - b[:hi])).astype(BF16)
        if hi < c:
            k_i = jnp.concatenate([k_i, jnp.zeros((c - hi, k_i.shape[1]), BF16)], axis=0)
        for h, hs in enumerate(heads):
            att[h].append(lax.dot_general(q_sub[lo:hi, hs], k_i[:, hs], (((1,), (1,)), ((), ())),
                                          preferred_element_type=F32))
    att = [jnp.where(tril, jnp.concatenate(a, axis=0), 0.0).astype(BF16) for a in att]
    return att, q_dec, k_end, dec


def _hgrn_back(front, v, sts):
    att, q_dec, k_end, dec = front
    heads = [slice(h * HEAD_DIM, (h + 1) * HEAD_DIM) for h in range(len(sts))]
    vb = v
    outs, new_sts = [], []
    for h, hs in enumerate(heads):
        o = jnp.dot(att[h], vb[:, hs], preferred_element_type=F32)
        o = o + lax.dot_general(q_dec[:, hs], sts[h].astype(BF16), (((1,), (1,)), ((), ())),
                                preferred_element_type=F32)
        outs.append(o)
    for h, hs in enumerate(heads):
        new_sts.append(sts[h] * dec[:, hs] + lax.dot_general(vb[:, hs], k_end[:, hs], (((0,), (0,)), ((), ())),
                                                             preferred_element_type=F32))
    return outs, new_sts


def _hgrn_kernel(lf_ref, qh_ref, kh_ref, v_ref, zh_ref, g_ref, s0_ref, o_ref, s_ref, st_scr, *, chunk, n_chunks, streams):
    tt = pl.program_id(2)
    nh = st_scr.shape[0] // streams
    tile = lf_ref.shape[0] // streams
    heads = [slice(h * HEAD_DIM, (h + 1) * HEAD_DIM) for h in range(nh)]

    @pl.when(tt == 0)
    def _():
        for s in range(streams):
            for h in range(nh):
                st_scr[s * nh + h] = s0_ref[s, h].T

    pfx = _prefix_matrix(chunk)
    ti = lax.broadcasted_iota(jnp.int32, (chunk, chunk), 0)
    si = lax.broadcasted_iota(jnp.int32, (chunk, chunk), 1)
    tril = si <= ti

    unroll = 2 if n_chunks % 2 == 0 else 1

    def step(ci, _):
        rss = [[pl.ds(pl.multiple_of(s * tile + (ci * unroll + u) * chunk, chunk), chunk) for u in range(unroll)]
               for s in range(streams)]
        fronts = [[_hgrn_front(lf_ref[rs, :], qh_ref[rs, :], kh_ref[rs, :], nh, pfx, tril) for rs in rss[s]]
                  for s in range(streams)]
        for s in range(streams):
            sts = [st_scr[s * nh + h] for h in range(nh)]
            for rs, front in zip(rss[s], fronts[s]):
                outs, sts = _hgrn_back(front, v_ref[rs, :], sts)
                for h in range(nh):
                    o = outs[h]
                    ms = jnp.mean(o * o, axis=-1, keepdims=True)
                    o_ref[rs, heads[h]] = (o * lax.rsqrt(ms + EPS) * g_ref[h]
                                           * _silu(zh_ref[rs, heads[h]])).astype(BF16)
            for h in range(nh):
                st_scr[s * nh + h] = sts[h]
        return 0

    lax.fori_loop(0, n_chunks // unroll, step, 0)

    @pl.when(tt == pl.num_programs(2) - 1)
    def _():
        for s in range(streams):
            for h in range(nh):
                s_ref[s, h] = st_scr[s * nh + h].T


def _hgrn_call(lf, qh, kh, v, zh, gain, s0, *, t, chunk, tile, heads, row0=0, streams=1):
    nb = s0.shape[0]
    tile = min(tile, t)
    rows = streams * tile
    assert t % tile == 0 and tile % chunk == 0 and chunk % HG_SUB == 0 and N_HEADS % heads == 0 and row0 % rows == 0
    assert nb % streams == 0 and (streams == 1 or tile == t)
    nt = t // tile
    tok_in = pl.BlockSpec((rows, heads * HEAD_DIM), lambda b, h, i: (row0 // rows + b * nt + i, h))
    tok = pl.BlockSpec((rows, heads * HEAD_DIM), lambda b, h, i: (b * nt + i, h))
    state = pl.BlockSpec((streams, heads, HEAD_DIM, HEAD_DIM), lambda b, h, i: (b, h, 0, 0))
    return pl.pallas_call(
        functools.partial(_hgrn_kernel, chunk=chunk, n_chunks=tile // chunk, streams=streams),
        grid=(nb // streams, N_HEADS // heads, nt),
        in_specs=[tok_in] * 5 + [pl.BlockSpec((heads, 1, HEAD_DIM), lambda b, h, i: (h, 0, 0)), state],
        out_specs=[tok, state],
        out_shape=[jax.ShapeDtypeStruct((nb * t, WIDTH), BF16), jax.ShapeDtypeStruct(s0.shape, F32)],
        scratch_shapes=[pltpu.VMEM((streams * heads, HEAD_DIM, HEAD_DIM), F32)],
        compiler_params=_cparams("arbitrary", "arbitrary", "arbitrary"),
        name="hgrn2",
    )(lf, qh, kh, v, zh, gain, s0)


def _out_kernel(gs_ref, gh_ref, gsb_ref, ghg_ref, x_ref, gate_ref, wsb_ref, whg_ref, wo_ref, y_ref):
    nb, tr, d = x_ref.shape
    y_sb = jnp.dot(gs_ref[...], wsb_ref[...], preferred_element_type=F32)
    y_h = jnp.dot(gh_ref[...], whg_ref[...], preferred_element_type=F32)
    merged = jax.nn.sigmoid(gsb_ref[...]) * y_sb + jax.nn.sigmoid(ghg_ref[...]) * y_h
    upd = jnp.dot(merged.astype(BF16), wo_ref[...], preferred_element_type=F32)
    y_ref[...] = x_ref[...] + gate_ref[...] * upd.reshape(nb, tr, d)


def _out_call(gs, gh, gg, x, gate, wsb, whg, wo, nb, tr, *, gg_row0=0):
    n, t, d = x.shape
    tm = nb * tr
    nt = t // tr
    assert gg_row0 % tm == 0 and (nb == 1 or nt == 1)
    rowblk = lambda w, c, r0=0: pl.BlockSpec((tm, w), lambda i, j: (r0 // tm + i * nt + j, c))
    const = lambda a: pl.BlockSpec(a.shape, lambda i, j: (0, 0), pipeline_mode=pl.Buffered(1))
    return pl.pallas_call(
        _out_kernel,
        grid=(n // nb, nt),
        in_specs=[rowblk(WIDTH, 0), rowblk(WIDTH, 0), rowblk(d, 0, gg_row0), rowblk(d, 1, gg_row0),
                  pl.BlockSpec((nb, tr, d), lambda i, j: (i, j, 0)),
                  pl.BlockSpec((nb, 1, d), lambda i, j: (i, 0, 0)),
                  const(wsb), const(whg), const(wo)],
        out_specs=pl.BlockSpec((nb, tr, d), lambda i, j: (i, j, 0)),
        out_shape=jax.ShapeDtypeStruct(x.shape, F32),
        compiler_params=_cparams("arbitrary", "arbitrary"),
        name="merge_out",
    )(gs, gh, gg, gg, x, gate, wsb, whg, wo)


def _layer(x_p, x_s, mod_p, mod_s, p, s0_s, caches):
    n_p, t_p, d = x_p.shape
    n_s, t_s, _ = x_s.shape
    assert n_p == 1
    rows_p, rows_s = t_p, n_s * t_s
    tm = math.gcd(1024, rows_p, rows_s)
    w_in = p["w_in"]

    h, q_p = _prenorm_q_call(x_p, p["norm_gain"], mod_p[1], mod_p[0], w_in, p["q_gain"], tm=tm, extra_rows=rows_s)
    pn_tr = min(1024, t_s)
    pn_nb = max(1, min(n_s, 1024 // pn_tr))
    h = _prenorm_call(x_s, p["norm_gain"], mod_s[1], mod_s[0], pn_nb, pn_tr, h, rows_p)

    proj = functools.partial(_proj_call, h, w_in, tm=tm)
    prompt, sample = dict(row0=0, rows=rows_p), dict(row0=rows_p, rows=rows_s)
    q_s, k_s, kb_s, v_s, vb_s = _proj_qkv_call(h, w_in, p["q_gain"], p["k_gain"], **sample)
    k_p, kb_p = proj(1 * WIDTH, WIDTH, "norm_k", (p["k_gain"],), **prompt)
    v_p, vb_p = proj(2 * WIDTH, WIDTH, "copy2", **prompt)
    (z_sb,) = proj(3 * WIDTH, WIDTH, "plain")
    logf, k_h = proj(4 * WIDTH, WIDTH, "forget", (p["lb_raw"],))
    (i_h,) = proj(5 * WIDTH, WIDTH, "plain_bf16")
    (q_h,) = proj(6 * WIDTH, WIDTH, "silu")
    (z_h,) = proj(7 * WIDTH, WIDTH, "plain")
    (gg,) = proj(8 * WIDTH, 2 * d, "plain")

    gs_p = _sb_prompt_call(q_p, kb_p, vb_p, z_sb)
    gs_s = _sb_sample_call(q_s, kb_s, vb_s, z_sb, caches[0], caches[1], z_row0=rows_p)
    hgrn = functools.partial(_hgrn_call, logf, q_h, k_h, i_h, z_h, p["onorm_gain"])
    gh_p, s_p = hgrn(jnp.zeros((n_p, N_HEADS, HEAD_DIM, HEAD_DIM), F32), t=t_p, chunk=min(128, t_p), tile=1024,
                     heads=N_HEADS)
    gh_s, s_s = hgrn(s0_s, t=t_s, chunk=t_s, tile=t_s, heads=N_HEADS, row0=rows_p, streams=math.gcd(n_s, 4))
    out = functools.partial(_out_call, wsb=p["w_br_sb"], whg=p["w_br_hg"], wo=p["w_out"])
    y_p = out(gs_p, gh_p, gg, x_p, mod_p[2], nb=1, tr=min(256, t_p))
    y_s = out(gs_s, gh_s, gg, x_s, mod_s[2], nb=max(1, min(n_s, 256 // t_s)), tr=t_s, gg_row0=rows_p)
    heads5 = lambda a, n, t: a.reshape(1, n, t, N_HEADS, HEAD_DIM)
    return (y_p, y_s, heads5(k_p, n_p, t_p), heads5(v_p, n_p, t_p), s_p[None],
            heads5(k_s, n_s, t_s), heads5(v_s, n_s, t_s), s_s[None])


def kernel(x_prompt, x_sample, cache_sb_k, cache_sb_v, state_hgrn, c_prompt, c_sample, norm_gain, w_ada, b_ada, w_in, q_norm_gain, k_norm_gain, hgrn_lb_raw, hgrn_onorm_gain, w_branch_sb, w_branch_hgrn, w_out):
    assert w_in.shape[0] == 1, "single-layer trunk"
    n_p, t_p, d = x_prompt.shape
    n_s, t_s, _ = x_sample.shape
    past = cache_sb_k.shape[2]

    c_all = jnp.concatenate([c_prompt, c_sample], axis=0)
    pad = (-c_all.shape[0]) % 8
    c_all = jnp.pad(c_all, ((0, pad), (0, 0)))
    mod = _ada_call(c_all, w_ada[0], b_ada[0].reshape(1, 3 * d))
    mods = lambda lo, hi: tuple(mod[lo:hi, i * d:(i + 1) * d].reshape(hi - lo, 1, d) for i in range(3))

    p = {
        "norm_gain": norm_gain[0].reshape(1, 1, d),
        "w_in": w_in[0],
        "q_gain": q_norm_gain[0].reshape(1, HEAD_DIM),
        "k_gain": k_norm_gain[0].reshape(1, HEAD_DIM),
        "lb_raw": hgrn_lb_raw,
        "onorm_gain": hgrn_onorm_gain[0].reshape(N_HEADS, 1, HEAD_DIM),
        "w_br_sb": w_branch_sb[0].astype(BF16),
        "w_br_hg": w_branch_hgrn[0].astype(BF16),
        "w_out": w_out[0].astype(BF16),
    }

    return _layer(x_prompt, x_sample, mods(0, n_p), mods(n_p, n_p + n_s), p, state_hgrn[0],
                  (cache_sb_k.reshape(n_s, past * N_HEADS, HEAD_DIM), cache_sb_v.reshape(n_s, past * N_HEADS, HEAD_DIM)))
```

```python
import functools
import math

import jax
import jax.numpy as jnp
from jax import lax
from jax.experimental import pallas as pl
from jax.experimental.pallas import tpu as pltpu

F32 = jnp.float32
BF16 = jnp.bfloat16

N_HEADS = 8
HEAD_DIM = 128
WIDTH = N_HEADS * HEAD_DIM
HG_SUB = 16
EPS = 1e-6
SB_BLOCK = 128
SB_LOG_CUTOFF = -88.0
VMEM_LIMIT = 56 * 1024 * 1024


def _cparams(*sem):
    return pltpu.CompilerParams(dimension_semantics=sem, vmem_limit_bytes=VMEM_LIMIT)


def _silu(x):
    return x * jax.nn.sigmoid(x)


def _ada_kernel(c_ref, w_ref, b_ref, o_ref):
    c = c_ref[...]
    a = _silu(c).astype(BF16)
    o_ref[...] = jnp.dot(a, w_ref[...].astype(BF16), preferred_element_type=F32) + b_ref[...]


def _ada_call(c, w, b):
    r, d = c.shape
    n = w.shape[1]
    tn = 1024
    return pl.pallas_call(
        _ada_kernel,
        grid=(n // tn,),
        in_specs=[pl.BlockSpec((r, d), lambda j: (0, 0)),
                  pl.BlockSpec((d, tn), lambda j: (0, j)),
                  pl.BlockSpec((1, tn), lambda j: (0, j))],
        out_specs=pl.BlockSpec((r, tn), lambda j: (0, j)),
        out_shape=jax.ShapeDtypeStruct((r, n), F32),
        compiler_params=_cparams("arbitrary"),
        name="ada_mod",
    )(c, w, b)


def _prenorm_kernel(x_ref, g_ref, sc_ref, sh_ref, hall_ref, h_ref):
    del hall_ref
    x = x_ref[...]
    ms = jnp.mean(x * x, axis=-1, keepdims=True)
    xn = x * lax.rsqrt(ms + EPS)
    h = xn * g_ref[...] * (1.0 + sc_ref[...]) + sh_ref[...]
    h_ref[...] = h.astype(BF16).reshape(h_ref.shape)


def _prenorm_call(x, gain, scale, shift, nb, tr, h_all, row0):
    n, t, d = x.shape
    tm = nb * tr
    assert row0 % tm == 0 and (nb == 1 or tr == t)
    vec = pl.BlockSpec((nb, 1, d), lambda i, j: (i, 0, 0))
    return pl.pallas_call(
        _prenorm_kernel,
        grid=(n // nb, t // tr),
        in_specs=[pl.BlockSpec((nb, tr, d), lambda i, j: (i, j, 0)),
                  pl.BlockSpec((1, 1, d), lambda i, j: (0, 0, 0)), vec, vec,
                  pl.BlockSpec(memory_space=pl.ANY)],
        out_specs=pl.BlockSpec((tm, d), lambda i, j: (row0 // tm + i * (t // tr) + j, 0)),
        out_shape=jax.ShapeDtypeStruct(h_all.shape, BF16),
        input_output_aliases={4: 0},
        compiler_params=_cparams("arbitrary", "arbitrary"),
        name="prenorm",
    )(x, gain, scale, shift, h_all)


def _head_rms(y, gain):
    outs = []
    for g in range(N_HEADS):
        yh = y[:, g * HEAD_DIM:(g + 1) * HEAD_DIM]
        ms = jnp.mean(yh * yh, axis=-1, keepdims=True)
        outs.append(yh * lax.rsqrt(ms + EPS) * gain)
    return outs


def _proj_kernel(*refs, kind):
    h_ref, w_ref = refs[0], refs[1]
    wb_ref = refs[-1]

    @pl.when(pl.program_id(1) == 0)
    def _():
        wb_ref[...] = w_ref[...].astype(BF16)

    y = jnp.dot(h_ref[...], wb_ref[...], preferred_element_type=F32)
    tm = y.shape[0]
    if kind == "plain":
        refs[2][...] = y
    elif kind == "plain_bf16":
        refs[2][...] = y.astype(BF16)
    elif kind == "silu":
        refs[2][...] = _silu(y)
    elif kind == "copy2":
        for g in range(N_HEADS):
            refs[2][pl.ds(g, tm, stride=N_HEADS), :] = y[:, g * HEAD_DIM:(g + 1) * HEAD_DIM]
        refs[3][...] = y.astype(BF16)
    elif kind == "norm_k":
        gain = refs[2][...]
        for g, o in enumerate(_head_rms(y, gain)):
            refs[3][pl.ds(g, tm, stride=N_HEADS), :] = o
            refs[4][:, g * HEAD_DIM:(g + 1) * HEAD_DIM] = o.astype(BF16)
    elif kind == "forget":
        raw = refs[2][...]
        e = jnp.exp(raw - jnp.max(raw, axis=0, keepdims=True))
        lb = e[0:1, :] / jnp.sum(e, axis=0, keepdims=True)
        f = lb + (1.0 - lb) * jax.nn.sigmoid(y)
        refs[3][...] = jnp.log(f)
        refs[4][...] = 1.0 - f
    else:
        raise ValueError(kind)


def _proj_call(h, w_in, col0, ncols, kind, extra=(), *, tm=512, tn=1024, row0=0, rows=None):
    d = h.shape[1]
    rows = h.shape[0] - row0 if rows is None else rows
    assert col0 % tn == 0 and ncols % tn == 0 and rows % tm == 0 and row0 % tm == 0
    jb, ib = col0 // tn, row0 // tm
    grid = (ncols // tn, rows // tm)
    tile = lambda: pl.BlockSpec((tm, tn), lambda j, i: (i, j))
    in_specs = [pl.BlockSpec((tm, d), lambda j, i: (ib + i, 0)),
                pl.BlockSpec((d, tn), lambda j, i: (0, jb + j))]
    for e in extra:
        in_specs.append(pl.BlockSpec(e.shape, lambda j, i: (0, 0)))
    if kind in ("plain", "silu"):
        out_dt = (F32,)
    elif kind == "plain_bf16":
        out_dt = (BF16,)
    elif kind in ("copy2", "norm_k"):
        out_dt = (F32, BF16)
    else:
        out_dt = (F32, F32)
    out_specs = [tile() for _ in out_dt]
    out_shape = [jax.ShapeDtypeStruct((rows, ncols), dt) for dt in out_dt]
    if kind in ("copy2", "norm_k"):
        assert ncols == WIDTH
        out_specs[0] = pl.BlockSpec((tm * N_HEADS, HEAD_DIM), lambda j, i: (i, 0))
        out_shape[0] = jax.ShapeDtypeStruct((rows * N_HEADS, HEAD_DIM), F32)
    outs = pl.pallas_call(
        functools.partial(_proj_kernel, kind=kind),
        grid=grid,
        in_specs=in_specs,
        out_specs=out_specs,
        out_shape=out_shape,
        scratch_shapes=[pltpu.VMEM((d, tn), BF16)],
        compiler_params=_cparams("arbitrary", "arbitrary"),
        name="proj_" + kind,
    )(h, w_in, *extra)
    return outs


def _proj_qkv_kernel(h_ref, w_ref, qg_ref, kg_ref, q_ref, k_ref, kb_ref, v_ref, vb_ref):
    j = pl.program_id(1)
    y = jnp.dot(h_ref[...], w_ref[...].astype(BF16), preferred_element_type=F32)
    tm = y.shape[0]
    heads = [slice(g * HEAD_DIM, (g + 1) * HEAD_DIM) for g in range(N_HEADS)]

    @pl.when(j == 0)
    def _():
        for g, o in enumerate(_head_rms(y, qg_ref[...])):
            q_ref[:, heads[g]] = o.astype(BF16)

    @pl.when(j == 1)
    def _():
        for g, o in enumerate(_head_rms(y, kg_ref[...])):
            k_ref[pl.ds(g, tm, stride=N_HEADS), :] = o
            kb_ref[:, heads[g]] = o.astype(BF16)

    @pl.when(j == 2)
    def _():
        for g in range(N_HEADS):
            v_ref[pl.ds(g, tm, stride=N_HEADS), :] = y[:, heads[g]]
        vb_ref[...] = y.astype(BF16)


def _proj_qkv_call(h, w_in, q_gain, k_gain, *, row0, rows, tm=512):
    d = h.shape[1]
    tm = min(tm, rows)
    assert rows % tm == 0 and row0 % tm == 0
    ib = row0 // tm
    wide = lambda: pl.BlockSpec((tm, WIDTH), lambda i, j: (i, 0))
    tall = lambda: pl.BlockSpec((tm * N_HEADS, HEAD_DIM), lambda i, j: (i, 0))
    gain = lambda g: pl.BlockSpec(g.shape, lambda i, j: (0, 0))
    return pl.pallas_call(
        _proj_qkv_kernel,
        grid=(rows // tm, 3),
        in_specs=[pl.BlockSpec((tm, d), lambda i, j: (ib + i, 0)),
                  pl.BlockSpec((d, WIDTH), lambda i, j: (0, j)), gain(q_gain), gain(k_gain)],
        out_specs=[wide(), tall(), wide(), tall(), wide()],
        out_shape=[jax.ShapeDtypeStruct((rows, WIDTH), BF16),
                   jax.ShapeDtypeStruct((rows * N_HEADS, HEAD_DIM), F32), jax.ShapeDtypeStruct((rows, WIDTH), BF16),
                   jax.ShapeDtypeStruct((rows * N_HEADS, HEAD_DIM), F32), jax.ShapeDtypeStruct((rows, WIDTH), BF16)],
        compiler_params=_cparams("arbitrary", "arbitrary"),
        name="proj_qkv",
    )(h, w_in, q_gain, k_gain)


def _prenorm_q_kernel(x_ref, g_ref, sc_ref, sh_ref, w_ref, qg_ref, h_ref, q_ref, wb_ref, h2_ref, *, nrow, extra):
    s = pl.program_id(0)

    @pl.when(s == 0)
    def _():
        wb_ref[...] = w_ref[...].astype(BF16)
        h2_ref[1] = jnp.zeros(h2_ref.shape[1:], BF16)

    y = jnp.dot(h2_ref[(s + 1) % 2], wb_ref[...], preferred_element_type=F32)
    for g, o in enumerate(_head_rms(y, qg_ref[...])):
        q_ref[:, g * HEAD_DIM:(g + 1) * HEAD_DIM] = o.astype(BF16)

    x = x_ref[0]
    ms = jnp.mean(x * x, axis=-1, keepdims=True)
    hn = (x * lax.rsqrt(ms + EPS) * g_ref[0] * (1.0 + sc_ref[0]) + sh_ref[0]).astype(BF16)
    h2_ref[s % 2] = hn
    h_ref[...] = jnp.where(s < nrow, hn, jnp.zeros_like(hn)) if extra else hn


def _prenorm_q_call(x, gain, scale, shift, w_in, q_gain, *, tm, extra_rows=0):
    n, t, d = x.shape
    assert n == 1 and t % tm == 0 and extra_rows % tm == 0
    nrow, extra = t // tm, extra_rows // tm
    steps = nrow + max(1, extra)
    this = lambda s: jnp.minimum(s, nrow - 1)
    prev = lambda s: jnp.minimum(jnp.maximum(s - 1, 0), nrow - 1)
    vec = pl.BlockSpec((1, 1, d), lambda s: (0, 0, 0))
    return pl.pallas_call(
        functools.partial(_prenorm_q_kernel, nrow=nrow, extra=extra),
        grid=(steps,),
        in_specs=[pl.BlockSpec((1, tm, d), lambda s: (0, this(s), 0)), vec, vec, vec,
                  pl.BlockSpec((d, WIDTH), lambda s: (0, 0), pipeline_mode=pl.Buffered(1)),
                  pl.BlockSpec(q_gain.shape, lambda s: (0, 0))],
        out_specs=[pl.BlockSpec((tm, d), lambda s: (jnp.minimum(s, nrow - 1 + extra), 0)),
                   pl.BlockSpec((tm, WIDTH), lambda s: (prev(s), 0))],
        out_shape=[jax.ShapeDtypeStruct((t + extra_rows, d), BF16), jax.ShapeDtypeStruct((t, WIDTH), BF16)],
        scratch_shapes=[pltpu.VMEM((d, WIDTH), BF16), pltpu.VMEM((2, tm, d), BF16)],
        compiler_params=_cparams("arbitrary"),
        name="prenorm_q",
    )(x, gain, scale, shift, w_in, q_gain)


def _suffix_matrix(bk):
    j = lax.broadcasted_iota(jnp.int32, (2 * bk, 2 * bk), 0) % bk
    s = lax.broadcasted_iota(jnp.int32, (2 * bk, 2 * bk), 1)
    return jnp.where((j > s) | (s >= bk), -1.0, 0.0).astype(BF16)


def _sb_tiles(qs, ks, vs, carries, sfx, masks=None, valid=None):
    bk = sfx.shape[0] // 2
    n = range(len(qs))
    spans = [range(ks[i].shape[0] // bk) for i in n]
    lanes = lambda x, t: x[:, t * bk:(t + 1) * bk]
    mask_of = lambda i, t: None if masks is None or masks[i] is None else masks[i][t]
    zs = [lax.dot_general(qs[i], ks[i], (((1,), (1,)), ((), ())), preferred_element_type=F32) * HEAD_DIM ** -0.5
          for i in n]
    sps = [jnp.maximum(z, 0.0) + jnp.log(1.0 + jnp.exp(-jnp.abs(z))) for z in zs]
    r2s = []
    for i in n:
        r2 = []
        for t in spans[i]:
            m = mask_of(i, t)
            l1m = lanes(sps[i], t) if m is None else jnp.where(m, lanes(sps[i], t), 0.0)
            hi = l1m.astype(BF16)
            lo = (l1m - hi.astype(F32)).astype(BF16)
            r2.append(jnp.dot(jnp.concatenate([hi, lo], axis=1), sfx, preferred_element_type=F32))
        r2s.append(r2)
    new, wss = [], []
    for i in n:
        c = carries[i]
        ws = []
        for t in spans[i]:
            w = jnp.exp(lanes(zs[i], t) - lanes(sps[i], t) + r2s[i][t][:, :bk] + c)
            m = mask_of(i, t)
            if m is not None:
                w = jnp.where(m, w, 0.0)
            if valid is not None and valid[i][t] is not None:
                w = jnp.where(valid[i][t], w, 0.0)
            ws.append(w.astype(BF16))
            c = c + r2s[i][t][:, bk:]
        new.append(c)
        wss.append(ws[0] if len(ws) == 1 else jnp.concatenate(ws, axis=1))
    pvs = [jnp.dot(wss[i], vs[i], preferred_element_type=F32) for i in n]
    return new, pvs


def _sb_prompt_kernel(q_ref, k_ref, v_ref, z_ref, o_ref, c_scr, acc_scr, *, n_groups, group, ahead):
    blk = SB_BLOCK
    qt = pl.program_id(1)
    sfx = _suffix_matrix(blk)
    row = lax.broadcasted_iota(jnp.int32, (blk, blk), 0)
    col = lax.broadcasted_iota(jnp.int32, (blk, blk), 1)
    causal = col < row
    alive = lambda cs: (functools.reduce(jnp.maximum, [jnp.max(c) for c in cs]) >= SB_LOG_CUTOFF).astype(jnp.int32)

    def kv(kb):
        start = pl.multiple_of(kb * blk, blk)
        return k_ref[pl.ds(start, blk), :], v_ref[pl.ds(start, blk), :]

    def qgroup(ig, _):
        gq0 = (qt * n_groups + ig) * group
        rows = [pl.ds(pl.multiple_of((ig * group + g) * blk, blk), blk) for g in range(group)]
        qs = [q_ref[r, :] for r in rows]
        kbs = [[gq0 + g - s for s in range(1 + ahead)] for g in range(group)]
        kvs = [[kv(jnp.maximum(kb, 0)) for kb in kbs[g]] for g in range(group)]
        cat = lambda xs: jnp.concatenate(xs, axis=0)
        cs, pvs = _sb_tiles(qs, [cat([k for k, _ in kvs[g]]) for g in range(group)],
                            [cat([v for _, v in kvs[g]]) for g in range(group)],
                            [jnp.zeros((blk, blk), F32)] * group, sfx,
                            masks=[[causal] + [None] * ahead] * group,
                            valid=[[None] + [kb >= 0 for kb in kbs[g][1:]] for g in range(group)])
        for g in range(group):
            c_scr[g] = cs[g]
            acc_scr[g] = pvs[g]

        def cond(st):
            s, go = st
            return jnp.logical_and(s <= gq0 + group - 1, go > 0)

        def body(st):
            s, _ = st
            kbs = [gq0 + g - s for g in range(group)]
            kvs = [kv(jnp.maximum(kb, 0)) for kb in kbs]
            cs, pvs = _sb_tiles(qs, [k for k, _ in kvs], [v for _, v in kvs],
                                [c_scr[g] for g in range(group)], sfx, valid=[[kb >= 0] for kb in kbs])
            for g in range(group):
                c_scr[g] = cs[g]
                acc_scr[g] += pvs[g]
            return s + 1, alive(cs)

        lax.while_loop(cond, body, (1 + ahead, alive(cs)))
        for g in range(group):
            o_ref[rows[g], :] = (acc_scr[g] * _silu(z_ref[rows[g], :])).astype(BF16)
        return 0

    lax.fori_loop(0, n_groups, qgroup, 0)


def _sb_prompt_call(q, k, v, z, *, tq=8192, group=32, ahead=2):
    t = q.shape[0]
    tq = min(tq, t)
    group = math.gcd(group, tq // SB_BLOCK)
    assert t % tq == 0 and tq % (SB_BLOCK * group) == 0
    qspec = pl.BlockSpec((tq, HEAD_DIM), lambda h, i: (i, h))
    kvspec = pl.BlockSpec((t, HEAD_DIM), lambda h, i: (0, h))
    return pl.pallas_call(
        functools.partial(_sb_prompt_kernel, n_groups=tq // (SB_BLOCK * group), group=group, ahead=ahead),
        grid=(N_HEADS, t // tq),
        in_specs=[qspec, kvspec, kvspec, qspec],
        out_specs=qspec,
        out_shape=jax.ShapeDtypeStruct((t, WIDTH), BF16),
        scratch_shapes=[pltpu.VMEM((group, SB_BLOCK, SB_BLOCK), F32), pltpu.VMEM((group, SB_BLOCK, HEAD_DIM), F32)],
        compiler_params=_cparams("arbitrary", "arbitrary"),
        name="sb_prompt",
    )(q, k, v, z)


def _sb_sample_kernel(q_ref, kn_ref, vn_ref, z_ref, kc_hbm, vc_hbm, o_ref, kbuf, vbuf, sem, c_scr, acc_scr, *, past):
    blk = SB_BLOCK
    nh = N_HEADS
    b = pl.program_id(0)
    tq = q_ref.shape[0]
    half = blk - tq
    n_full = (past - half) // blk
    rem = (past - half) % blk
    sfx = _suffix_matrix(blk)
    row = lax.broadcasted_iota(jnp.int32, (tq, blk), 0)
    col = lax.broadcasted_iota(jnp.int32, (tq, blk), 1)
    heads = [slice(h * HEAD_DIM, (h + 1) * HEAD_DIM) for h in range(nh)]
    alive = lambda cs: (functools.reduce(jnp.maximum, [jnp.max(c) for c in cs]) >= SB_LOG_CUTOFF).astype(jnp.int32)

    def copies(stream, key0, nkeys, slot):
        src = pl.ds(key0 * nh, nkeys * nh)
        dst = pl.ds(0, nkeys * nh)
        return (pltpu.make_async_copy(kc_hbm.at[stream, src, :], kbuf.at[slot, dst, :], sem.at[0, slot]),
                pltpu.make_async_copy(vc_hbm.at[stream, src, :], vbuf.at[slot, dst, :], sem.at[1, slot]))

    def start(cps):
        for cp in cps:
            cp.start()

    def wait(cps):
        for cp in cps:
            cp.wait()

    first = min(n_full, 1)
    base = 2 * (b % 2)

    def tile_copies(j):
        return copies(b, past - half - (j + 1) * blk, blk, base + (j + 1) % 2)

    def first_copies(stream):
        slot0 = 2 * (stream % 2)
        cps = copies(stream, past - half, half, slot0)
        return cps + copies(stream, past - half - blk, blk, slot0 + 1) if first else cps

    def cached(buf, slot, h, nkeys):
        return buf[slot, pl.ds(h, nkeys, stride=nh), :].astype(BF16)

    @pl.when(b == 0)
    def _():
        start(first_copies(b))

    @pl.when(b + 1 < pl.num_programs(0))
    def _():
        start(first_copies(b + 1))

    wait(first_copies(b))

    qs = [q_ref[:, heads[h]] for h in range(nh)]

    def span(buf, new_ref, h):
        tiles = [cached(buf, base, h, half), new_ref[:, heads[h]]] + [cached(buf, base + 1, h, blk)] * first
        return jnp.concatenate(tiles, axis=0)

    cs, pvs = _sb_tiles(qs, [span(kbuf, kn_ref, h) for h in range(nh)], [span(vbuf, vn_ref, h) for h in range(nh)],
                        [jnp.zeros((tq, blk), F32)] * nh, sfx, masks=[[col < row + half] + [None] * first] * nh)
    for h in range(nh):
        c_scr[h] = cs[h]
        acc_scr[h] = pvs[h]

    def sweep(slot, mask):
        cs, pvs = _sb_tiles(qs, [cached(kbuf, slot, h, blk) for h in range(nh)],
                            [cached(vbuf, slot, h, blk) for h in range(nh)],
                            [c_scr[h] for h in range(nh)], sfx, masks=None if mask is None else [[mask]] * nh)
        for h in range(nh):
            c_scr[h] = cs[h]
            acc_scr[h] += pvs[h]
        return cs

    def cond(st):
        j, go = st
        return jnp.logical_and(j < n_full, go > 0)

    def body(st):
        j, _ = st
        cps = tile_copies(j)
        start(cps)
        wait(cps)
        return j + 1, alive(sweep(base + (j + 1) % 2, None))

    _, go = lax.while_loop(cond, body, (first, alive(cs)))

    if rem:
        @pl.when(go > 0)
        def _():
            cps = copies(b, 0, blk, base)
            start(cps)
            wait(cps)
            sweep(base, col < rem)

    for h in range(nh):
        o_ref[:, heads[h]] = (acc_scr[h] * _silu(z_ref[:, heads[h]])).astype(BF16)


def _sb_sample_call(q, kn, vn, z, kc, vc, *, z_row0=0):
    nb = kc.shape[0]
    past = kc.shape[1] // N_HEADS
    tq = q.shape[0] // nb
    assert tq % 16 == 0 and tq < SB_BLOCK and past >= SB_BLOCK and z_row0 % tq == 0
    new = pl.BlockSpec((tq, WIDTH), lambda b: (b, 0))
    zspec = pl.BlockSpec((tq, WIDTH), lambda b: (z_row0 // tq + b, 0))
    hbm = pl.BlockSpec(memory_space=pl.ANY)
    return pl.pallas_call(
        functools.partial(_sb_sample_kernel, past=past),
        grid=(nb,),
        in_specs=[new, new, new, zspec, hbm, hbm],
        out_specs=new,
        out_shape=jax.ShapeDtypeStruct(q.shape, BF16),
        scratch_shapes=[pltpu.VMEM((4, SB_BLOCK * N_HEADS, HEAD_DIM), F32),
                        pltpu.VMEM((4, SB_BLOCK * N_HEADS, HEAD_DIM), F32),
                        pltpu.SemaphoreType.DMA((2, 4)),
                        pltpu.VMEM((N_HEADS, tq, SB_BLOCK), F32), pltpu.VMEM((N_HEADS, tq, HEAD_DIM), F32)],
        compiler_params=_cparams("arbitrary"),
        name="sb_sample",
    )(q, kn, vn, z, kc, vc)


def _prefix_matrix(c):
    t = lax.broadcasted_iota(jnp.int32, (2 * c, c), 0)
    s = lax.broadcasted_iota(jnp.int32, (2 * c, c), 1)
    incl = (t < c) & (s <= t)
    sub = (t >= c) & (s < ((t - c) // HG_SUB) * HG_SUB)
    return jnp.where(incl | sub, 1.0, 0.0).astype(BF16)


def _hgrn_front(lf, qh, kh, n_heads, pfx, tril):
    c = lf.shape[0]
    n_sub = c // HG_SUB
    heads = [slice(h * HEAD_DIM, (h + 1) * HEAD_DIM) for h in range(n_heads)]
    p0 = lf.astype(BF16)
    r1 = lf - p0.astype(F32)
    p1 = r1.astype(BF16)
    p2 = (r1 - p1.astype(F32)).astype(BF16)
    br = (jnp.dot(pfx, p0, preferred_element_type=F32) + jnp.dot(pfx, p1, preferred_element_type=F32)
          + jnp.dot(pfx, p2, preferred_element_type=F32))
    b = br[:c]
    r = br[c:]
    b_last = b[c - 1:c, :]
    q_sub = (qh * jnp.exp(b - r)).astype(BF16)
    q_dec = (qh * jnp.exp(b)).astype(BF16)
    k_end = (kh * jnp.exp(b_last - b)).astype(BF16)
    dec = jnp.exp(b_last)
    att = [[] for _ in heads]
    for i in range(n_sub):
        lo, hi = i * HG_SUB, (i + 1) * HG_SUB
        k_i = (kh[:hi] * jnp.exp(r[lo:lo + 1, :] - b[:hi])).astype(BF16)
        if hi < c:
            k_i = jnp.concatenate([k_i, jnp.zeros((c - hi, k_i.shape[1]), BF16)], axis=0)
        for h, hs in enumerate(heads):
            att[h].append(lax.dot_general(q_sub[lo:hi, hs], k_i[:, hs], (((1,), (1,)), ((), ())),
                                          preferred_element_type=F32))
    att = [jnp.where(tril, jnp.concatenate(a, axis=0), 0.0).astype(BF16) for a in att]
    return att, q_dec, k_end, dec


def _hgrn_back(front, v, sts):
    att, q_dec, k_end, dec = front
    heads = [slice(h * HEAD_DIM, (h + 1) * HEAD_DIM) for h in range(len(sts))]
    vb = v
    outs, new_sts = [], []
    for h, hs in enumerate(heads):
        o = jnp.dot(att[h], vb[:, hs], preferred_element_type=F32)
        o = o + lax.dot_general(q_dec[:, hs], sts[h].astype(BF16), (((1,), (1,)), ((), ())),
                                preferred_element_type=F32)
        outs.append(o)
    for h, hs in enumerate(heads):
        new_sts.append(sts[h] * dec[:, hs] + lax.dot_general(vb[:, hs], k_end[:, hs], (((0,), (0,)), ((), ())),
                                                             preferred_element_type=F32))
    return outs, new_sts


def _hgrn_kernel(lf_ref, qh_ref, kh_ref, v_ref, zh_ref, g_ref, s0_ref, o_ref, s_ref, st_scr, *, chunk, n_chunks, streams):
    tt = pl.program_id(2)
    nh = st_scr.shape[0] // streams
    tile = lf_ref.shape[0] // streams
    heads = [slice(h * HEAD_DIM, (h + 1) * HEAD_DIM) for h in range(nh)]

    @pl.when(tt == 0)
    def _():
        for s in range(streams):
            for h in range(nh):
                st_scr[s * nh + h] = s0_ref[s, h].T

    pfx = _prefix_matrix(chunk)
    ti = lax.broadcasted_iota(jnp.int32, (chunk, chunk), 0)
    si = lax.broadcasted_iota(jnp.int32, (chunk, chunk), 1)
    tril = si <= ti

    unroll = 2 if n_chunks % 2 == 0 else 1

    def step(ci, _):
        rss = [[pl.ds(pl.multiple_of(s * tile + (ci * unroll + u) * chunk, chunk), chunk) for u in range(unroll)]
               for s in range(streams)]
        fronts = [[_hgrn_front(lf_ref[rs, :], qh_ref[rs, :], kh_ref[rs, :], nh, pfx, tril) for rs in rss[s]]
                  for s in range(streams)]
        for s in range(streams):
            sts = [st_scr[s * nh + h] for h in range(nh)]
            for rs, front in zip(rss[s], fronts[s]):
                outs, sts = _hgrn_back(front, v_ref[rs, :], sts)
                for h in range(nh):
                    o = outs[h]
                    ms = jnp.mean(o * o, axis=-1, keepdims=True)
                    o_ref[rs, heads[h]] = (o * lax.rsqrt(ms + EPS) * g_ref[h]
                                           * _silu(zh_ref[rs, heads[h]])).astype(BF16)
            for h in range(nh):
                st_scr[s * nh + h] = sts[h]
        return 0

    lax.fori_loop(0, n_chunks // unroll, step, 0)

    @pl.when(tt == pl.num_programs(2) - 1)
    def _():
        for s in range(streams):
            for h in range(nh):
                s_ref[s, h] = st_scr[s * nh + h].T


def _hgrn_call(lf, qh, kh, v, zh, gain, s0, *, t, chunk, tile, heads, row0=0, streams=1):
    nb = s0.shape[0]
    tile = min(tile, t)
    rows = streams * tile
    assert t % tile == 0 and tile % chunk == 0 and chunk % HG_SUB == 0 and N_HEADS % heads == 0 and row0 % rows == 0
    assert nb % streams == 0 and (streams == 1 or tile == t)
    nt = t // tile
    tok_in = pl.BlockSpec((rows, heads * HEAD_DIM), lambda b, h, i: (row0 // rows + b * nt + i, h))
    tok = pl.BlockSpec((rows, heads * HEAD_DIM), lambda b, h, i: (b * nt + i, h))
    state = pl.BlockSpec((streams, heads, HEAD_DIM, HEAD_DIM), lambda b, h, i: (b, h, 0, 0))
    return pl.pallas_call(
        functools.partial(_hgrn_kernel, chunk=chunk, n_chunks=tile // chunk, streams=streams),
        grid=(nb // streams, N_HEADS // heads, nt),
        in_specs=[tok_in] * 5 + [pl.BlockSpec((heads, 1, HEAD_DIM), lambda b, h, i: (h, 0, 0)), state],
        out_specs=[tok, state],
        out_shape=[jax.ShapeDtypeStruct((nb * t, WIDTH), BF16), jax.ShapeDtypeStruct(s0.shape, F32)],
        scratch_shapes=[pltpu.VMEM((streams * heads, HEAD_DIM, HEAD_DIM), F32)],
        compiler_params=_cparams("arbitrary", "arbitrary", "arbitrary"),
        name="hgrn2",
    )(lf, qh, kh, v, zh, gain, s0)


def _out_kernel(gs_ref, gh_ref, gsb_ref, ghg_ref, x_ref, gate_ref, wsb_ref, whg_ref, wo_ref, y_ref):
    nb, tr, d = x_ref.shape
    y_sb = jnp.dot(gs_ref[...], wsb_ref[...], preferred_element_type=F32)
    y_h = jnp.dot(gh_ref[...], whg_ref[...], preferred_element_type=F32)
    merged = jax.nn.sigmoid(gsb_ref[...]) * y_sb + jax.nn.sigmoid(ghg_ref[...]) * y_h
    upd = jnp.dot(merged.astype(BF16), wo_ref[...], preferred_element_type=F32)
    y_ref[...] = x_ref[...] + gate_ref[...] * upd.reshape(nb, tr, d)


def _out_call(gs, gh, gg, x, gate, wsb, whg, wo, nb, tr, *, gg_row0=0):
    n, t, d = x.shape
    tm = nb * tr
    nt = t // tr
    assert gg_row0 % tm == 0 and (nb == 1 or nt == 1)
    rowblk = lambda w, c, r0=0: pl.BlockSpec((tm, w), lambda i, j: (r0 // tm + i * nt + j, c))
    const = lambda a: pl.BlockSpec(a.shape, lambda i, j: (0, 0), pipeline_mode=pl.Buffered(1))
    return pl.pallas_call(
        _out_kernel,
        grid=(n // nb, nt),
        in_specs=[rowblk(WIDTH, 0), rowblk(WIDTH, 0), rowblk(d, 0, gg_row0), rowblk(d, 1, gg_row0),
                  pl.BlockSpec((nb, tr, d), lambda i, j: (i, j, 0)),
                  pl.BlockSpec((nb, 1, d), lambda i, j: (i, 0, 0)),
                  const(wsb), const(whg), const(wo)],
        out_specs=pl.BlockSpec((nb, tr, d), lambda i, j: (i, j, 0)),
        out_shape=jax.ShapeDtypeStruct(x.shape, F32),
        compiler_params=_cparams("arbitrary", "arbitrary"),
        name="merge_out",
    )(gs, gh, gg, gg, x, gate, wsb, whg, wo)


def _layer(x_p, x_s, mod_p, mod_s, p, s0_s, caches):
    n_p, t_p, d = x_p.shape
    n_s, t_s, _ = x_s.shape
    assert n_p == 1
    rows_p, rows_s = t_p, n_s * t_s
    tm = math.gcd(1024, rows_p, rows_s)
    w_in = p["w_in"]

    h, q_p = _prenorm_q_call(x_p, p["norm_gain"], mod_p[1], mod_p[0], w_in, p["q_gain"], tm=tm, extra_rows=rows_s)
    pn_tr = min(1024, t_s)
    pn_nb = max(1, min(n_s, 1024 // pn_tr))
    h = _prenorm_call(x_s, p["norm_gain"], mod_s[1], mod_s[0], pn_nb, pn_tr, h, rows_p)

    proj = functools.partial(_proj_call, h, w_in, tm=tm)
    prompt, sample = dict(row0=0, rows=rows_p), dict(row0=rows_p, rows=rows_s)
    q_s, k_s, kb_s, v_s, vb_s = _proj_qkv_call(h, w_in, p["q_gain"], p["k_gain"], **sample)
    k_p, kb_p = proj(1 * WIDTH, WIDTH, "norm_k", (p["k_gain"],), **prompt)
    v_p, vb_p = proj(2 * WIDTH, WIDTH, "copy2", **prompt)
    (z_sb,) = proj(3 * WIDTH, WIDTH, "plain")
    logf, k_h = proj(4 * WIDTH, WIDTH, "forget", (p["lb_raw"],))
    (i_h,) = proj(5 * WIDTH, WIDTH, "plain_bf16")
    (q_h,) = proj(6 * WIDTH, WIDTH, "silu")
    (z_h,) = proj(7 * WIDTH, WIDTH, "plain")
    (gg,) = proj(8 * WIDTH, 2 * d, "plain")

    gs_p = _sb_prompt_call(q_p, kb_p, vb_p, z_sb)
    gs_s = _sb_sample_call(q_s, kb_s, vb_s, z_sb, caches[0], caches[1], z_row0=rows_p)
    hgrn = functools.partial(_hgrn_call, logf, q_h, k_h, i_h, z_h, p["onorm_gain"])
    gh_p, s_p = hgrn(jnp.zeros((n_p, N_HEADS, HEAD_DIM, HEAD_DIM), F32), t=t_p, chunk=min(128, t_p), tile=1024,
                     heads=N_HEADS)
    gh_s, s_s = hgrn(s0_s, t=t_s, chunk=t_s, tile=t_s, heads=N_HEADS, row0=rows_p, streams=math.gcd(n_s, 4))
    out = functools.partial(_out_call, wsb=p["w_br_sb"], whg=p["w_br_hg"], wo=p["w_out"])
    y_p = out(gs_p, gh_p, gg, x_p, mod_p[2], nb=1, tr=min(256, t_p))
    y_s = out(gs_s, gh_s, gg, x_s, mod_s[2], nb=max(1, min(n_s, 256 // t_s)), tr=t_s, gg_row0=rows_p)
    heads5 = lambda a, n, t: a.reshape(1, n, t, N_HEADS, HEAD_DIM)
    return (y_p, y_s, heads5(k_p, n_p, t_p), heads5(v_p, n_p, t_p), s_p[None],
            heads5(k_s, n_s, t_s), heads5(v_s, n_s, t_s), s_s[None])


def kernel(x_prompt, x_sample, cache_sb_k, cache_sb_v, state_hgrn, c_prompt, c_sample, norm_gain, w_ada, b_ada, w_in, q_norm_gain, k_norm_gain, hgrn_lb_raw, hgrn_onorm_gain, w_branch_sb, w_branch_hgrn, w_out):
    assert w_in.shape[0] == 1, "single-layer trunk"
    n_p, t_p, d = x_prompt.shape
    n_s, t_s, _ = x_sample.shape
    past = cache_sb_k.shape[2]

    c_all = jnp.concatenate([c_prompt, c_sample], axis=0)
    pad = (-c_all.shape[0]) % 8
    c_all = jnp.pad(c_all, ((0, pad), (0, 0)))
    mod = _ada_call(c_all, w_ada[0], b_ada[0].reshape(1, 3 * d))
    mods = lambda lo, hi: tuple(mod[lo:hi, i * d:(i + 1) * d].reshape(hi - lo, 1, d) for i in range(3))

    p = {
        "norm_gain": norm_gain[0].reshape(1, 1, d),
        "w_in": w_in[0],
        "q_gain": q_norm_gain[0].reshape(1, HEAD_DIM),
        "k_gain": k_norm_gain[0].reshape(1, HEAD_DIM),
        "lb_raw": hgrn_lb_raw,
        "onorm_gain": hgrn_onorm_gain[0].reshape(N_HEADS, 1, HEAD_DIM),
        "w_br_sb": w_branch_sb[0].astype(BF16),
        "w_br_hg": w_branch_hgrn[0].astype(BF16),
        "w_out": w_out[0].astype(BF16),
    }

    return _layer(x_prompt, x_sample, mods(0, n_p), mods(n_p, n_p + n_s), p, state_hgrn[0],
                  (cache_sb_k.reshape(n_s, past * N_HEADS, HEAD_DIM), cache_sb_v.reshape(n_s, past * N_HEADS, HEAD_DIM)))
```

```python
import functools
import math

import jax
import jax.numpy as jnp
from jax import lax
from jax.experimental import pallas as pl
from jax.experimental.pallas import tpu as pltpu

F32 = jnp.float32
BF16 = jnp.bfloat16

N_HEADS = 8
HEAD_DIM = 128
WIDTH = N_HEADS * HEAD_DIM
HG_SUB = 16
EPS = 1e-6
SB_BLOCK = 128
SB_LOG_CUTOFF = -88.0
VMEM_LIMIT = 56 * 1024 * 1024


def _cparams(*sem):
    return pltpu.CompilerParams(dimension_semantics=sem, vmem_limit_bytes=VMEM_LIMIT)


def _silu(x):
    return x * jax.nn.sigmoid(x)


def _ada_kernel(c_ref, w_ref, b_ref, o_ref):
    c = c_ref[...]
    a = _silu(c).astype(BF16)
    o_ref[...] = jnp.dot(a, w_ref[...].astype(BF16), preferred_element_type=F32) + b_ref[...]


def _ada_call(c, w, b):
    r, d = c.shape
    n = w.shape[1]
    tn = 1024
    return pl.pallas_call(
        _ada_kernel,
        grid=(n // tn,),
        in_specs=[pl.BlockSpec((r, d), lambda j: (0, 0)),
                  pl.BlockSpec((d, tn), lambda j: (0, j)),
                  pl.BlockSpec((1, tn), lambda j: (0, j))],
        out_specs=pl.BlockSpec((r, tn), lambda j: (0, j)),
        out_shape=jax.ShapeDtypeStruct((r, n), F32),
        compiler_params=_cparams("arbitrary"),
        name="ada_mod",
    )(c, w, b)


def _prenorm_kernel(x_ref, g_ref, sc_ref, sh_ref, hall_ref, h_ref):
    del hall_ref
    x = x_ref[...]
    ms = jnp.mean(x * x, axis=-1, keepdims=True)
    xn = x * lax.rsqrt(ms + EPS)
    h = xn * g_ref[...] * (1.0 + sc_ref[...]) + sh_ref[...]
    h_ref[...] = h.astype(BF16).reshape(h_ref.shape)


def _prenorm_call(x, gain, scale, shift, nb, tr, h_all, row0):
    n, t, d = x.shape
    tm = nb * tr
    assert row0 % tm == 0 and (nb == 1 or tr == t)
    vec = pl.BlockSpec((nb, 1, d), lambda i, j: (i, 0, 0))
    return pl.pallas_call(
        _prenorm_kernel,
        grid=(n // nb, t // tr),
        in_specs=[pl.BlockSpec((nb, tr, d), lambda i, j: (i, j, 0)),
                  pl.BlockSpec((1, 1, d), lambda i, j: (0, 0, 0)), vec, vec,
                  pl.BlockSpec(memory_space=pl.ANY)],
        out_specs=pl.BlockSpec((tm, d), lambda i, j: (row0 // tm + i * (t // tr) + j, 0)),
        out_shape=jax.ShapeDtypeStruct(h_all.shape, BF16),
        input_output_aliases={4: 0},
        compiler_params=_cparams("arbitrary", "arbitrary"),
        name="prenorm",
    )(x, gain, scale, shift, h_all)


def _head_rms(y, gain):
    outs = []
    for g in range(N_HEADS):
        yh = y[:, g * HEAD_DIM:(g + 1) * HEAD_DIM]
        ms = jnp.mean(yh * yh, axis=-1, keepdims=True)
        outs.append(yh * lax.rsqrt(ms + EPS) * gain)
    return outs


def _proj_kernel(*refs, kind):
    h_ref, w_ref = refs[0], refs[1]
    wb_ref = refs[-1]

    @pl.when(pl.program_id(1) == 0)
    def _():
        wb_ref[...] = w_ref[...].astype(BF16)

    y = jnp.dot(h_ref[...], wb_ref[...], preferred_element_type=F32)
    tm = y.shape[0]
    if kind == "plain":
        refs[2][...] = y
    elif kind == "plain_bf16":
        refs[2][...] = y.astype(BF16)
    elif kind == "silu":
        refs[2][...] = _silu(y)
    elif kind == "copy2":
        for g in range(N_HEADS):
            refs[2][pl.ds(g, tm, stride=N_HEADS), :] = y[:, g * HEAD_DIM:(g + 1) * HEAD_DIM]
        refs[3][...] = y.astype(BF16)
    elif kind == "norm_k":
        gain = refs[2][...]
        for g, o in enumerate(_head_rms(y, gain)):
            refs[3][pl.ds(g, tm, stride=N_HEADS), :] = o
            refs[4][:, g * HEAD_DIM:(g + 1) * HEAD_DIM] = o.astype(BF16)
    elif kind == "forget":
        raw = refs[2][...]
        e = jnp.exp(raw - jnp.max(raw, axis=0, keepdims=True))
        lb = e[0:1, :] / jnp.sum(e, axis=0, keepdims=True)
        f = lb + (1.0 - lb) * jax.nn.sigmoid(y)
        refs[3][...] = jnp.log(f)
        refs[4][...] = 1.0 - f
    else:
        raise ValueError(kind)


def _proj_call(h, w_in, col0, ncols, kind, extra=(), *, tm=512, tn=1024, row0=0, rows=None):
    d = h.shape[1]
    rows = h.shape[0] - row0 if rows is None else rows
    assert col0 % tn == 0 and ncols % tn == 0 and rows % tm == 0 and row0 % tm == 0
    jb, ib = col0 // tn, row0 // tm
    grid = (ncols // tn, rows // tm)
    tile = lambda: pl.BlockSpec((tm, tn), lambda j, i: (i, j))
    in_specs = [pl.BlockSpec((tm, d), lambda j, i: (ib + i, 0)),
                pl.BlockSpec((d, tn), lambda j, i: (0, jb + j))]
    for e in extra:
        in_specs.append(pl.BlockSpec(e.shape, lambda j, i: (0, 0)))
    if kind in ("plain", "silu"):
        out_dt = (F32,)
    elif kind == "plain_bf16":
        out_dt = (BF16,)
    elif kind in ("copy2", "norm_k"):
        out_dt = (F32, BF16)
    else:
        out_dt = (F32, F32)
    out_specs = [tile() for _ in out_dt]
    out_shape = [jax.ShapeDtypeStruct((rows, ncols), dt) for dt in out_dt]
    if kind in ("copy2", "norm_k"):
        assert ncols == WIDTH
        out_specs[0] = pl.BlockSpec((tm * N_HEADS, HEAD_DIM), lambda j, i: (i, 0))
        out_shape[0] = jax.ShapeDtypeStruct((rows * N_HEADS, HEAD_DIM), F32)
    outs = pl.pallas_call(
        functools.partial(_proj_kernel, kind=kind),
        grid=grid,
        in_specs=in_specs,
        out_specs=out_specs,
        out_shape=out_shape,
        scratch_shapes=[pltpu.VMEM((d, tn), BF16)],
        compiler_params=_cparams("arbitrary", "arbitrary"),
        name="proj_" + kind,
    )(h, w_in, *extra)
    return outs


def _proj_qkv_kernel(h_ref, w_ref, qg_ref, kg_ref, q_ref, k_ref, kb_ref, v_ref, vb_ref):
    j = pl.program_id(1)
    y = jnp.dot(h_ref[...], w_ref[...].astype(BF16), preferred_element_type=F32)
    tm = y.shape[0]
    heads = [slice(g * HEAD_DIM, (g + 1) * HEAD_DIM) for g in range(N_HEADS)]

    @pl.when(j == 0)
    def _():
        for g, o in enumerate(_head_rms(y, qg_ref[...])):
            q_ref[:, heads[g]] = o.astype(BF16)

    @pl.when(j == 1)
    def _():
        for g, o in enumerate(_head_rms(y, kg_ref[...])):
            k_ref[pl.ds(g, tm, stride=N_HEADS), :] = o
            kb_ref[:, heads[g]] = o.astype(BF16)

    @pl.when(j == 2)
    def _():
        for g in range(N_HEADS):
            v_ref[pl.ds(g, tm, stride=N_HEADS), :] = y[:, heads[g]]
        vb_ref[...] = y.astype(BF16)


def _proj_qkv_call(h, w_in, q_gain, k_gain, *, row0, rows, tm=512):
    d = h.shape[1]
    tm = min(tm, rows)
    assert rows % tm == 0 and row0 % tm == 0
    ib = row0 // tm
    wide = lambda: pl.BlockSpec((tm, WIDTH), lambda i, j: (i, 0))
    tall = lambda: pl.BlockSpec((tm * N_HEADS, HEAD_DIM), lambda i, j: (i, 0))
    gain = lambda g: pl.BlockSpec(g.shape, lambda i, j: (0, 0))
    return pl.pallas_call(
        _proj_qkv_kernel,
        grid=(rows // tm, 3),
        in_specs=[pl.BlockSpec((tm, d), lambda i, j: (ib + i, 0)),
                  pl.BlockSpec((d, WIDTH), lambda i, j: (0, j)), gain(q_gain), gain(k_gain)],
        out_specs=[wide(), tall(), wide(), tall(), wide()],
        out_shape=[jax.ShapeDtypeStruct((rows, WIDTH), BF16),
                   jax.ShapeDtypeStruct((rows * N_HEADS, HEAD_DIM), F32), jax.ShapeDtypeStruct((rows, WIDTH), BF16),
                   jax.ShapeDtypeStruct((rows * N_HEADS, HEAD_DIM), F32), jax.ShapeDtypeStruct((rows, WIDTH), BF16)],
        compiler_params=_cparams("arbitrary", "arbitrary"),
        name="proj_qkv",
    )(h, w_in, q_gain, k_gain)


def _prenorm_q_kernel(x_ref, g_ref, sc_ref, sh_ref, w_ref, qg_ref, h_ref, q_ref, wb_ref, h2_ref, *, nrow, extra):
    s = pl.program_id(0)

    @pl.when(s == 0)
    def _():
        wb_ref[...] = w_ref[...].astype(BF16)
        h2_ref[1] = jnp.zeros(h2_ref.shape[1:], BF16)

    y = jnp.dot(h2_ref[(s + 1) % 2], wb_ref[...], preferred_element_type=F32)
    for g, o in enumerate(_head_rms(y, qg_ref[...])):
        q_ref[:, g * HEAD_DIM:(g + 1) * HEAD_DIM] = o.astype(BF16)

    x = x_ref[0]
    ms = jnp.mean(x * x, axis=-1, keepdims=True)
    hn = (x * lax.rsqrt(ms + EPS) * g_ref[0] * (1.0 + sc_ref[0]) + sh_ref[0]).astype(BF16)
    h2_ref[s % 2] = hn
    h_ref[...] = jnp.where(s < nrow, hn, jnp.zeros_like(hn)) if extra else hn


def _prenorm_q_call(x, gain, scale, shift, w_in, q_gain, *, tm, extra_rows=0):
    n, t, d = x.shape
    assert n == 1 and t % tm == 0 and extra_rows % tm == 0
    nrow, extra = t // tm, extra_rows // tm
    steps = nrow + max(1, extra)
    this = lambda s: jnp.minimum(s, nrow - 1)
    prev = lambda s: jnp.minimum(jnp.maximum(s - 1, 0), nrow - 1)
    vec = pl.BlockSpec((1, 1, d), lambda s: (0, 0, 0))
    return pl.pallas_call(
        functools.partial(_prenorm_q_kernel, nrow=nrow, extra=extra),
        grid=(steps,),
        in_specs=[pl.BlockSpec((1, tm, d), lambda s: (0, this(s), 0)), vec, vec, vec,
                  pl.BlockSpec((d, WIDTH), lambda s: (0, 0), pipeline_mode=pl.Buffered(1)),
                  pl.BlockSpec(q_gain.shape, lambda s: (0, 0))],
        out_specs=[pl.BlockSpec((tm, d), lambda s: (jnp.minimum(s, nrow - 1 + extra), 0)),
                   pl.BlockSpec((tm, WIDTH), lambda s: (prev(s), 0))],
        out_shape=[jax.ShapeDtypeStruct((t + extra_rows, d), BF16), jax.ShapeDtypeStruct((t, WIDTH), BF16)],
        scratch_shapes=[pltpu.VMEM((d, WIDTH), BF16), pltpu.VMEM((2, tm, d), BF16)],
        compiler_params=_cparams("arbitrary"),
        name="prenorm_q",
    )(x, gain, scale, shift, w_in, q_gain)


def _suffix_matrix(bk):
    j = lax.broadcasted_iota(jnp.int32, (2 * bk, 2 * bk), 0) % bk
    s = lax.broadcasted_iota(jnp.int32, (2 * bk, 2 * bk), 1)
    return jnp.where((j > s) | (s >= bk), -1.0, 0.0).astype(BF16)


def _sb_tiles(qs, ks, vs, carries, sfx, masks=None, valid=None):
    bk = sfx.shape[0] // 2
    n = range(len(qs))
    spans = [range(ks[i].shape[0] // bk) for i in n]
    lanes = lambda x, t: x[:, t * bk:(t + 1) * bk]
    mask_of = lambda i, t: None if masks is None or masks[i] is None else masks[i][t]
    zs = [lax.dot_general(qs[i], ks[i], (((1,), (1,)), ((), ())), preferred_element_type=F32) * HEAD_DIM ** -0.5
          for i in n]
    sps = [jnp.maximum(z, 0.0) + jnp.log(1.0 + jnp.exp(-jnp.abs(z))) for z in zs]
    r2s = []
    for i in n:
        r2 = []
        for t in spans[i]:
            m = mask_of(i, t)
            l1m = lanes(sps[i], t) if m is None else jnp.where(m, lanes(sps[i], t), 0.0)
            hi = l1m.astype(BF16)
            lo = (l1m - hi.astype(F32)).astype(BF16)
            r2.append(jnp.dot(jnp.concatenate([hi, lo], axis=1), sfx, preferred_element_type=F32))
        r2s.append(r2)
    new, wss = [], []
    for i in n:
        c = carries[i]
        ws = []
        for t in spans[i]:
            w = jnp.exp(lanes(zs[i], t) - lanes(sps[i], t) + r2s[i][t][:, :bk] + c)
            m = mask_of(i, t)
            if m is not None:
                w = jnp.where(m, w, 0.0)
            if valid is not None and valid[i][t] is not None:
                w = jnp.where(valid[i][t], w, 0.0)
            ws.append(w.astype(BF16))
            c = c + r2s[i][t][:, bk:]
        new.append(c)
        wss.append(ws[0] if len(ws) == 1 else jnp.concatenate(ws, axis=1))
    pvs = [jnp.dot(wss[i], vs[i], preferred_element_type=F32) for i in n]
    return new, pvs


def _sb_prompt_kernel(q_ref, k_ref, v_ref, z_ref, o_ref, c_scr, acc_scr, *, n_groups, group, ahead):
    blk = SB_BLOCK
    qt = pl.program_id(1)
    sfx = _suffix_matrix(blk)
    row = lax.broadcasted_iota(jnp.int32, (blk, blk), 0)
    col = lax.broadcasted_iota(jnp.int32, (blk, blk), 1)
    causal = col < row
    alive = lambda cs: (functools.reduce(jnp.maximum, [jnp.max(c) for c in cs]) >= SB_LOG_CUTOFF).astype(jnp.int32)

    def kv(kb):
        start = pl.multiple_of(kb * blk, blk)
        return k_ref[pl.ds(start, blk), :], v_ref[pl.ds(start, blk), :]

    def qgroup(ig, _):
        gq0 = (qt * n_groups + ig) * group
        rows = [pl.ds(pl.multiple_of((ig * group + g) * blk, blk), blk) for g in range(group)]
        qs = [q_ref[r, :] for r in rows]
        kbs = [[gq0 + g - s for s in range(1 + ahead)] for g in range(group)]
        kvs = [[kv(jnp.maximum(kb, 0)) for kb in kbs[g]] for g in range(group)]
        cat = lambda xs: jnp.concatenate(xs, axis=0)
        cs, pvs = _sb_tiles(qs, [cat([k for k, _ in kvs[g]]) for g in range(group)],
                            [cat([v for _, v in kvs[g]]) for g in range(group)],
                            [jnp.zeros((blk, blk), F32)] * group, sfx,
                            masks=[[causal] + [None] * ahead] * group,
                            valid=[[None] + [kb >= 0 for kb in kbs[g][1:]] for g in range(group)])
        for g in range(group):
            c_scr[g] = cs[g]
            acc_scr[g] = pvs[g]

        def cond(st):
            s, go = st
            return jnp.logical_and(s <= gq0 + group - 1, go > 0)

        def body(st):
            s, _ = st
            kbs = [gq0 + g - s for g in range(group)]
            kvs = [kv(jnp.maximum(kb, 0)) for kb in kbs]
            cs, pvs = _sb_tiles(qs, [k for k, _ in kvs], [v for _, v in kvs],
                                [c_scr[g] for g in range(group)], sfx, valid=[[kb >= 0] for kb in kbs])
            for g in range(group):
                c_scr[g] = cs[g]
                acc_scr[g] += pvs[g]
            return s + 1, alive(cs)

        lax.while_loop(cond, body, (1 + ahead, alive(cs)))
        for g in range(group):
            o_ref[rows[g], :] = (acc_scr[g] * _silu(z_ref[rows[g], :])).astype(BF16)
        return 0

    lax.fori_loop(0, n_groups, qgroup, 0)


def _sb_prompt_call(q, k, v, z, *, tq=8192, group=32, ahead=2):
    t = q.shape[0]
    tq = min(tq, t)
    group = math.gcd(group, tq // SB_BLOCK)
    assert t % tq == 0 and tq % (SB_BLOCK * group) == 0
    qspec = pl.BlockSpec((tq, HEAD_DIM), lambda h, i: (i, h))
    kvspec = pl.BlockSpec((t, HEAD_DIM), lambda h, i: (0, h))
    return pl.pallas_call(
        functools.partial(_sb_prompt_kernel, n_groups=tq // (SB_BLOCK * group), group=group, ahead=ahead),
        grid=(N_HEADS, t // tq),
        in_specs=[qspec, kvspec, kvspec, qspec],
        out_specs=qspec,
        out_shape=jax.ShapeDtypeStruct((t, WIDTH), BF16),
        scratch_shapes=[pltpu.VMEM((group, SB_BLOCK, SB_BLOCK), F32), pltpu.VMEM((group, SB_BLOCK, HEAD_DIM), F32)],
        compiler_params=_cparams("arbitrary", "arbitrary"),
        name="sb_prompt",
    )(q, k, v, z)


def _sb_sample_kernel(q_ref, kn_ref, vn_ref, z_ref, kc_hbm, vc_hbm, o_ref, kbuf, vbuf, sem, c_scr, acc_scr, *, past):
    blk = SB_BLOCK
    nh = N_HEADS
    b = pl.program_id(0)
    tq = q_ref.shape[0]
    half = blk - tq
    n_full = (past - half) // blk
    rem = (past - half) % blk
    sfx = _suffix_matrix(blk)
    row = lax.broadcasted_iota(jnp.int32, (tq, blk), 0)
    col = lax.broadcasted_iota(jnp.int32, (tq, blk), 1)
    heads = [slice(h * HEAD_DIM, (h + 1) * HEAD_DIM) for h in range(nh)]
    alive = lambda cs: (functools.reduce(jnp.maximum, [jnp.max(c) for c in cs]) >= SB_LOG_CUTOFF).astype(jnp.int32)

    def copies(stream, key0, nkeys, slot):
        src = pl.ds(key0 * nh, nkeys * nh)
        dst = pl.ds(0, nkeys * nh)
        return (pltpu.make_async_copy(kc_hbm.at[stream, src, :], kbuf.at[slot, dst, :], sem.at[0, slot]),
                pltpu.make_async_copy(vc_hbm.at[stream, src, :], vbuf.at[slot, dst, :], sem.at[1, slot]))

    def start(cps):
        for cp in cps:
            cp.start()

    def wait(cps):
        for cp in cps:
            cp.wait()

    first = min(n_full, 1)
    base = 2 * (b % 2)

    def tile_copies(j):
        return copies(b, past - half - (j + 1) * blk, blk, base + (j + 1) % 2)

    def first_copies(stream):
        slot0 = 2 * (stream % 2)
        cps = copies(stream, past - half, half, slot0)
        return cps + copies(stream, past - half - blk, blk, slot0 + 1) if first else cps

    def cached(buf, slot, h, nkeys):
        return buf[slot, pl.ds(h, nkeys, stride=nh), :].astype(BF16)

    @pl.when(b == 0)
    def _():
        start(first_copies(b))

    @pl.when(b + 1 < pl.num_programs(0))
    def _():
        start(first_copies(b + 1))

    wait(first_copies(b))

    qs = [q_ref[:, heads[h]] for h in range(nh)]

    def span(buf, new_ref, h):
        tiles = [cached(buf, base, h, half), new_ref[:, heads[h]]] + [cached(buf, base + 1, h, blk)] * first
        return jnp.concatenate(tiles, axis=0)

    cs, pvs = _sb_tiles(qs, [span(kbuf, kn_ref, h) for h in range(nh)], [span(vbuf, vn_ref, h) for h in range(nh)],
                        [jnp.zeros((tq, blk), F32)] * nh, sfx, masks=[[col < row + half] + [None] * first] * nh)
    for h in range(nh):
        c_scr[h] = cs[h]
        acc_scr[h] = pvs[h]

    def sweep(slot, mask):
        cs, pvs = _sb_tiles(qs, [cached(kbuf, slot, h, blk) for h in range(nh)],
                            [cached(vbuf, slot, h, blk) for h in range(nh)],
                            [c_scr[h] for h in range(nh)], sfx, masks=None if mask is None else [[mask]] * nh)
        for h in range(nh):
            c_scr[h] = cs[h]
            acc_scr[h] += pvs[h]
        return cs

    def cond(st):
        j, go = st
        return jnp.logical_and(j < n_full, go > 0)

    def body(st):
        j, _ = st
        cps = tile_copies(j)
        start(cps)
        wait(cps)
        return j + 1, alive(sweep(base + (j + 1) % 2, None))

    _, go = lax.while_loop(cond, body, (first, alive(cs)))

    if rem:
        @pl.when(go > 0)
        def _():
            cps = copies(b, 0, blk, base)
            start(cps)
            wait(cps)
            sweep(base, col < rem)

    for h in range(nh):
        o_ref[:, heads[h]] = (acc_scr[h] * _silu(z_ref[:, heads[h]])).astype(BF16)


def _sb_sample_call(q, kn, vn, z, kc, vc, *, z_row0=0):
    nb = kc.shape[0]
    past = kc.shape[1] // N_HEADS
    tq = q.shape[0] // nb
    assert tq % 16 == 0 and tq < SB_BLOCK and past >= SB_BLOCK and z_row0 % tq == 0
    new = pl.BlockSpec((tq, WIDTH), lambda b: (b, 0))
    zspec = pl.BlockSpec((tq, WIDTH), lambda b: (z_row0 // tq + b, 0))
    hbm = pl.BlockSpec(memory_space=pl.ANY)
    return pl.pallas_call(
        functools.partial(_sb_sample_kernel, past=past),
        grid=(nb,),
        in_specs=[new, new, new, zspec, hbm, hbm],
        out_specs=new,
        out_shape=jax.ShapeDtypeStruct(q.shape, BF16),
        scratch_shapes=[pltpu.VMEM((4, SB_BLOCK * N_HEADS, HEAD_DIM), F32),
                        pltpu.VMEM((4, SB_BLOCK * N_HEADS, HEAD_DIM), F32),
                        pltpu.SemaphoreType.DMA((2, 4)),
                        pltpu.VMEM((N_HEADS, tq, SB_BLOCK), F32), pltpu.VMEM((N_HEADS, tq, HEAD_DIM), F32)],
        compiler_params=_cparams("arbitrary"),
        name="sb_sample",
    )(q, kn, vn, z, kc, vc)


def _prefix_matrix(c):
    t = lax.broadcasted_iota(jnp.int32, (2 * c, c), 0)
    s = lax.broadcasted_iota(jnp.int32, (2 * c, c), 1)
    incl = (t < c) & (s <= t)
    sub = (t >= c) & (s < ((t - c) // HG_SUB) * HG_SUB)
    return jnp.where(incl | sub, 1.0, 0.0).astype(BF16)


def _hgrn_front(lf, qh, kh, n_heads, pfx, tril):
    c = lf.shape[0]
    n_sub = c // HG_SUB
    heads = [slice(h * HEAD_DIM, (h + 1) * HEAD_DIM) for h in range(n_heads)]
    p0 = lf.astype(BF16)
    r1 = lf - p0.astype(F32)
    p1 = r1.astype(BF16)
    p2 = (r1 - p1.astype(F32)).astype(BF16)
    br = (jnp.dot(pfx, p0, preferred_element_type=F32) + jnp.dot(pfx, p1, preferred_element_type=F32)
          + jnp.dot(pfx, p2, preferred_element_type=F32))
    b = br[:c]
    r = br[c:]
    b_last = b[c - 1:c, :]
    q_sub = (qh * jnp.exp(b - r)).astype(BF16)
    q_dec = (qh * jnp.exp(b)).astype(BF16)
    k_end = (kh * jnp.exp(b_last - b)).astype(BF16)
    dec = jnp.exp(b_last)
    att = [[] for _ in heads]
    for i in range(n_sub):
        lo, hi = i * HG_SUB, (i + 1) * HG_SUB
        k_i = (kh[:hi] * jnp.exp(r[lo:lo + 1, :] - b[:hi])).astype(BF16)
        if hi < c:
            k_i = jnp.concatenate([k_i, jnp.zeros((c - hi, k_i.shape[1]), BF16)], axis=0)
        for h, hs in enumerate(heads):
            att[h].append(lax.dot_general(q_sub[lo:hi, hs], k_i[:, hs], (((1,), (1,)), ((), ())),
                                          preferred_element_type=F32))
    att = [jnp.where(tril, jnp.concatenate(a, axis=0), 0.0).astype(BF16) for a in att]
    return att, q_dec, k_end, dec


def _hgrn_back(front, v, sts):
    att, q_dec, k_end, dec = front
    heads = [slice(h * HEAD_DIM, (h + 1) * HEAD_DIM) for h in range(len(sts))]
    vb = v
    outs, new_sts = [], []
    for h, hs in enumerate(heads):
        o = jnp.dot(att[h], vb[:, hs], preferred_element_type=F32)
        o = o + lax.dot_general(q_dec[:, hs], sts[h].astype(BF16), (((1,), (1,)), ((), ())),
                                preferred_element_type=F32)
        outs.append(o)
    for h, hs in enumerate(heads):
        new_sts.append(sts[h] * dec[:, hs] + lax.dot_general(vb[:, hs], k_end[:, hs], (((0,), (0,)), ((), ())),
                                                             preferred_element_type=F32))
    return outs, new_sts


def _hgrn_kernel(lf_ref, qh_ref, kh_ref, v_ref, zh_ref, g_ref, s0_ref, o_ref, s_ref, st_scr, *, chunk, n_chunks, streams):
    tt = pl.program_id(2)
    nh = st_scr.shape[0] // streams
    tile = lf_ref.shape[0] // streams
    heads = [slice(h * HEAD_DIM, (h + 1) * HEAD_DIM) for h in range(nh)]

    @pl.when(tt == 0)
    def _():
        for s in range(streams):
            for h in range(nh):
                st_scr[s * nh + h] = s0_ref[s, h].T

    pfx = _prefix_matrix(chunk)
    ti = lax.broadcasted_iota(jnp.int32, (chunk, chunk), 0)
    si = lax.broadcasted_iota(jnp.int32, (chunk, chunk), 1)
    tril = si <= ti

    unroll = math.gcd(n_chunks, 4)

    def step(ci, _):
        rss = [[pl.ds(pl.multiple_of(s * tile + (ci * unroll + u) * chunk, chunk), chunk) for u in range(unroll)]
               for s in range(streams)]
        fronts = [[_hgrn_front(lf_ref[rs, :], qh_ref[rs, :], kh_ref[rs, :], nh, pfx, tril) for rs in rss[s]]
                  for s in range(streams)]
        for s in range(streams):
            sts = [st_scr[s * nh + h] for h in range(nh)]
            for rs, front in zip(rss[s], fronts[s]):
                outs, sts = _hgrn_back(front, v_ref[rs, :], sts)
                for h in range(nh):
                    o = outs[h]
                    ms = jnp.mean(o * o, axis=-1, keepdims=True)
                    o_ref[rs, heads[h]] = (o * lax.rsqrt(ms + EPS) * g_ref[h]
                                           * _silu(zh_ref[rs, heads[h]])).astype(BF16)
            for h in range(nh):
                st_scr[s * nh + h] = sts[h]
        return 0

    lax.fori_loop(0, n_chunks // unroll, step, 0)

    @pl.when(tt == pl.num_programs(2) - 1)
    def _():
        for s in range(streams):
            for h in range(nh):
                s_ref[s, h] = st_scr[s * nh + h].T


def _hgrn_call(lf, qh, kh, v, zh, gain, s0, *, t, chunk, tile, heads, row0=0, streams=1):
    nb = s0.shape[0]
    tile = min(tile, t)
    rows = streams * tile
    assert t % tile == 0 and tile % chunk == 0 and chunk % HG_SUB == 0 and N_HEADS % heads == 0 and row0 % rows == 0
    assert nb % streams == 0 and (streams == 1 or tile == t)
    nt = t // tile
    tok_in = pl.BlockSpec((rows, heads * HEAD_DIM), lambda b, h, i: (row0 // rows + b * nt + i, h))
    tok = pl.BlockSpec((rows, heads * HEAD_DIM), lambda b, h, i: (b * nt + i, h))
    state = pl.BlockSpec((streams, heads, HEAD_DIM, HEAD_DIM), lambda b, h, i: (b, h, 0, 0))
    return pl.pallas_call(
        functools.partial(_hgrn_kernel, chunk=chunk, n_chunks=tile // chunk, streams=streams),
        grid=(nb // streams, N_HEADS // heads, nt),
        in_specs=[tok_in] * 5 + [pl.BlockSpec((heads, 1, HEAD_DIM), lambda b, h, i: (h, 0, 0)), state],
        out_specs=[tok, state],
        out_shape=[jax.ShapeDtypeStruct((nb * t, WIDTH), BF16), jax.ShapeDtypeStruct(s0.shape, F32)],
        scratch_shapes=[pltpu.VMEM((streams * heads, HEAD_DIM, HEAD_DIM), F32)],
        compiler_params=_cparams("arbitrary", "arbitrary", "arbitrary"),
        name="hgrn2",
    )(lf, qh, kh, v, zh, gain, s0)


def _out_kernel(gs_ref, gh_ref, gsb_ref, ghg_ref, x_ref, gate_ref, wsb_ref, whg_ref, wo_ref, y_ref):
    nb, tr, d = x_ref.shape
    y_sb = jnp.dot(gs_ref[...], wsb_ref[...], preferred_element_type=F32)
    y_h = jnp.dot(gh_ref[...], whg_ref[...], preferred_element_type=F32)
    merged = jax.nn.sigmoid(gsb_ref[...]) * y_sb + jax.nn.sigmoid(ghg_ref[...]) * y_h
    upd = jnp.dot(merged.astype(BF16), wo_ref[...], preferred_element_type=F32)
    y_ref[...] = x_ref[...] + gate_ref[...] * upd.reshape(nb, tr, d)


def _out_call(gs, gh, gg, x, gate, wsb, whg, wo, nb, tr, *, gg_row0=0):
    n, t, d = x.shape
    tm = nb * tr
    nt = t // tr
    assert gg_row0 % tm == 0 and (nb == 1 or nt == 1)
    rowblk = lambda w, c, r0=0: pl.BlockSpec((tm, w), lambda i, j: (r0 // tm + i * nt + j, c))
    const = lambda a: pl.BlockSpec(a.shape, lambda i, j: (0, 0), pipeline_mode=pl.Buffered(1))
    return pl.pallas_call(
        _out_kernel,
        grid=(n // nb, nt),
        in_specs=[rowblk(WIDTH, 0), rowblk(WIDTH, 0), rowblk(d, 0, gg_row0), rowblk(d, 1, gg_row0),
                  pl.BlockSpec((nb, tr, d), lambda i, j: (i, j, 0)),
                  pl.BlockSpec((nb, 1, d), lambda i, j: (i, 0, 0)),
                  const(wsb), const(whg), const(wo)],
        out_specs=pl.BlockSpec((nb, tr, d), lambda i, j: (i, j, 0)),
        out_shape=jax.ShapeDtypeStruct(x.shape, F32),
        compiler_params=_cparams("arbitrary", "arbitrary"),
        name="merge_out",
    )(gs, gh, gg, gg, x, gate, wsb, whg, wo)


def _layer(x_p, x_s, mod_p, mod_s, p, s0_s, caches):
    n_p, t_p, d = x_p.shape
    n_s, t_s, _ = x_s.shape
    assert n_p == 1
    rows_p, rows_s = t_p, n_s * t_s
    tm = math.gcd(1024, rows_p, rows_s)
    w_in = p["w_in"]

    h, q_p = _prenorm_q_call(x_p, p["norm_gain"], mod_p[1], mod_p[0], w_in, p["q_gain"], tm=tm, extra_rows=rows_s)
    pn_tr = min(1024, t_s)
    pn_nb = max(1, min(n_s, 1024 // pn_tr))
    h = _prenorm_call(x_s, p["norm_gain"], mod_s[1], mod_s[0], pn_nb, pn_tr, h, rows_p)

    proj = functools.partial(_proj_call, h, w_in, tm=tm)
    prompt, sample = dict(row0=0, rows=rows_p), dict(row0=rows_p, rows=rows_s)
    q_s, k_s, kb_s, v_s, vb_s = _proj_qkv_call(h, w_in, p["q_gain"], p["k_gain"], **sample)
    k_p, kb_p = proj(1 * WIDTH, WIDTH, "norm_k", (p["k_gain"],), **prompt)
    v_p, vb_p = proj(2 * WIDTH, WIDTH, "copy2", **prompt)
    (z_sb,) = proj(3 * WIDTH, WIDTH, "plain")
    logf, k_h = proj(4 * WIDTH, WIDTH, "forget", (p["lb_raw"],))
    (i_h,) = proj(5 * WIDTH, WIDTH, "plain_bf16")
    (q_h,) = proj(6 * WIDTH, WIDTH, "silu")
    (z_h,) = proj(7 * WIDTH, WIDTH, "plain")
    (gg,) = proj(8 * WIDTH, 2 * d, "plain")

    gs_p = _sb_prompt_call(q_p, kb_p, vb_p, z_sb)
    gs_s = _sb_sample_call(q_s, kb_s, vb_s, z_sb, caches[0], caches[1], z_row0=rows_p)
    hgrn = functools.partial(_hgrn_call, logf, q_h, k_h, i_h, z_h, p["onorm_gain"])
    gh_p, s_p = hgrn(jnp.zeros((n_p, N_HEADS, HEAD_DIM, HEAD_DIM), F32), t=t_p, chunk=min(128, t_p), tile=1024,
                     heads=N_HEADS)
    gh_s, s_s = hgrn(s0_s, t=t_s, chunk=t_s, tile=t_s, heads=N_HEADS, row0=rows_p, streams=math.gcd(n_s, 4))
    out = functools.partial(_out_call, wsb=p["w_br_sb"], whg=p["w_br_hg"], wo=p["w_out"])
    y_p = out(gs_p, gh_p, gg, x_p, mod_p[2], nb=1, tr=min(256, t_p))
    y_s = out(gs_s, gh_s, gg, x_s, mod_s[2], nb=max(1, min(n_s, 256 // t_s)), tr=t_s, gg_row0=rows_p)
    heads5 = lambda a, n, t: a.reshape(1, n, t, N_HEADS, HEAD_DIM)
    return (y_p, y_s, heads5(k_p, n_p, t_p), heads5(v_p, n_p, t_p), s_p[None],
            heads5(k_s, n_s, t_s), heads5(v_s, n_s, t_s), s_s[None])


def kernel(x_prompt, x_sample, cache_sb_k, cache_sb_v, state_hgrn, c_prompt, c_sample, norm_gain, w_ada, b_ada, w_in, q_norm_gain, k_norm_gain, hgrn_lb_raw, hgrn_onorm_gain, w_branch_sb, w_branch_hgrn, w_out):
    assert w_in.shape[0] == 1, "single-layer trunk"
    n_p, t_p, d = x_prompt.shape
    n_s, t_s, _ = x_sample.shape
    past = cache_sb_k.shape[2]

    c_all = jnp.concatenate([c_prompt, c_sample], axis=0)
    pad = (-c_all.shape[0]) % 8
    c_all = jnp.pad(c_all, ((0, pad), (0, 0)))
    mod = _ada_call(c_all, w_ada[0], b_ada[0].reshape(1, 3 * d))
    mods = lambda lo, hi: tuple(mod[lo:hi, i * d:(i + 1) * d].reshape(hi - lo, 1, d) for i in range(3))

    p = {
        "norm_gain": norm_gain[0].reshape(1, 1, d),
        "w_in": w_in[0],
        "q_gain": q_norm_gain[0].reshape(1, HEAD_DIM),
        "k_gain": k_norm_gain[0].reshape(1, HEAD_DIM),
        "lb_raw": hgrn_lb_raw,
        "onorm_gain": hgrn_onorm_gain[0].reshape(N_HEADS, 1, HEAD_DIM),
        "w_br_sb": w_branch_sb[0].astype(BF16),
        "w_br_hg": w_branch_hgrn[0].astype(BF16),
        "w_out": w_out[0].astype(BF16),
    }

    return _layer(x_prompt, x_sample, mods(0, n_p), mods(n_p, n_p + n_s), p, state_hgrn[0],
                  (cache_sb_k.reshape(n_s, past * N_HEADS, HEAD_DIM), cache_sb_v.reshape(n_s, past * N_HEADS, HEAD_DIM)))
```

```python
import functools
import math

import jax
import jax.numpy as jnp
from jax import lax
from jax.experimental import pallas as pl
from jax.experimental.pallas import tpu as pltpu

F32 = jnp.float32
BF16 = jnp.bfloat16

N_HEADS = 8
HEAD_DIM = 128
WIDTH = N_HEADS * HEAD_DIM
HG_SUB = 16
EPS = 1e-6
SB_BLOCK = 128
SB_LOG_CUTOFF = -88.0
VMEM_LIMIT = 56 * 1024 * 1024


def _cparams(*sem):
    return pltpu.CompilerParams(dimension_semantics=sem, vmem_limit_bytes=VMEM_LIMIT)


def _silu(x):
    return x * jax.nn.sigmoid(x)


def _ada_kernel(c_ref, w_ref, b_ref, o_ref):
    c = c_ref[...]
    a = _silu(c).astype(BF16)
    o_ref[...] = jnp.dot(a, w_ref[...].astype(BF16), preferred_element_type=F32) + b_ref[...]


def _ada_call(c, w, b):
    r, d = c.shape
    n = w.shape[1]
    tn = 1024
    return pl.pallas_call(
        _ada_kernel,
        grid=(n // tn,),
        in_specs=[pl.BlockSpec((r, d), lambda j: (0, 0)),
                  pl.BlockSpec((d, tn), lambda j: (0, j)),
                  pl.BlockSpec((1, tn), lambda j: (0, j))],
        out_specs=pl.BlockSpec((r, tn), lambda j: (0, j)),
        out_shape=jax.ShapeDtypeStruct((r, n), F32),
        compiler_params=_cparams("arbitrary"),
        name="ada_mod",
    )(c, w, b)


def _prenorm_kernel(x_ref, g_ref, sc_ref, sh_ref, hall_ref, h_ref):
    del hall_ref
    x = x_ref[...]
    ms = jnp.mean(x * x, axis=-1, keepdims=True)
    xn = x * lax.rsqrt(ms + EPS)
    h = xn * g_ref[...] * (1.0 + sc_ref[...]) + sh_ref[...]
    h_ref[...] = h.astype(BF16).reshape(h_ref.shape)


def _prenorm_call(x, gain, scale, shift, nb, tr, h_all, row0):
    n, t, d = x.shape
    tm = nb * tr
    assert row0 % tm == 0 and (nb == 1 or tr == t)
    vec = pl.BlockSpec((nb, 1, d), lambda i, j: (i, 0, 0))
    return pl.pallas_call(
        _prenorm_kernel,
        grid=(n // nb, t // tr),
        in_specs=[pl.BlockSpec((nb, tr, d), lambda i, j: (i, j, 0)),
                  pl.BlockSpec((1, 1, d), lambda i, j: (0, 0, 0)), vec, vec,
                  pl.BlockSpec(memory_space=pl.ANY)],
        out_specs=pl.BlockSpec((tm, d), lambda i, j: (row0 // tm + i * (t // tr) + j, 0)),
        out_shape=jax.ShapeDtypeStruct(h_all.shape, BF16),
        input_output_aliases={4: 0},
        compiler_params=_cparams("arbitrary", "arbitrary"),
        name="prenorm",
    )(x, gain, scale, shift, h_all)


def _head_rms(y, gain):
    outs = []
    for g in range(N_HEADS):
        yh = y[:, g * HEAD_DIM:(g + 1) * HEAD_DIM]
        ms = jnp.mean(yh * yh, axis=-1, keepdims=True)
        outs.append(yh * lax.rsqrt(ms + EPS) * gain)
    return outs


def _proj_kernel(*refs, kind):
    h_ref, w_ref = refs[0], refs[1]
    wb_ref = refs[-1]

    @pl.when(pl.program_id(1) == 0)
    def _():
        wb_ref[...] = w_ref[...].astype(BF16)

    y = jnp.dot(h_ref[...], wb_ref[...], preferred_element_type=F32)
    tm = y.shape[0]
    if kind == "plain":
        refs[2][...] = y
    elif kind == "plain_bf16":
        refs[2][...] = y.astype(BF16)
    elif kind == "silu":
        refs[2][...] = _silu(y)
    elif kind == "copy2":
        for g in range(N_HEADS):
            refs[2][pl.ds(g, tm, stride=N_HEADS), :] = y[:, g * HEAD_DIM:(g + 1) * HEAD_DIM]
        refs[3][...] = y.astype(BF16)
    elif kind == "norm_k":
        gain = refs[2][...]
        for g, o in enumerate(_head_rms(y, gain)):
            refs[3][pl.ds(g, tm, stride=N_HEADS), :] = o
            refs[4][:, g * HEAD_DIM:(g + 1) * HEAD_DIM] = o.astype(BF16)
    elif kind == "forget":
        raw = refs[2][...]
        e = jnp.exp(raw - jnp.max(raw, axis=0, keepdims=True))
        lb = e[0:1, :] / jnp.sum(e, axis=0, keepdims=True)
        f = lb + (1.0 - lb) * jax.nn.sigmoid(y)
        refs[3][...] = jnp.log(f)
        refs[4][...] = 1.0 - f
    else:
        raise ValueError(kind)


def _proj_call(h, w_in, col0, ncols, kind, extra=(), *, tm=512, tn=1024, row0=0, rows=None):
    d = h.shape[1]
    rows = h.shape[0] - row0 if rows is None else rows
    assert col0 % tn == 0 and ncols % tn == 0 and rows % tm == 0 and row0 % tm == 0
    jb, ib = col0 // tn, row0 // tm
    grid = (ncols // tn, rows // tm)
    tile = lambda: pl.BlockSpec((tm, tn), lambda j, i: (i, j))
    in_specs = [pl.BlockSpec((tm, d), lambda j, i: (ib + i, 0)),
                pl.BlockSpec((d, tn), lambda j, i: (0, jb + j))]
    for e in extra:
        in_specs.append(pl.BlockSpec(e.shape, lambda j, i: (0, 0)))
    if kind in ("plain", "silu"):
        out_dt = (F32,)
    elif kind == "plain_bf16":
        out_dt = (BF16,)
    elif kind in ("copy2", "norm_k"):
        out_dt = (F32, BF16)
    else:
        out_dt = (F32, F32)
    out_specs = [tile() for _ in out_dt]
    out_shape = [jax.ShapeDtypeStruct((rows, ncols), dt) for dt in out_dt]
    if kind in ("copy2", "norm_k"):
        assert ncols == WIDTH
        out_specs[0] = pl.BlockSpec((tm * N_HEADS, HEAD_DIM), lambda j, i: (i, 0))
        out_shape[0] = jax.ShapeDtypeStruct((rows * N_HEADS, HEAD_DIM), F32)
    outs = pl.pallas_call(
        functools.partial(_proj_kernel, kind=kind),
        grid=grid,
        in_specs=in_specs,
        out_specs=out_specs,
        out_shape=out_shape,
        scratch_shapes=[pltpu.VMEM((d, tn), BF16)],
        compiler_params=_cparams("arbitrary", "arbitrary"),
        name="proj_" + kind,
    )(h, w_in, *extra)
    return outs


def _proj_qkv_kernel(h_ref, w_ref, qg_ref, kg_ref, q_ref, k_ref, kb_ref, v_ref, vb_ref):
    j = pl.program_id(1)
    y = jnp.dot(h_ref[...], w_ref[...].astype(BF16), preferred_element_type=F32)
    tm = y.shape[0]
    heads = [slice(g * HEAD_DIM, (g + 1) * HEAD_DIM) for g in range(N_HEADS)]

    @pl.when(j == 0)
    def _():
        for g, o in enumerate(_head_rms(y, qg_ref[...])):
            q_ref[:, heads[g]] = o.astype(BF16)

    @pl.when(j == 1)
    def _():
        for g, o in enumerate(_head_rms(y, kg_ref[...])):
            k_ref[pl.ds(g, tm, stride=N_HEADS), :] = o
            kb_ref[:, heads[g]] = o.astype(BF16)

    @pl.when(j == 2)
    def _():
        for g in range(N_HEADS):
            v_ref[pl.ds(g, tm, stride=N_HEADS), :] = y[:, heads[g]]
        vb_ref[...] = y.astype(BF16)


def _proj_qkv_call(h, w_in, q_gain, k_gain, *, row0, rows, tm=512):
    d = h.shape[1]
    tm = min(tm, rows)
    assert rows % tm == 0 and row0 % tm == 0
    ib = row0 // tm
    wide = lambda: pl.BlockSpec((tm, WIDTH), lambda i, j: (i, 0))
    tall = lambda: pl.BlockSpec((tm * N_HEADS, HEAD_DIM), lambda i, j: (i, 0))
    gain = lambda g: pl.BlockSpec(g.shape, lambda i, j: (0, 0))
    return pl.pallas_call(
        _proj_qkv_kernel,
        grid=(rows // tm, 3),
        in_specs=[pl.BlockSpec((tm, d), lambda i, j: (ib + i, 0)),
                  pl.BlockSpec((d, WIDTH), lambda i, j: (0, j)), gain(q_gain), gain(k_gain)],
        out_specs=[wide(), tall(), wide(), tall(), wide()],
        out_shape=[jax.ShapeDtypeStruct((rows, WIDTH), BF16),
                   jax.ShapeDtypeStruct((rows * N_HEADS, HEAD_DIM), F32), jax.ShapeDtypeStruct((rows, WIDTH), BF16),
                   jax.ShapeDtypeStruct((rows * N_HEADS, HEAD_DIM), F32), jax.ShapeDtypeStruct((rows, WIDTH), BF16)],
        compiler_params=_cparams("arbitrary", "arbitrary"),
        name="proj_qkv",
    )(h, w_in, q_gain, k_gain)


def _prenorm_q_kernel(x_ref, g_ref, sc_ref, sh_ref, w_ref, qg_ref, h_ref, q_ref, wb_ref, h2_ref, *, nrow, extra):
    s = pl.program_id(0)

    @pl.when(s == 0)
    def _():
        wb_ref[...] = w_ref[...].astype(BF16)
        h2_ref[1] = jnp.zeros(h2_ref.shape[1:], BF16)

    y = jnp.dot(h2_ref[(s + 1) % 2], wb_ref[...], preferred_element_type=F32)
    for g, o in enumerate(_head_rms(y, qg_ref[...])):
        q_ref[:, g * HEAD_DIM:(g + 1) * HEAD_DIM] = o.astype(BF16)

    x = x_ref[0]
    ms = jnp.mean(x * x, axis=-1, keepdims=True)
    hn = (x * lax.rsqrt(ms + EPS) * g_ref[0] * (1.0 + sc_ref[0]) + sh_ref[0]).astype(BF16)
    h2_ref[s % 2] = hn
    h_ref[...] = jnp.where(s < nrow, hn, jnp.zeros_like(hn)) if extra else hn


def _prenorm_q_call(x, gain, scale, shift, w_in, q_gain, *, tm, extra_rows=0):
    n, t, d = x.shape
    assert n == 1 and t % tm == 0 and extra_rows % tm == 0
    nrow, extra = t // tm, extra_rows // tm
    steps = nrow + max(1, extra)
    this = lambda s: jnp.minimum(s, nrow - 1)
    prev = lambda s: jnp.minimum(jnp.maximum(s - 1, 0), nrow - 1)
    vec = pl.BlockSpec((1, 1, d), lambda s: (0, 0, 0))
    return pl.pallas_call(
        functools.partial(_prenorm_q_kernel, nrow=nrow, extra=extra),
        grid=(steps,),
        in_specs=[pl.BlockSpec((1, tm, d), lambda s: (0, this(s), 0)), vec, vec, vec,
                  pl.BlockSpec((d, WIDTH), lambda s: (0, 0), pipeline_mode=pl.Buffered(1)),
                  pl.BlockSpec(q_gain.shape, lambda s: (0, 0))],
        out_specs=[pl.BlockSpec((tm, d), lambda s: (jnp.minimum(s, nrow - 1 + extra), 0)),
                   pl.BlockSpec((tm, WIDTH), lambda s: (prev(s), 0))],
        out_shape=[jax.ShapeDtypeStruct((t + extra_rows, d), BF16), jax.ShapeDtypeStruct((t, WIDTH), BF16)],
        scratch_shapes=[pltpu.VMEM((d, WIDTH), BF16), pltpu.VMEM((2, tm, d), BF16)],
        compiler_params=_cparams("arbitrary"),
        name="prenorm_q",
    )(x, gain, scale, shift, w_in, q_gain)


def _suffix_matrix(bk):
    j = lax.broadcasted_iota(jnp.int32, (2 * bk, 2 * bk), 0) % bk
    s = lax.broadcasted_iota(jnp.int32, (2 * bk, 2 * bk), 1)
    return jnp.where((j > s) | (s >= bk), -1.0, 0.0).astype(BF16)


def _sb_tiles(qs, ks, vs, carries, sfx, masks=None, valid=None):
    bk = sfx.shape[0] // 2
    n = range(len(qs))
    spans = [range(ks[i].shape[0] // bk) for i in n]
    lanes = lambda x, t: x[:, t * bk:(t + 1) * bk]
    mask_of = lambda i, t: None if masks is None or masks[i] is None else masks[i][t]
    zs = [lax.dot_general(qs[i], ks[i], (((1,), (1,)), ((), ())), preferred_element_type=F32) * HEAD_DIM ** -0.5
          for i in n]
    sps = [jnp.maximum(z, 0.0) + jnp.log(1.0 + jnp.exp(-jnp.abs(z))) for z in zs]
    r2s = []
    for i in n:
        r2 = []
        for t in spans[i]:
            m = mask_of(i, t)
            l1m = lanes(sps[i], t) if m is None else jnp.where(m, lanes(sps[i], t), 0.0)
            hi = l1m.astype(BF16)
            lo = (l1m - hi.astype(F32)).astype(BF16)
            r2.append(jnp.dot(jnp.concatenate([hi, lo], axis=1), sfx, preferred_element_type=F32))
        r2s.append(r2)
    new, wss = [], []
    for i in n:
        c = carries[i]
        ws = []
        for t in spans[i]:
            w = jnp.exp(lanes(zs[i], t) - lanes(sps[i], t) + r2s[i][t][:, :bk] + c)
            m = mask_of(i, t)
            if m is not None:
                w = jnp.where(m, w, 0.0)
            if valid is not None and valid[i][t] is not None:
                w = jnp.where(valid[i][t], w, 0.0)
            ws.append(w.astype(BF16))
            c = c + r2s[i][t][:, bk:]
        new.append(c)
        wss.append(ws[0] if len(ws) == 1 else jnp.concatenate(ws, axis=1))
    pvs = [jnp.dot(wss[i], vs[i], preferred_element_type=F32) for i in n]
    return new, pvs


def _sb_prompt_kernel(q_ref, k_ref, v_ref, z_ref, o_ref, c_scr, acc_scr, *, n_groups, group, ahead):
    blk = SB_BLOCK
    qt = pl.program_id(1)
    sfx = _suffix_matrix(blk)
    row = lax.broadcasted_iota(jnp.int32, (blk, blk), 0)
    col = lax.broadcasted_iota(jnp.int32, (blk, blk), 1)
    causal = col < row
    alive = lambda cs: (functools.reduce(jnp.maximum, [jnp.max(c) for c in cs]) >= SB_LOG_CUTOFF).astype(jnp.int32)

    def kv(kb):
        start = pl.multiple_of(kb * blk, blk)
        return k_ref[pl.ds(start, blk), :], v_ref[pl.ds(start, blk), :]

    def qgroup(ig, _):
        gq0 = (qt * n_groups + ig) * group
        rows = [pl.ds(pl.multiple_of((ig * group + g) * blk, blk), blk) for g in range(group)]
        qs = [q_ref[r, :] for r in rows]
        kbs = [[gq0 + g - s for s in range(1 + ahead)] for g in range(group)]
        kvs = [[kv(jnp.maximum(kb, 0)) for kb in kbs[g]] for g in range(group)]
        cat = lambda xs: jnp.concatenate(xs, axis=0)
        cs, pvs = _sb_tiles(qs, [cat([k for k, _ in kvs[g]]) for g in range(group)],
                            [cat([v for _, v in kvs[g]]) for g in range(group)],
                            [jnp.zeros((blk, blk), F32)] * group, sfx,
                            masks=[[causal] + [None] * ahead] * group,
                            valid=[[None] + [kb >= 0 for kb in kbs[g][1:]] for g in range(group)])
        for g in range(group):
            c_scr[g] = cs[g]
            acc_scr[g] = pvs[g]

        def cond(st):
            s, go = st
            return jnp.logical_and(s <= gq0 + group - 1, go > 0)

        def body(st):
            s, _ = st
            kbs = [gq0 + g - s for g in range(group)]
            kvs = [kv(jnp.maximum(kb, 0)) for kb in kbs]
            cs, pvs = _sb_tiles(qs, [k for k, _ in kvs], [v for _, v in kvs],
                                [c_scr[g] for g in range(group)], sfx, valid=[[kb >= 0] for kb in kbs])
            for g in range(group):
                c_scr[g] = cs[g]
                acc_scr[g] += pvs[g]
            return s + 1, alive(cs)

        lax.while_loop(cond, body, (1 + ahead, alive(cs)))
        for g in range(group):
            o_ref[rows[g], :] = (acc_scr[g] * _silu(z_ref[rows[g], :])).astype(BF16)
        return 0

    lax.fori_loop(0, n_groups, qgroup, 0)


def _sb_prompt_call(q, k, v, z, *, tq=8192, group=32, ahead=2):
    t = q.shape[0]
    tq = min(tq, t)
    group = math.gcd(group, tq // SB_BLOCK)
    assert t % tq == 0 and tq % (SB_BLOCK * group) == 0
    qspec = pl.BlockSpec((tq, HEAD_DIM), lambda h, i: (i, h))
    kvspec = pl.BlockSpec((t, HEAD_DIM), lambda h, i: (0, h))
    return pl.pallas_call(
        functools.partial(_sb_prompt_kernel, n_groups=tq // (SB_BLOCK * group), group=group, ahead=ahead),
        grid=(N_HEADS, t // tq),
        in_specs=[qspec, kvspec, kvspec, qspec],
        out_specs=qspec,
        out_shape=jax.ShapeDtypeStruct((t, WIDTH), BF16),
        scratch_shapes=[pltpu.VMEM((group, SB_BLOCK, SB_BLOCK), F32), pltpu.VMEM((group, SB_BLOCK, HEAD_DIM), F32)],
        compiler_params=_cparams("arbitrary", "arbitrary"),
        name="sb_prompt",
    )(q, k, v, z)


def _sb_sample_kernel(q_ref, kn_ref, vn_ref, z_ref, kc_hbm, vc_hbm, o_ref, kbuf, vbuf, sem, c_scr, acc_scr, *, past):
    blk = SB_BLOCK
    nh = N_HEADS
    b = pl.program_id(0)
    tq = q_ref.shape[0]
    half = blk - tq
    n_full = (past - half) // blk
    rem = (past - half) % blk
    sfx = _suffix_matrix(blk)
    row = lax.broadcasted_iota(jnp.int32, (tq, blk), 0)
    col = lax.broadcasted_iota(jnp.int32, (tq, blk), 1)
    heads = [slice(h * HEAD_DIM, (h + 1) * HEAD_DIM) for h in range(nh)]
    alive = lambda cs: (functools.reduce(jnp.maximum, [jnp.max(c) for c in cs]) >= SB_LOG_CUTOFF).astype(jnp.int32)

    def copies(stream, key0, nkeys, slot):
        src = pl.ds(key0 * nh, nkeys * nh)
        dst = pl.ds(0, nkeys * nh)
        return (pltpu.make_async_copy(kc_hbm.at[stream, src, :], kbuf.at[slot, dst, :], sem.at[0, slot]),
                pltpu.make_async_copy(vc_hbm.at[stream, src, :], vbuf.at[slot, dst, :], sem.at[1, slot]))

    def start(cps):
        for cp in cps:
            cp.start()

    def wait(cps):
        for cp in cps:
            cp.wait()

    first = min(n_full, 1)
    base = 2 * (b % 2)

    def tile_copies(j):
        return copies(b, past - half - (j + 1) * blk, blk, base + (j + 1) % 2)

    def first_copies(stream):
        slot0 = 2 * (stream % 2)
        cps = copies(stream, past - half, half, slot0)
        return cps + copies(stream, past - half - blk, blk, slot0 + 1) if first else cps

    def cached(buf, slot, h, nkeys):
        return buf[slot, pl.ds(h, nkeys, stride=nh), :].astype(BF16)

    @pl.when(b == 0)
    def _():
        start(first_copies(b))

    @pl.when(b + 1 < pl.num_programs(0))
    def _():
        start(first_copies(b + 1))

    wait(first_copies(b))

    qs = [q_ref[:, heads[h]] for h in range(nh)]

    def span(buf, new_ref, h):
        tiles = [cached(buf, base, h, half), new_ref[:, heads[h]]] + [cached(buf, base + 1, h, blk)] * first
        return jnp.concatenate(tiles, axis=0)

    cs, pvs = _sb_tiles(qs, [span(kbuf, kn_ref, h) for h in range(nh)], [span(vbuf, vn_ref, h) for h in range(nh)],
                        [jnp.zeros((tq, blk), F32)] * nh, sfx, masks=[[col < row + half] + [None] * first] * nh)
    for h in range(nh):
        c_scr[h] = cs[h]
        acc_scr[h] = pvs[h]

    def sweep(slot, mask):
        cs, pvs = _sb_tiles(qs, [cached(kbuf, slot, h, blk) for h in range(nh)],
                            [cached(vbuf, slot, h, blk) for h in range(nh)],
                            [c_scr[h] for h in range(nh)], sfx, masks=None if mask is None else [[mask]] * nh)
        for h in range(nh):
            c_scr[h] = cs[h]
            acc_scr[h] += pvs[h]
        return cs

    def cond(st):
        j, go = st
        return jnp.logical_and(j < n_full, go > 0)

    def body(st):
        j, _ = st
        cps = tile_copies(j)
        start(cps)
        wait(cps)
        return j + 1, alive(sweep(base + (j + 1) % 2, None))

    _, go = lax.while_loop(cond, body, (first, alive(cs)))

    if rem:
        @pl.when(go > 0)
        def _():
            cps = copies(b, 0, blk, base)
            start(cps)
            wait(cps)
            sweep(base, col < rem)

    for h in range(nh):
        o_ref[:, heads[h]] = (acc_scr[h] * _silu(z_ref[:, heads[h]])).astype(BF16)


def _sb_sample_call(q, kn, vn, z, kc, vc, *, z_row0=0):
    nb = kc.shape[0]
    past = kc.shape[1] // N_HEADS
    tq = q.shape[0] // nb
    assert tq % 16 == 0 and tq < SB_BLOCK and past >= SB_BLOCK and z_row0 % tq == 0
    new = pl.BlockSpec((tq, WIDTH), lambda b: (b, 0))
    zspec = pl.BlockSpec((tq, WIDTH), lambda b: (z_row0 // tq + b, 0))
    hbm = pl.BlockSpec(memory_space=pl.ANY)
    return pl.pallas_call(
        functools.partial(_sb_sample_kernel, past=past),
        grid=(nb,),
        in_specs=[new, new, new, zspec, hbm, hbm],
        out_specs=new,
        out_shape=jax.ShapeDtypeStruct(q.shape, BF16),
        scratch_shapes=[pltpu.VMEM((4, SB_BLOCK * N_HEADS, HEAD_DIM), F32),
                        pltpu.VMEM((4, SB_BLOCK * N_HEADS, HEAD_DIM), F32),
                        pltpu.SemaphoreType.DMA((2, 4)),
                        pltpu.VMEM((N_HEADS, tq, SB_BLOCK), F32), pltpu.VMEM((N_HEADS, tq, HEAD_DIM), F32)],
        compiler_params=_cparams("arbitrary"),
        name="sb_sample",
    )(q, kn, vn, z, kc, vc)


def _prefix_matrix(c):
    t = lax.broadcasted_iota(jnp.int32, (2 * c, c), 0)
    s = lax.broadcasted_iota(jnp.int32, (2 * c, c), 1)
    incl = (t < c) & (s <= t)
    sub = (t >= c) & (s < ((t - c) // HG_SUB) * HG_SUB)
    return jnp.where(incl | sub, 1.0, 0.0).astype(BF16)


def _hgrn_front(lf, qh, kh, n_heads, pfx, tril):
    c = lf.shape[0]
    n_sub = c // HG_SUB
    heads = [slice(h * HEAD_DIM, (h + 1) * HEAD_DIM) for h in range(n_heads)]
    p0 = lf.astype(BF16)
    r1 = lf - p0.astype(F32)
    p1 = r1.astype(BF16)
    p2 = (r1 - p1.astype(F32)).astype(BF16)
    br = (jnp.dot(pfx, p0, preferred_element_type=F32) + jnp.dot(pfx, p1, preferred_element_type=F32)
          + jnp.dot(pfx, p2, preferred_element_type=F32))
    b = br[:c]
    r = br[c:]
    b_last = b[c - 1:c, :]
    q_sub = (qh * jnp.exp(b - r)).astype(BF16)
    q_dec = (qh * jnp.exp(b)).astype(BF16)
    k_end = (kh * jnp.exp(b_last - b)).astype(BF16)
    dec = jnp.exp(b_last)
    att = [[] for _ in heads]
    for i in range(n_sub):
        lo, hi = i * HG_SUB, (i + 1) * HG_SUB
        k_i = (kh[:hi] * jnp.exp(r[lo:lo + 1, :] - b[:hi])).astype(BF16)
        if hi < c:
            k_i = jnp.concatenate([k_i, jnp.zeros((c - hi, k_i.shape[1]), BF16)], axis=0)
        for h, hs in enumerate(heads):
            att[h].append(lax.dot_general(q_sub[lo:hi, hs], k_i[:, hs], (((1,), (1,)), ((), ())),
                                          preferred_element_type=F32))
    att = [jnp.where(tril, jnp.concatenate(a, axis=0), 0.0).astype(BF16) for a in att]
    return att, q_dec, k_end, dec


def _hgrn_back(front, v, sts):
    att, q_dec, k_end, dec = front
    heads = [slice(h * HEAD_DIM, (h + 1) * HEAD_DIM) for h in range(len(sts))]
    vb = v
    outs, new_sts = [], []
    for h, hs in enumerate(heads):
        o = jnp.dot(att[h], vb[:, hs], preferred_element_type=F32)
        o = o + lax.dot_general(q_dec[:, hs], sts[h].astype(BF16), (((1,), (1,)), ((), ())),
                                preferred_element_type=F32)
        outs.append(o)
    for h, hs in enumerate(heads):
        new_sts.append(sts[h] * dec[:, hs] + lax.dot_general(vb[:, hs], k_end[:, hs], (((0,), (0,)), ((), ())),
                                                             preferred_element_type=F32))
    return outs, new_sts


def _hgrn_kernel(lf_ref, qh_ref, kh_ref, v_ref, zh_ref, g_ref, s0_ref, o_ref, s_ref, st_scr, *, chunk, n_chunks, streams):
    tt = pl.program_id(2)
    nh = st_scr.shape[0] // streams
    tile = lf_ref.shape[0] // streams
    heads = [slice(h * HEAD_DIM, (h + 1) * HEAD_DIM) for h in range(nh)]

    @pl.when(tt == 0)
    def _():
        for s in range(streams):
            for h in range(nh):
                st_scr[s * nh + h] = s0_ref[s, h].T

    pfx = _prefix_matrix(chunk)
    ti = lax.broadcasted_iota(jnp.int32, (chunk, chunk), 0)
    si = lax.broadcasted_iota(jnp.int32, (chunk, chunk), 1)
    tril = si <= ti

    unroll = 2 if n_chunks % 2 == 0 else 1

    def step(ci, _):
        rss = [[pl.ds(pl.multiple_of(s * tile + (ci * unroll + u) * chunk, chunk), chunk) for u in range(unroll)]
               for s in range(streams)]
        fronts = [[_hgrn_front(lf_ref[rs, :], qh_ref[rs, :], kh_ref[rs, :], nh, pfx, tril) for rs in rss[s]]
                  for s in range(streams)]
        for s in range(streams):
            sts = [st_scr[s * nh + h] for h in range(nh)]
            for rs, front in zip(rss[s], fronts[s]):
                outs, sts = _hgrn_back(front, v_ref[rs, :], sts)
                for h in range(nh):
                    o = outs[h]
                    ms = jnp.mean(o * o, axis=-1, keepdims=True)
                    o_ref[rs, heads[h]] = (o * lax.rsqrt(ms + EPS) * g_ref[h]
                                           * _silu(zh_ref[rs, heads[h]])).astype(BF16)
            for h in range(nh):
                st_scr[s * nh + h] = sts[h]
        return 0

    lax.fori_loop(0, n_chunks // unroll, step, 0)

    @pl.when(tt == pl.num_programs(2) - 1)
    def _():
        for s in range(streams):
            for h in range(nh):
                s_ref[s, h] = st_scr[s * nh + h].T


def _hgrn_call(lf, qh, kh, v, zh, gain, s0, *, t, chunk, tile, heads, row0=0, streams=1):
    nb = s0.shape[0]
    tile = min(tile, t)
    rows = streams * tile
    assert t % tile == 0 and tile % chunk == 0 and chunk % HG_SUB == 0 and N_HEADS % heads == 0 and row0 % rows == 0
    assert nb % streams == 0 and (streams == 1 or tile == t)
    nt = t // tile
    tok_in = pl.BlockSpec((rows, heads * HEAD_DIM), lambda b, h, i: (row0 // rows + b * nt + i, h))
    tok = pl.BlockSpec((rows, heads * HEAD_DIM), lambda b, h, i: (b * nt + i, h))
    state = pl.BlockSpec((streams, heads, HEAD_DIM, HEAD_DIM), lambda b, h, i: (b, h, 0, 0))
    return pl.pallas_call(
        functools.partial(_hgrn_kernel, chunk=chunk, n_chunks=tile // chunk, streams=streams),
        grid=(nb // streams, N_HEADS // heads, nt),
        in_specs=[tok_in] * 5 + [pl.BlockSpec((heads, 1, HEAD_DIM), lambda b, h, i: (h, 0, 0)), state],
        out_specs=[tok, state],
        out_shape=[jax.ShapeDtypeStruct((nb * t, WIDTH), BF16), jax.ShapeDtypeStruct(s0.shape, F32)],
        scratch_shapes=[pltpu.VMEM((streams * heads, HEAD_DIM, HEAD_DIM), F32)],
        compiler_params=_cparams("arbitrary", "arbitrary", "arbitrary"),
        name="hgrn2",
    )(lf, qh, kh, v, zh, gain, s0)


def _out_kernel(gs_ref, gh_ref, *rest):
    *gate_refs, x_ref, gate_ref, wsb_ref, whg_ref, wo_ref, y_ref = rest
    nb, tr, d = x_ref.shape
    half = len(gate_refs) // 2
    g_sb = jnp.concatenate([r[...] for r in gate_refs[:half]], axis=1)
    g_hg = jnp.concatenate([r[...] for r in gate_refs[half:]], axis=1)
    y_sb = jnp.dot(gs_ref[...], wsb_ref[...], preferred_element_type=F32)
    y_h = jnp.dot(gh_ref[...], whg_ref[...], preferred_element_type=F32)
    merged = jax.nn.sigmoid(g_sb) * y_sb + jax.nn.sigmoid(g_hg) * y_h
    upd = jnp.dot(merged.astype(BF16), wo_ref[...], preferred_element_type=F32)
    y_ref[...] = x_ref[...] + gate_ref[...] * upd.reshape(nb, tr, d)


def _out_call(gs, gh, gg, x, gate, wsb, whg, wo, nb, tr, *, gg_row0=0, gg_col0=0):
    n, t, d = x.shape
    tm = nb * tr
    nt = t // tr
    assert gg_row0 % tm == 0 and (nb == 1 or nt == 1) and gg_col0 % WIDTH == 0 and d % WIDTH == 0
    rowblk = lambda w, c, r0=0: pl.BlockSpec((tm, w), lambda i, j: (r0 // tm + i * nt + j, c))
    const = lambda a: pl.BlockSpec(a.shape, lambda i, j: (0, 0), pipeline_mode=pl.Buffered(1))
    n_gate = 2 * d // WIDTH
    return pl.pallas_call(
        _out_kernel,
        grid=(n // nb, nt),
        in_specs=[rowblk(WIDTH, 0), rowblk(WIDTH, 0)]
                 + [rowblk(WIDTH, gg_col0 // WIDTH + c, gg_row0) for c in range(n_gate)]
                 + [pl.BlockSpec((nb, tr, d), lambda i, j: (i, j, 0)),
                    pl.BlockSpec((nb, 1, d), lambda i, j: (i, 0, 0)),
                    const(wsb), const(whg), const(wo)],
        out_specs=pl.BlockSpec((nb, tr, d), lambda i, j: (i, j, 0)),
        out_shape=jax.ShapeDtypeStruct(x.shape, F32),
        compiler_params=_cparams("arbitrary", "arbitrary"),
        name="merge_out",
    )(gs, gh, *([gg] * n_gate), x, gate, wsb, whg, wo)


def _layer(x_p, x_s, mod_p, mod_s, p, s0_s, caches):
    n_p, t_p, d = x_p.shape
    n_s, t_s, _ = x_s.shape
    assert n_p == 1
    rows_p, rows_s = t_p, n_s * t_s
    tm = math.gcd(1024, rows_p, rows_s)
    w_in = p["w_in"]

    h, q_p = _prenorm_q_call(x_p, p["norm_gain"], mod_p[1], mod_p[0], w_in, p["q_gain"], tm=tm, extra_rows=rows_s)
    pn_tr = min(1024, t_s)
    pn_nb = max(1, min(n_s, 1024 // pn_tr))
    h = _prenorm_call(x_s, p["norm_gain"], mod_s[1], mod_s[0], pn_nb, pn_tr, h, rows_p)

    proj = functools.partial(_proj_call, h, w_in, tm=tm)
    prompt, sample = dict(row0=0, rows=rows_p), dict(row0=rows_p, rows=rows_s)
    q_s, k_s, kb_s, v_s, vb_s = _proj_qkv_call(h, w_in, p["q_gain"], p["k_gain"], **sample)
    k_p, kb_p = proj(1 * WIDTH, WIDTH, "norm_k", (p["k_gain"],), **prompt)
    v_p, vb_p = proj(2 * WIDTH, WIDTH, "copy2", **prompt)
    (z_sb,) = proj(3 * WIDTH, WIDTH, "plain")
    logf, k_h = proj(4 * WIDTH, WIDTH, "forget", (p["lb_raw"],))
    (i_h,) = proj(5 * WIDTH, WIDTH, "plain_bf16")
    (q_h,) = proj(6 * WIDTH, WIDTH, "silu")
    (zg,) = proj(7 * WIDTH, WIDTH + 2 * d, "plain")

    gs_p = _sb_prompt_call(q_p, kb_p, vb_p, z_sb)
    gs_s = _sb_sample_call(q_s, kb_s, vb_s, z_sb, caches[0], caches[1], z_row0=rows_p)
    hgrn = functools.partial(_hgrn_call, logf, q_h, k_h, i_h, zg, p["onorm_gain"])
    gh_p, s_p = hgrn(jnp.zeros((n_p, N_HEADS, HEAD_DIM, HEAD_DIM), F32), t=t_p, chunk=min(128, t_p), tile=1024,
                     heads=N_HEADS)
    gh_s, s_s = hgrn(s0_s, t=t_s, chunk=t_s, tile=t_s, heads=N_HEADS, row0=rows_p, streams=math.gcd(n_s, 4))
    out = functools.partial(_out_call, wsb=p["w_br_sb"], whg=p["w_br_hg"], wo=p["w_out"], gg_col0=WIDTH)
    y_p = out(gs_p, gh_p, zg, x_p, mod_p[2], nb=1, tr=min(256, t_p))
    y_s = out(gs_s, gh_s, zg, x_s, mod_s[2], nb=max(1, min(n_s, 256 // t_s)), tr=t_s, gg_row0=rows_p)
    heads5 = lambda a, n, t: a.reshape(1, n, t, N_HEADS, HEAD_DIM)
    return (y_p, y_s, heads5(k_p, n_p, t_p), heads5(v_p, n_p, t_p), s_p[None],
            heads5(k_s, n_s, t_s), heads5(v_s, n_s, t_s), s_s[None])


def kernel(x_prompt, x_sample, cache_sb_k, cache_sb_v, state_hgrn, c_prompt, c_sample, norm_gain, w_ada, b_ada, w_in, q_norm_gain, k_norm_gain, hgrn_lb_raw, hgrn_onorm_gain, w_branch_sb, w_branch_hgrn, w_out):
    assert w_in.shape[0] == 1, "single-layer trunk"
    n_p, t_p, d = x_prompt.shape
    n_s, t_s, _ = x_sample.shape
    past = cache_sb_k.shape[2]

    c_all = jnp.concatenate([c_prompt, c_sample], axis=0)
    pad = (-c_all.shape[0]) % 8
    c_all = jnp.pad(c_all, ((0, pad), (0, 0)))
    mod = _ada_call(c_all, w_ada[0], b_ada[0].reshape(1, 3 * d))
    mods = lambda lo, hi: tuple(mod[lo:hi, i * d:(i + 1) * d].reshape(hi - lo, 1, d) for i in range(3))

    p = {
        "norm_gain": norm_gain[0].reshape(1, 1, d),
        "w_in": w_in[0],
        "q_gain": q_norm_gain[0].reshape(1, HEAD_DIM),
        "k_gain": k_norm_gain[0].reshape(1, HEAD_DIM),
        "lb_raw": hgrn_lb_raw,
        "onorm_gain": hgrn_onorm_gain[0].reshape(N_HEADS, 1, HEAD_DIM),
        "w_br_sb": w_branch_sb[0].astype(BF16),
        "w_br_hg": w_branch_hgrn[0].astype(BF16),
        "w_out": w_out[0].astype(BF16),
    }

    return _layer(x_prompt, x_sample, mods(0, n_p), mods(n_p, n_p + n_s), p, state_hgrn[0],
                  (cache_sb_k.reshape(n_s, past * N_HEADS, HEAD_DIM), cache_sb_v.reshape(n_s, past * N_HEADS, HEAD_DIM)))
```

```python
import functools
import math

import jax
import jax.numpy as jnp
from jax import lax
from jax.experimental import pallas as pl
from jax.experimental.pallas import tpu as pltpu

F32 = jnp.float32
BF16 = jnp.bfloat16

N_HEADS = 8
HEAD_DIM = 128
WIDTH = N_HEADS * HEAD_DIM
HG_SUB = 16
EPS = 1e-6
SB_BLOCK = 128
SB_LOG_CUTOFF = -88.0
VMEM_LIMIT = 56 * 1024 * 1024


def _cparams(*sem):
    return pltpu.CompilerParams(dimension_semantics=sem, vmem_limit_bytes=VMEM_LIMIT)


def _silu(x):
    return x * jax.nn.sigmoid(x)


def _ada_kernel(c_ref, w_ref, b_ref, o_ref):
    c = c_ref[...]
    a = _silu(c).astype(BF16)
    o_ref[...] = jnp.dot(a, w_ref[...].astype(BF16), preferred_element_type=F32) + b_ref[...]


def _ada_call(c, w, b):
    r, d = c.shape
    n = w.shape[1]
    tn = 1024
    return pl.pallas_call(
        _ada_kernel,
        grid=(n // tn,),
        in_specs=[pl.BlockSpec((r, d), lambda j: (0, 0)),
                  pl.BlockSpec((d, tn), lambda j: (0, j)),
                  pl.BlockSpec((1, tn), lambda j: (0, j))],
        out_specs=pl.BlockSpec((r, tn), lambda j: (0, j)),
        out_shape=jax.ShapeDtypeStruct((r, n), F32),
        compiler_params=_cparams("arbitrary"),
        name="ada_mod",
    )(c, w, b)


def _prenorm_kernel(x_ref, g_ref, sc_ref, sh_ref, hall_ref, h_ref):
    del hall_ref
    x = x_ref[...]
    ms = jnp.mean(x * x, axis=-1, keepdims=True)
    xn = x * lax.rsqrt(ms + EPS)
    h = xn * g_ref[...] * (1.0 + sc_ref[...]) + sh_ref[...]
    h_ref[...] = h.astype(BF16).reshape(h_ref.shape)


def _prenorm_call(x, gain, scale, shift, nb, tr, h_all, row0):
    n, t, d = x.shape
    tm = nb * tr
    assert row0 % tm == 0 and (nb == 1 or tr == t)
    vec = pl.BlockSpec((nb, 1, d), lambda i, j: (i, 0, 0))
    return pl.pallas_call(
        _prenorm_kernel,
        grid=(n // nb, t // tr),
        in_specs=[pl.BlockSpec((nb, tr, d), lambda i, j: (i, j, 0)),
                  pl.BlockSpec((1, 1, d), lambda i, j: (0, 0, 0)), vec, vec,
                  pl.BlockSpec(memory_space=pl.ANY)],
        out_specs=pl.BlockSpec((tm, d), lambda i, j: (row0 // tm + i * (t // tr) + j, 0)),
        out_shape=jax.ShapeDtypeStruct(h_all.shape, BF16),
        input_output_aliases={4: 0},
        compiler_params=_cparams("arbitrary", "arbitrary"),
        name="prenorm",
    )(x, gain, scale, shift, h_all)


def _head_rms(y, gain):
    outs = []
    for g in range(N_HEADS):
        yh = y[:, g * HEAD_DIM:(g + 1) * HEAD_DIM]
        ms = jnp.mean(yh * yh, axis=-1, keepdims=True)
        outs.append(yh * lax.rsqrt(ms + EPS) * gain)
    return outs


def _proj_kernel(*refs, kind):
    h_ref, w_ref = refs[0], refs[1]
    wb_ref = refs[-1]

    @pl.when(pl.program_id(1) == 0)
    def _():
        wb_ref[...] = w_ref[...].astype(BF16)

    y = jnp.dot(h_ref[...], wb_ref[...], preferred_element_type=F32)
    tm = y.shape[0]
    if kind == "plain":
        refs[2][...] = y
    elif kind == "plain_bf16":
        refs[2][...] = y.astype(BF16)
    elif kind == "silu":
        refs[2][...] = _silu(y)
    elif kind == "copy2":
        for g in range(N_HEADS):
            refs[2][pl.ds(g, tm, stride=N_HEADS), :] = y[:, g * HEAD_DIM:(g + 1) * HEAD_DIM]
        refs[3][...] = y.astype(BF16)
    elif kind == "norm_k":
        gain = refs[2][...]
        for g, o in enumerate(_head_rms(y, gain)):
            refs[3][pl.ds(g, tm, stride=N_HEADS), :] = o
            refs[4][:, g * HEAD_DIM:(g + 1) * HEAD_DIM] = o.astype(BF16)
    elif kind == "forget":
        raw = refs[2][...]
        e = jnp.exp(raw - jnp.max(raw, axis=0, keepdims=True))
        lb = e[0:1, :] / jnp.sum(e, axis=0, keepdims=True)
        f = lb + (1.0 - lb) * jax.nn.sigmoid(y)
        refs[3][...] = jnp.log(f)
        refs[4][...] = 1.0 - f
    else:
        raise ValueError(kind)


def _proj_call(h, w_in, col0, ncols, kind, extra=(), *, tm=512, tn=1024, row0=0, rows=None):
    d = h.shape[1]
    rows = h.shape[0] - row0 if rows is None else rows
    assert col0 % tn == 0 and ncols % tn == 0 and rows % tm == 0 and row0 % tm == 0
    jb, ib = col0 // tn, row0 // tm
    grid = (ncols // tn, rows // tm)
    tile = lambda: pl.BlockSpec((tm, tn), lambda j, i: (i, j))
    in_specs = [pl.BlockSpec((tm, d), lambda j, i: (ib + i, 0)),
                pl.BlockSpec((d, tn), lambda j, i: (0, jb + j))]
    for e in extra:
        in_specs.append(pl.BlockSpec(e.shape, lambda j, i: (0, 0)))
    if kind in ("plain", "silu"):
        out_dt = (F32,)
    elif kind == "plain_bf16":
        out_dt = (BF16,)
    elif kind in ("copy2", "norm_k"):
        out_dt = (F32, BF16)
    else:
        out_dt = (F32, F32)
    out_specs = [tile() for _ in out_dt]
    out_shape = [jax.ShapeDtypeStruct((rows, ncols), dt) for dt in out_dt]
    if kind in ("copy2", "norm_k"):
        assert ncols == WIDTH
        out_specs[0] = pl.BlockSpec((tm * N_HEADS, HEAD_DIM), lambda j, i: (i, 0))
        out_shape[0] = jax.ShapeDtypeStruct((rows * N_HEADS, HEAD_DIM), F32)
    outs = pl.pallas_call(
        functools.partial(_proj_kernel, kind=kind),
        grid=grid,
        in_specs=in_specs,
        out_specs=out_specs,
        out_shape=out_shape,
        scratch_shapes=[pltpu.VMEM((d, tn), BF16)],
        compiler_params=_cparams("arbitrary", "arbitrary"),
        name="proj_" + kind,
    )(h, w_in, *extra)
    return outs


def _proj_qkv_kernel(h_ref, w_ref, qg_ref, kg_ref, q_ref, k_ref, kb_ref, v_ref, vb_ref):
    j = pl.program_id(1)
    y = jnp.dot(h_ref[...], w_ref[...].astype(BF16), preferred_element_type=F32)
    tm = y.shape[0]
    heads = [slice(g * HEAD_DIM, (g + 1) * HEAD_DIM) for g in range(N_HEADS)]

    @pl.when(j == 0)
    def _():
        for g, o in enumerate(_head_rms(y, qg_ref[...])):
            q_ref[:, heads[g]] = o.astype(BF16)

    @pl.when(j == 1)
    def _():
        for g, o in enumerate(_head_rms(y, kg_ref[...])):
            k_ref[pl.ds(g, tm, stride=N_HEADS), :] = o
            kb_ref[:, heads[g]] = o.astype(BF16)

    @pl.when(j == 2)
    def _():
        for g in range(N_HEADS):
            v_ref[pl.ds(g, tm, stride=N_HEADS), :] = y[:, heads[g]]
        vb_ref[...] = y.astype(BF16)


def _proj_qkv_call(h, w_in, q_gain, k_gain, *, row0, rows, tm=512):
    d = h.shape[1]
    tm = min(tm, rows)
    assert rows % tm == 0 and row0 % tm == 0
    ib = row0 // tm
    wide = lambda: pl.BlockSpec((tm, WIDTH), lambda i, j: (i, 0))
    tall = lambda: pl.BlockSpec((tm * N_HEADS, HEAD_DIM), lambda i, j: (i, 0))
    gain = lambda g: pl.BlockSpec(g.shape, lambda i, j: (0, 0))
    return pl.pallas_call(
        _proj_qkv_kernel,
        grid=(rows // tm, 3),
        in_specs=[pl.BlockSpec((tm, d), lambda i, j: (ib + i, 0)),
                  pl.BlockSpec((d, WIDTH), lambda i, j: (0, j)), gain(q_gain), gain(k_gain)],
        out_specs=[wide(), tall(), wide(), tall(), wide()],
        out_shape=[jax.ShapeDtypeStruct((rows, WIDTH), BF16),
                   jax.ShapeDtypeStruct((rows * N_HEADS, HEAD_DIM), F32), jax.ShapeDtypeStruct((rows, WIDTH), BF16),
                   jax.ShapeDtypeStruct((rows * N_HEADS, HEAD_DIM), F32), jax.ShapeDtypeStruct((rows, WIDTH), BF16)],
        compiler_params=_cparams("arbitrary", "arbitrary"),
        name="proj_qkv",
    )(h, w_in, q_gain, k_gain)


def _prenorm_q_kernel(x_ref, g_ref, sc_ref, sh_ref, w_ref, qg_ref, h_ref, q_ref, wb_ref, h2_ref, *, nrow, extra):
    s = pl.program_id(0)

    @pl.when(s == 0)
    def _():
        wb_ref[...] = w_ref[...].astype(BF16)
        h2_ref[1] = jnp.zeros(h2_ref.shape[1:], BF16)

    y = jnp.dot(h2_ref[(s + 1) % 2], wb_ref[...], preferred_element_type=F32)
    for g, o in enumerate(_head_rms(y, qg_ref[...])):
        q_ref[:, g * HEAD_DIM:(g + 1) * HEAD_DIM] = o.astype(BF16)

    x = x_ref[0]
    ms = jnp.mean(x * x, axis=-1, keepdims=True)
    hn = (x * lax.rsqrt(ms + EPS) * g_ref[0] * (1.0 + sc_ref[0]) + sh_ref[0]).astype(BF16)
    h2_ref[s % 2] = hn
    h_ref[...] = jnp.where(s < nrow, hn, jnp.zeros_like(hn)) if extra else hn


def _prenorm_q_call(x, gain, scale, shift, w_in, q_gain, *, tm, extra_rows=0):
    n, t, d = x.shape
    assert n == 1 and t % tm == 0 and extra_rows % tm == 0
    nrow, extra = t // tm, extra_rows // tm
    steps = nrow + max(1, extra)
    this = lambda s: jnp.minimum(s, nrow - 1)
    prev = lambda s: jnp.minimum(jnp.maximum(s - 1, 0), nrow - 1)
    vec = pl.BlockSpec((1, 1, d), lambda s: (0, 0, 0))
    return pl.pallas_call(
        functools.partial(_prenorm_q_kernel, nrow=nrow, extra=extra),
        grid=(steps,),
        in_specs=[pl.BlockSpec((1, tm, d), lambda s: (0, this(s), 0)), vec, vec, vec,
                  pl.BlockSpec((d, WIDTH), lambda s: (0, 0), pipeline_mode=pl.Buffered(1)),
                  pl.BlockSpec(q_gain.shape, lambda s: (0, 0))],
        out_specs=[pl.BlockSpec((tm, d), lambda s: (jnp.minimum(s, nrow - 1 + extra), 0)),
                   pl.BlockSpec((tm, WIDTH), lambda s: (prev(s), 0))],
        out_shape=[jax.ShapeDtypeStruct((t + extra_rows, d), BF16), jax.ShapeDtypeStruct((t, WIDTH), BF16)],
        scratch_shapes=[pltpu.VMEM((d, WIDTH), BF16), pltpu.VMEM((2, tm, d), BF16)],
        compiler_params=_cparams("arbitrary"),
        name="prenorm_q",
    )(x, gain, scale, shift, w_in, q_gain)


def _suffix_matrix(bk):
    j = lax.broadcasted_iota(jnp.int32, (2 * bk, 2 * bk), 0) % bk
    s = lax.broadcasted_iota(jnp.int32, (2 * bk, 2 * bk), 1)
    return jnp.where((j > s) | (s >= bk), -1.0, 0.0).astype(BF16)


def _sb_tiles(qs, ks, vs, carries, sfx, masks=None, valid=None):
    bk = sfx.shape[0] // 2
    n = range(len(qs))
    spans = [range(ks[i].shape[0] // bk) for i in n]
    lanes = lambda x, t: x[:, t * bk:(t + 1) * bk]
    mask_of = lambda i, t: None if masks is None or masks[i] is None else masks[i][t]
    zs = [lax.dot_general(qs[i], ks[i], (((1,), (1,)), ((), ())), preferred_element_type=F32) * HEAD_DIM ** -0.5
          for i in n]
    sps = [jnp.maximum(z, 0.0) + jnp.log(1.0 + jnp.exp(-jnp.abs(z))) for z in zs]
    r2s = []
    for i in n:
        r2 = []
        for t in spans[i]:
            m = mask_of(i, t)
            l1m = lanes(sps[i], t) if m is None else jnp.where(m, lanes(sps[i], t), 0.0)
            hi = l1m.astype(BF16)
            lo = (l1m - hi.astype(F32)).astype(BF16)
            r2.append(jnp.dot(jnp.concatenate([hi, lo], axis=1), sfx, preferred_element_type=F32))
        r2s.append(r2)
    new, wss = [], []
    for i in n:
        c = carries[i]
        ws = []
        for t in spans[i]:
            w = jnp.exp(lanes(zs[i], t) - lanes(sps[i], t) + r2s[i][t][:, :bk] + c)
            m = mask_of(i, t)
            if m is not None:
                w = jnp.where(m, w, 0.0)
            if valid is not None and valid[i][t] is not None:
                w = jnp.where(valid[i][t], w, 0.0)
            ws.append(w.astype(BF16))
            c = c + r2s[i][t][:, bk:]
        new.append(c)
        wss.append(ws[0] if len(ws) == 1 else jnp.concatenate(ws, axis=1))
    pvs = [jnp.dot(wss[i], vs[i], preferred_element_type=F32) for i in n]
    return new, pvs


def _sb_prompt_kernel(q_ref, k_ref, v_ref, z_ref, o_ref, c_scr, acc_scr, *, n_groups, group, ahead):
    blk = SB_BLOCK
    qt = pl.program_id(1)
    sfx = _suffix_matrix(blk)
    row = lax.broadcasted_iota(jnp.int32, (blk, blk), 0)
    col = lax.broadcasted_iota(jnp.int32, (blk, blk), 1)
    causal = col < row
    alive = lambda cs: (functools.reduce(jnp.maximum, [jnp.max(c) for c in cs]) >= SB_LOG_CUTOFF).astype(jnp.int32)

    def kv(kb):
        start = pl.multiple_of(kb * blk, blk)
        return k_ref[pl.ds(start, blk), :], v_ref[pl.ds(start, blk), :]

    def qgroup(ig, _):
        gq0 = (qt * n_groups + ig) * group
        rows = [pl.ds(pl.multiple_of((ig * group + g) * blk, blk), blk) for g in range(group)]
        qs = [q_ref[r, :] for r in rows]
        kbs = [[gq0 + g - s for s in range(1 + ahead)] for g in range(group)]
        kvs = [[kv(jnp.maximum(kb, 0)) for kb in kbs[g]] for g in range(group)]
        cat = lambda xs: jnp.concatenate(xs, axis=0)
        cs, pvs = _sb_tiles(qs, [cat([k for k, _ in kvs[g]]) for g in range(group)],
                            [cat([v for _, v in kvs[g]]) for g in range(group)],
                            [jnp.zeros((blk, blk), F32)] * group, sfx,
                            masks=[[causal] + [None] * ahead] * group,
                            valid=[[None] + [kb >= 0 for kb in kbs[g][1:]] for g in range(group)])
        for g in range(group):
            c_scr[g] = cs[g]
            acc_scr[g] = pvs[g]

        def cond(st):
            s, go = st
            return jnp.logical_and(s <= gq0 + group - 1, go > 0)

        def body(st):
            s, _ = st
            kbs = [gq0 + g - s for g in range(group)]
            kvs = [kv(jnp.maximum(kb, 0)) for kb in kbs]
            cs, pvs = _sb_tiles(qs, [k for k, _ in kvs], [v for _, v in kvs],
                                [c_scr[g] for g in range(group)], sfx, valid=[[kb >= 0] for kb in kbs])
            for g in range(group):
                c_scr[g] = cs[g]
                acc_scr[g] += pvs[g]
            return s + 1, alive(cs)

        lax.while_loop(cond, body, (1 + ahead, alive(cs)))
        for g in range(group):
            o_ref[rows[g], :] = (acc_scr[g] * z_ref[rows[g], :]).astype(BF16)
        return 0

    lax.fori_loop(0, n_groups, qgroup, 0)


def _sb_prompt_call(q, k, v, z, *, tq=8192, group=32, ahead=2):
    t = q.shape[0]
    tq = min(tq, t)
    group = math.gcd(group, tq // SB_BLOCK)
    assert t % tq == 0 and tq % (SB_BLOCK * group) == 0
    qspec = pl.BlockSpec((tq, HEAD_DIM), lambda h, i: (i, h))
    kvspec = pl.BlockSpec((t, HEAD_DIM), lambda h, i: (0, h))
    return pl.pallas_call(
        functools.partial(_sb_prompt_kernel, n_groups=tq // (SB_BLOCK * group), group=group, ahead=ahead),
        grid=(N_HEADS, t // tq),
        in_specs=[qspec, kvspec, kvspec, qspec],
        out_specs=qspec,
        out_shape=jax.ShapeDtypeStruct((t, WIDTH), BF16),
        scratch_shapes=[pltpu.VMEM((group, SB_BLOCK, SB_BLOCK), F32), pltpu.VMEM((group, SB_BLOCK, HEAD_DIM), F32)],
        compiler_params=_cparams("arbitrary", "arbitrary"),
        name="sb_prompt",
    )(q, k, v, z)


def _sb_sample_kernel(q_ref, kn_ref, vn_ref, z_ref, kc_hbm, vc_hbm, o_ref, kbuf, vbuf, sem, c_scr, acc_scr, *, past):
    blk = SB_BLOCK
    nh = N_HEADS
    b = pl.program_id(0)
    tq = q_ref.shape[0]
    half = blk - tq
    n_full = (past - half) // blk
    rem = (past - half) % blk
    sfx = _suffix_matrix(blk)
    row = lax.broadcasted_iota(jnp.int32, (tq, blk), 0)
    col = lax.broadcasted_iota(jnp.int32, (tq, blk), 1)
    heads = [slice(h * HEAD_DIM, (h + 1) * HEAD_DIM) for h in range(nh)]
    alive = lambda cs: (functools.reduce(jnp.maximum, [jnp.max(c) for c in cs]) >= SB_LOG_CUTOFF).astype(jnp.int32)

    def copies(stream, key0, nkeys, slot):
        src = pl.ds(key0 * nh, nkeys * nh)
        dst = pl.ds(0, nkeys * nh)
        return (pltpu.make_async_copy(kc_hbm.at[stream, src, :], kbuf.at[slot, dst, :], sem.at[0, slot]),
                pltpu.make_async_copy(vc_hbm.at[stream, src, :], vbuf.at[slot, dst, :], sem.at[1, slot]))

    def start(cps):
        for cp in cps:
            cp.start()

    def wait(cps):
        for cp in cps:
            cp.wait()

    first = min(n_full, 1)
    base = 2 * (b % 2)

    def tile_copies(j):
        return copies(b, past - half - (j + 1) * blk, blk, base + (j + 1) % 2)

    def first_copies(stream):
        slot0 = 2 * (stream % 2)
        cps = copies(stream, past - half, half, slot0)
        return cps + copies(stream, past - half - blk, blk, slot0 + 1) if first else cps

    def cached(buf, slot, h, nkeys):
        return buf[slot, pl.ds(h, nkeys, stride=nh), :].astype(BF16)

    @pl.when(b == 0)
    def _():
        start(first_copies(b))

    @pl.when(b + 1 < pl.num_programs(0))
    def _():
        start(first_copies(b + 1))

    wait(first_copies(b))

    qs = [q_ref[:, heads[h]] for h in range(nh)]

    def span(buf, new_ref, h):
        tiles = [cached(buf, base, h, half), new_ref[:, heads[h]]] + [cached(buf, base + 1, h, blk)] * first
        return jnp.concatenate(tiles, axis=0)

    cs, pvs = _sb_tiles(qs, [span(kbuf, kn_ref, h) for h in range(nh)], [span(vbuf, vn_ref, h) for h in range(nh)],
                        [jnp.zeros((tq, blk), F32)] * nh, sfx, masks=[[col < row + half] + [None] * first] * nh)
    for h in range(nh):
        c_scr[h] = cs[h]
        acc_scr[h] = pvs[h]

    def sweep(slot, mask):
        cs, pvs = _sb_tiles(qs, [cached(kbuf, slot, h, blk) for h in range(nh)],
                            [cached(vbuf, slot, h, blk) for h in range(nh)],
                            [c_scr[h] for h in range(nh)], sfx, masks=None if mask is None else [[mask]] * nh)
        for h in range(nh):
            c_scr[h] = cs[h]
            acc_scr[h] += pvs[h]
        return cs

    def cond(st):
        j, go = st
        return jnp.logical_and(j < n_full, go > 0)

    def body(st):
        j, _ = st
        cps = tile_copies(j)
        start(cps)
        wait(cps)
        return j + 1, alive(sweep(base + (j + 1) % 2, None))

    _, go = lax.while_loop(cond, body, (first, alive(cs)))

    if rem:
        @pl.when(go > 0)
        def _():
            cps = copies(b, 0, blk, base)
            start(cps)
            wait(cps)
            sweep(base, col < rem)

    for h in range(nh):
        o_ref[:, heads[h]] = (acc_scr[h] * z_ref[:, heads[h]]).astype(BF16)


def _sb_sample_call(q, kn, vn, z, kc, vc, *, z_row0=0):
    nb = kc.shape[0]
    past = kc.shape[1] // N_HEADS
    tq = q.shape[0] // nb
    assert tq % 16 == 0 and tq < SB_BLOCK and past >= SB_BLOCK and z_row0 % tq == 0
    new = pl.BlockSpec((tq, WIDTH), lambda b: (b, 0))
    zspec = pl.BlockSpec((tq, WIDTH), lambda b: (z_row0 // tq + b, 0))
    hbm = pl.BlockSpec(memory_space=pl.ANY)
    return pl.pallas_call(
        functools.partial(_sb_sample_kernel, past=past),
        grid=(nb,),
        in_specs=[new, new, new, zspec, hbm, hbm],
        out_specs=new,
        out_shape=jax.ShapeDtypeStruct(q.shape, BF16),
        scratch_shapes=[pltpu.VMEM((4, SB_BLOCK * N_HEADS, HEAD_DIM), F32),
                        pltpu.VMEM((4, SB_BLOCK * N_HEADS, HEAD_DIM), F32),
                        pltpu.SemaphoreType.DMA((2, 4)),
                        pltpu.VMEM((N_HEADS, tq, SB_BLOCK), F32), pltpu.VMEM((N_HEADS, tq, HEAD_DIM), F32)],
        compiler_params=_cparams("arbitrary"),
        name="sb_sample",
    )(q, kn, vn, z, kc, vc)


def _prefix_matrix(c):
    t = lax.broadcasted_iota(jnp.int32, (2 * c, c), 0)
    s = lax.broadcasted_iota(jnp.int32, (2 * c, c), 1)
    incl = (t < c) & (s <= t)
    sub = (t >= c) & (s < ((t - c) // HG_SUB) * HG_SUB)
    return jnp.where(incl | sub, 1.0, 0.0).astype(BF16)


def _hgrn_front(lf, qh, kh, n_heads, pfx, tril):
    c = lf.shape[0]
    n_sub = c // HG_SUB
    heads = [slice(h * HEAD_DIM, (h + 1) * HEAD_DIM) for h in range(n_heads)]
    p0 = lf.astype(BF16)
    r1 = lf - p0.astype(F32)
    p1 = r1.astype(BF16)
    p2 = (r1 - p1.astype(F32)).astype(BF16)
    br = (jnp.dot(pfx, p0, preferred_element_type=F32) + jnp.dot(pfx, p1, preferred_element_type=F32)
          + jnp.dot(pfx, p2, preferred_element_type=F32))
    b = br[:c]
    r = br[c:]
    b_last = b[c - 1:c, :]
    q_sub = (qh * jnp.exp(b - r)).astype(BF16)
    q_dec = (qh * jnp.exp(b)).astype(BF16)
    k_end = (kh * jnp.exp(b_last - b)).astype(BF16)
    dec = jnp.exp(b_last)
    att = [[] for _ in heads]
    for i in range(n_sub):
        lo, hi = i * HG_SUB, (i + 1) * HG_SUB
        k_i = (kh[:hi] * jnp.exp(r[lo:lo + 1, :] - b[:hi])).astype(BF16)
        if hi < c:
            k_i = jnp.concatenate([k_i, jnp.zeros((c - hi, k_i.shape[1]), BF16)], axis=0)
        for h, hs in enumerate(heads):
            att[h].append(lax.dot_general(q_sub[lo:hi, hs], k_i[:, hs], (((1,), (1,)), ((), ())),
                                          preferred_element_type=F32))
    att = [jnp.where(tril, jnp.concatenate(a, axis=0), 0.0).astype(BF16) for a in att]
    return att, q_dec, k_end, dec


def _hgrn_back(front, v, sts):
    att, q_dec, k_end, dec = front
    heads = [slice(h * HEAD_DIM, (h + 1) * HEAD_DIM) for h in range(len(sts))]
    vb = v
    outs, new_sts = [], []
    for h, hs in enumerate(heads):
        o = jnp.dot(att[h], vb[:, hs], preferred_element_type=F32)
        o = o + lax.dot_general(q_dec[:, hs], sts[h].astype(BF16), (((1,), (1,)), ((), ())),
                                preferred_element_type=F32)
        outs.append(o)
    for h, hs in enumerate(heads):
        new_sts.append(sts[h] * dec[:, hs] + lax.dot_general(vb[:, hs], k_end[:, hs], (((0,), (0,)), ((), ())),
                                                             preferred_element_type=F32))
    return outs, new_sts


def _hgrn_kernel(lf_ref, qh_ref, kh_ref, v_ref, zh_ref, g_ref, s0_ref, o_ref, s_ref, st_scr, *, chunk, n_chunks, streams):
    tt = pl.program_id(2)
    nh = st_scr.shape[0] // streams
    tile = lf_ref.shape[0] // streams
    heads = [slice(h * HEAD_DIM, (h + 1) * HEAD_DIM) for h in range(nh)]

    @pl.when(tt == 0)
    def _():
        for s in range(streams):
            for h in range(nh):
                st_scr[s * nh + h] = s0_ref[s, h].T

    pfx = _prefix_matrix(chunk)
    ti = lax.broadcasted_iota(jnp.int32, (chunk, chunk), 0)
    si = lax.broadcasted_iota(jnp.int32, (chunk, chunk), 1)
    tril = si <= ti

    unroll = 2 if n_chunks % 2 == 0 else 1

    def step(ci, _):
        rss = [[pl.ds(pl.multiple_of(s * tile + (ci * unroll + u) * chunk, chunk), chunk) for u in range(unroll)]
               for s in range(streams)]
        fronts = [[_hgrn_front(lf_ref[rs, :], qh_ref[rs, :], kh_ref[rs, :], nh, pfx, tril) for rs in rss[s]]
                  for s in range(streams)]
        for s in range(streams):
            sts = [st_scr[s * nh + h] for h in range(nh)]
            for rs, front in zip(rss[s], fronts[s]):
                outs, sts = _hgrn_back(front, v_ref[rs, :], sts)
                for h in range(nh):
                    o = outs[h]
                    ms = jnp.mean(o * o, axis=-1, keepdims=True)
                    o_ref[rs, heads[h]] = (o * lax.rsqrt(ms + EPS) * g_ref[h]
                                           * _silu(zh_ref[rs, heads[h]])).astype(BF16)
            for h in range(nh):
                st_scr[s * nh + h] = sts[h]
        return 0

    lax.fori_loop(0, n_chunks // unroll, step, 0)

    @pl.when(tt == pl.num_programs(2) - 1)
    def _():
        for s in range(streams):
            for h in range(nh):
                s_ref[s, h] = st_scr[s * nh + h].T


def _hgrn_call(lf, qh, kh, v, zh, gain, s0, *, t, chunk, tile, heads, row0=0, streams=1):
    nb = s0.shape[0]
    tile = min(tile, t)
    rows = streams * tile
    assert t % tile == 0 and tile % chunk == 0 and chunk % HG_SUB == 0 and N_HEADS % heads == 0 and row0 % rows == 0
    assert nb % streams == 0 and (streams == 1 or tile == t)
    nt = t // tile
    tok_in = pl.BlockSpec((rows, heads * HEAD_DIM), lambda b, h, i: (row0 // rows + b * nt + i, h))
    tok = pl.BlockSpec((rows, heads * HEAD_DIM), lambda b, h, i: (b * nt + i, h))
    state = pl.BlockSpec((streams, heads, HEAD_DIM, HEAD_DIM), lambda b, h, i: (b, h, 0, 0))
    return pl.pallas_call(
        functools.partial(_hgrn_kernel, chunk=chunk, n_chunks=tile // chunk, streams=streams),
        grid=(nb // streams, N_HEADS // heads, nt),
        in_specs=[tok_in] * 5 + [pl.BlockSpec((heads, 1, HEAD_DIM), lambda b, h, i: (h, 0, 0)), state],
        out_specs=[tok, state],
        out_shape=[jax.ShapeDtypeStruct((nb * t, WIDTH), BF16), jax.ShapeDtypeStruct(s0.shape, F32)],
        scratch_shapes=[pltpu.VMEM((streams * heads, HEAD_DIM, HEAD_DIM), F32)],
        compiler_params=_cparams("arbitrary", "arbitrary", "arbitrary"),
        name="hgrn2",
    )(lf, qh, kh, v, zh, gain, s0)


def _out_kernel(gs_ref, gh_ref, *rest):
    *gate_refs, x_ref, gate_ref, wsb_ref, whg_ref, wo_ref, y_ref = rest
    nb, tr, d = x_ref.shape
    half = len(gate_refs) // 2
    g_sb = jnp.concatenate([r[...] for r in gate_refs[:half]], axis=1)
    g_hg = jnp.concatenate([r[...] for r in gate_refs[half:]], axis=1)
    y_sb = jnp.dot(gs_ref[...], wsb_ref[...], preferred_element_type=F32)
    y_h = jnp.dot(gh_ref[...], whg_ref[...], preferred_element_type=F32)
    merged = jax.nn.sigmoid(g_sb) * y_sb + jax.nn.sigmoid(g_hg) * y_h
    upd = jnp.dot(merged.astype(BF16), wo_ref[...], preferred_element_type=F32)
    y_ref[...] = x_ref[...] + gate_ref[...] * upd.reshape(nb, tr, d)


def _out_call(gs, gh, gg, x, gate, wsb, whg, wo, nb, tr, *, gg_row0=0, gg_col0=0):
    n, t, d = x.shape
    tm = nb * tr
    nt = t // tr
    assert gg_row0 % tm == 0 and (nb == 1 or nt == 1) and gg_col0 % WIDTH == 0 and d % WIDTH == 0
    rowblk = lambda w, c, r0=0: pl.BlockSpec((tm, w), lambda i, j: (r0 // tm + i * nt + j, c))
    const = lambda a: pl.BlockSpec(a.shape, lambda i, j: (0, 0), pipeline_mode=pl.Buffered(1))
    n_gate = 2 * d // WIDTH
    return pl.pallas_call(
        _out_kernel,
        grid=(n // nb, nt),
        in_specs=[rowblk(WIDTH, 0), rowblk(WIDTH, 0)]
                 + [rowblk(WIDTH, gg_col0 // WIDTH + c, gg_row0) for c in range(n_gate)]
                 + [pl.BlockSpec((nb, tr, d), lambda i, j: (i, j, 0)),
                    pl.BlockSpec((nb, 1, d), lambda i, j: (i, 0, 0)),
                    const(wsb), const(whg), const(wo)],
        out_specs=pl.BlockSpec((nb, tr, d), lambda i, j: (i, j, 0)),
        out_shape=jax.ShapeDtypeStruct(x.shape, F32),
        compiler_params=_cparams("arbitrary", "arbitrary"),
        name="merge_out",
    )(gs, gh, *([gg] * n_gate), x, gate, wsb, whg, wo)


def _layer(x_p, x_s, mod_p, mod_s, p, s0_s, caches):
    n_p, t_p, d = x_p.shape
    n_s, t_s, _ = x_s.shape
    assert n_p == 1
    rows_p, rows_s = t_p, n_s * t_s
    tm = math.gcd(1024, rows_p, rows_s)
    w_in = p["w_in"]

    h, q_p = _prenorm_q_call(x_p, p["norm_gain"], mod_p[1], mod_p[0], w_in, p["q_gain"], tm=tm, extra_rows=rows_s)
    pn_tr = min(1024, t_s)
    pn_nb = max(1, min(n_s, 1024 // pn_tr))
    h = _prenorm_call(x_s, p["norm_gain"], mod_s[1], mod_s[0], pn_nb, pn_tr, h, rows_p)

    proj = functools.partial(_proj_call, h, w_in, tm=tm)
    prompt, sample = dict(row0=0, rows=rows_p), dict(row0=rows_p, rows=rows_s)
    q_s, k_s, kb_s, v_s, vb_s = _proj_qkv_call(h, w_in, p["q_gain"], p["k_gain"], **sample)
    k_p, kb_p = proj(1 * WIDTH, WIDTH, "norm_k", (p["k_gain"],), **prompt)
    v_p, vb_p = proj(2 * WIDTH, WIDTH, "copy2", **prompt)
    (z_sb,) = proj(3 * WIDTH, WIDTH, "silu")
    logf, k_h = proj(4 * WIDTH, WIDTH, "forget", (p["lb_raw"],))
    (i_h,) = proj(5 * WIDTH, WIDTH, "plain_bf16")
    (q_h,) = proj(6 * WIDTH, WIDTH, "silu")
    (zg,) = proj(7 * WIDTH, WIDTH + 2 * d, "plain")

    gs_p = _sb_prompt_call(q_p, kb_p, vb_p, z_sb)
    gs_s = _sb_sample_call(q_s, kb_s, vb_s, z_sb, caches[0], caches[1], z_row0=rows_p)
    hgrn = functools.partial(_hgrn_call, logf, q_h, k_h, i_h, zg, p["onorm_gain"])
    gh_p, s_p = hgrn(jnp.zeros((n_p, N_HEADS, HEAD_DIM, HEAD_DIM), F32), t=t_p, chunk=min(128, t_p), tile=1024,
                     heads=N_HEADS)
    gh_s, s_s = hgrn(s0_s, t=t_s, chunk=t_s, tile=t_s, heads=N_HEADS, row0=rows_p, streams=math.gcd(n_s, 4))
    out = functools.partial(_out_call, wsb=p["w_br_sb"], whg=p["w_br_hg"], wo=p["w_out"], gg_col0=WIDTH)
    y_p = out(gs_p, gh_p, zg, x_p, mod_p[2], nb=1, tr=min(256, t_p))
    y_s = out(gs_s, gh_s, zg, x_s, mod_s[2], nb=max(1, min(n_s, 256 // t_s)), tr=t_s, gg_row0=rows_p)
    heads5 = lambda a, n, t: a.reshape(1, n, t, N_HEADS, HEAD_DIM)
    return (y_p, y_s, heads5(k_p, n_p, t_p), heads5(v_p, n_p, t_p), s_p[None],
            heads5(k_s, n_s, t_s), heads5(v_s, n_s, t_s), s_s[None])


def kernel(x_prompt, x_sample, cache_sb_k, cache_sb_v, state_hgrn, c_prompt, c_sample, norm_gain, w_ada, b_ada, w_in, q_norm_gain, k_norm_gain, hgrn_lb_raw, hgrn_onorm_gain, w_branch_sb, w_branch_hgrn, w_out):
    assert w_in.shape[0] == 1, "single-layer trunk"
    n_p, t_p, d = x_prompt.shape
    n_s, t_s, _ = x_sample.shape
    past = cache_sb_k.shape[2]

    c_all = jnp.concatenate([c_prompt, c_sample], axis=0)
    pad = (-c_all.shape[0]) % 8
    c_all = jnp.pad(c_all, ((0, pad), (0, 0)))
    mod = _ada_call(c_all, w_ada[0], b_ada[0].reshape(1, 3 * d))
    mods = lambda lo, hi: tuple(mod[lo:hi, i * d:(i + 1) * d].reshape(hi - lo, 1, d) for i in range(3))

    p = {
        "norm_gain": norm_gain[0].reshape(1, 1, d),
        "w_in": w_in[0],
        "q_gain": q_norm_gain[0].reshape(1, HEAD_DIM),
        "k_gain": k_norm_gain[0].reshape(1, HEAD_DIM),
        "lb_raw": hgrn_lb_raw,
        "onorm_gain": hgrn_onorm_gain[0].reshape(N_HEADS, 1, HEAD_DIM),
        "w_br_sb": w_branch_sb[0].astype(BF16),
        "w_br_hg": w_branch_hgrn[0].astype(BF16),
        "w_out": w_out[0].astype(BF16),
    }

    return _layer(x_prompt, x_sample, mods(0, n_p), mods(n_p, n_p + n_s), p, state_hgrn[0],
                  (cache_sb_k.reshape(n_s, past * N_HEADS, HEAD_DIM), cache_sb_v.reshape(n_s, past * N_HEADS, HEAD_DIM)))
```
